```python
import math
import jax
import jax.numpy as jnp
from jax import lax
import numpy as np


D_MODEL = 1024
BATCH = 4
SEQ = 4096
DEPTH = 2

HEAD_DIM = 64
N_HEADS = D_MODEL // HEAD_DIM
D_MIX = N_HEADS * HEAD_DIM
H_FOX = 3 * N_HEADS // 8
H_GDN = 3 * N_HEADS // 8
H_GLA = N_HEADS - H_FOX - H_GDN
FOX_DIM = H_FOX * HEAD_DIM
GDN_DIM = H_GDN * HEAD_DIM
GLA_DK = HEAD_DIM // 2
GLA_DV = HEAD_DIM
GLA_K_DIM = H_GLA * GLA_DK
GLA_V_DIM = H_GLA * GLA_DV
GLA_RANK = 16
GLA_TAU = 16.0
CONV_K = 4
Q_BLOCK = 128
CHUNK = 64
N_GROUPS = 4
EXPERTS_PER_GROUP = 8
N_EXPERTS = N_GROUPS * EXPERTS_PER_GROUP
TOP_K = 2
D_EXPERT = D_MODEL // 2
MOE_BLOCK = 128
RMS_EPS = 1e-6
N_IN = 3 * FOX_DIM + H_FOX + 4 * GDN_DIM + 2 * H_GDN + 2 * GLA_K_DIM + 2 * GLA_V_DIM + GLA_RANK
F32 = jnp.float32

kernel_name = 'hybrid_fox_gdn_gla_hier_moe'


def rms_norm(x, gain):
    xf = x.astype(F32)
    y = xf * lax.rsqrt(jnp.mean(xf * xf, axis=-1, keepdims=True) + RMS_EPS)
    return (y * gain.astype(F32)).astype(x.dtype)


def l2_normalize(x):
    xf = x.astype(F32)
    return (xf * lax.rsqrt(jnp.sum(xf * xf, axis=-1, keepdims=True) + RMS_EPS)).astype(x.dtype)


def causal_depthwise_conv(x, w):
    c = x.shape[-1]
    return lax.conv_general_dilated(x, w[:, None, :].astype(x.dtype), window_strides=(1,),
                                    padding=[(CONV_K - 1, 0)],
                                    dimension_numbers=('NWC', 'WIO', 'NWC'),
                                    feature_group_count=c)


def split_columns(proj):
    sizes = [FOX_DIM, FOX_DIM, FOX_DIM, H_FOX,
             3 * GDN_DIM, GDN_DIM, H_GDN, H_GDN,
             GLA_K_DIM, GLA_K_DIM, GLA_V_DIM, GLA_V_DIM, GLA_RANK]
    idx = np.cumsum(sizes)[:-1].tolist()
    return jnp.split(proj, idx, axis=-1)


def fox_branch(q, k, v, f_logit, q_gain, k_gain, f_bias, o_gain):
    b, s, _ = q.shape
    q = rms_norm(q.reshape(b, s, H_FOX, HEAD_DIM), q_gain)
    k = rms_norm(k.reshape(b, s, H_FOX, HEAD_DIM), k_gain)
    v = v.reshape(b, s, H_FOX, HEAD_DIM)
    log_f = jax.nn.log_sigmoid(f_logit.astype(F32) + f_bias.astype(F32))
    cum = jnp.cumsum(log_f, axis=1).transpose(0, 2, 1)
    nq = s // Q_BLOCK
    q_blocks = q.reshape(b, nq, Q_BLOCK, H_FOX, HEAD_DIM).transpose(1, 0, 2, 3, 4)
    c_blocks = cum.reshape(b, H_FOX, nq, Q_BLOCK).transpose(2, 0, 1, 3)
    key_pos = jnp.arange(s)
    scale = HEAD_DIM ** -0.5

    def attend(args):
        q_i, c_i, blk = args
        logits = (jnp.einsum('bqhd,bkhd->bhqk', q_i, k).astype(F32) * scale
                  + (c_i[..., :, None] - cum[..., None, :]))
        query_pos = blk * Q_BLOCK + jnp.arange(Q_BLOCK)
        logits = jnp.where(key_pos[None, :] <= query_pos[:, None], logits, -jnp.inf)
        p = jax.nn.softmax(logits, axis=-1).astype(v.dtype)
        return jnp.einsum('bhqk,bkhd->bqhd', p, v)

    o = lax.map(attend, (q_blocks, c_blocks, jnp.arange(nq)))
    o = o.transpose(1, 0, 2, 3, 4).reshape(b, s, H_FOX, HEAD_DIM)
    return rms_norm(o, o_gain).reshape(b, s, FOX_DIM)


def gated_delta_rule(q, k, v, g, beta):
    b, s, h, dk = q.shape
    dv = v.shape[-1]
    n = s // CHUNK

    def chunks(t):
        return t.astype(F32).reshape(b, n, CHUNK, h, t.shape[-1]).transpose(0, 3, 1, 2, 4)

    qc, kc, vc = chunks(q), chunks(k), chunks(v)
    gc = g.reshape(b, n, CHUNK, h).transpose(0, 3, 1, 2)
    bc = beta.reshape(b, n, CHUNK, h).transpose(0, 3, 1, 2)
    decay = jnp.cumsum(gc, axis=-1)
    causal = jnp.tril(jnp.ones((CHUNK, CHUNK), bool))
    strict = jnp.tril(jnp.ones((CHUNK, CHUNK), bool), k=-1)
    diff = decay[..., :, None] - decay[..., None, :]
    gamma = jnp.where(causal, jnp.exp(jnp.where(causal, diff, 0.0)), 0.0)
    kb = kc * bc[..., None]
    a_strict = jnp.where(strict, jnp.einsum('bhnid,bhnjd->bhnij', kb, kc) * gamma, 0.0)
    lhs = a_strict + jnp.eye(CHUNK, dtype=F32)
    rhs = jnp.concatenate([vc * bc[..., None], kb * jnp.exp(decay)[..., None]], axis=-1)
    sol = lax.linalg.triangular_solve(lhs, rhs, left_side=True, lower=True)
    u, w = sol[..., :dv], sol[..., dv:]
    intra = jnp.einsum('bhnid,bhnjd->bhnij', qc, kc) * gamma
    q_dec = qc * jnp.exp(decay)[..., None]
    k_dec = kc * jnp.exp(decay[..., -1:] - decay)[..., None]
    last = jnp.exp(decay[..., -1])

    def step(state, xs):
        u_i, w_i, qd_i, kd_i, a_i, last_i = xs
        v_new = u_i - jnp.einsum('bhcd,bhde->bhce', w_i, state)
        o = jnp.einsum('bhcd,bhde->bhce', qd_i, state) + jnp.einsum('bhij,bhje->bhie', a_i, v_new)
        state = state * last_i[..., None, None] + jnp.einsum('bhcd,bhce->bhde', kd_i, v_new)
        return state, o

    xs = tuple(jnp.moveaxis(t, 2, 0) for t in (u, w, q_dec, k_dec, intra, last))
    _, o = lax.scan(step, jnp.zeros((b, h, dk, dv), F32), xs)
    return o.transpose(1, 0, 3, 2, 4).reshape(b, s, h, dv)


def gdn_branch(qkv, gate, a, beta_logit, conv_w, a_log, dt_bias, o_gain):
    b, s, _ = qkv.shape
    qkv = jax.nn.silu(causal_depthwise_conv(qkv, conv_w))
    q, k, v = jnp.split(qkv, 3, axis=-1)
    q = l2_normalize(q.reshape(b, s, H_GDN, HEAD_DIM)) * (HEAD_DIM ** -0.5)
    k = l2_normalize(k.reshape(b, s, H_GDN, HEAD_DIM))
    v = v.reshape(b, s, H_GDN, HEAD_DIM)
    g = -jnp.exp(a_log.astype(F32)) * jax.nn.softplus(a.astype(F32) + dt_bias.astype(F32))
    beta = jax.nn.sigmoid(beta_logit.astype(F32))
    o = gated_delta_rule(q, k, v, g, beta).astype(qkv.dtype)
    o = rms_norm(o, o_gain) * jax.nn.silu(gate.reshape(b, s, H_GDN, HEAD_DIM))
    return o.reshape(b, s, GDN_DIM)


def gla_chunked(q, k, v, log_a):
    b, s, h, dk = q.shape
    dv = v.shape[-1]
    n = s // CHUNK

    def chunks(t):
        return t.astype(F32).reshape(b, n, CHUNK, h, t.shape[-1]).transpose(1, 0, 3, 2, 4)

    qc, kc, vc = chunks(q), chunks(k), chunks(v)
    bcum = jnp.cumsum(chunks(log_a), axis=3)
    causal = jnp.tril(jnp.ones((CHUNK, CHUNK), bool))[..., None]

    def step(state, xs):
        q_i, k_i, v_i, b_i = xs
        diff = b_i[:, :, :, None, :] - b_i[:, :, None, :, :]
        dec = jnp.where(causal, jnp.exp(jnp.where(causal, diff, 0.0)), 0.0)
        scores = jnp.einsum('bhid,bhjd,bhijd->bhij', q_i, k_i, dec)
        o = (jnp.einsum('bhid,bhde->bhie', q_i * jnp.exp(b_i), state)
             + jnp.einsum('bhij,bhje->bhie', scores, v_i))
        b_last = b_i[:, :, -1:, :]
        state = (state * jnp.exp(b_last[:, :, 0, :, None])
                 + jnp.einsum('bhjd,bhje->bhde', k_i * jnp.exp(b_last - b_i), v_i))
        return state, o

    _, o = lax.scan(step, jnp.zeros((b, h, dk, dv), F32), (qc, kc, vc, bcum))
    return o.transpose(1, 0, 3, 2, 4).reshape(b, s, h, dv)


def gla_branch(q, k, v, r, a1, w_a2, b_a, o_gain):
    b, s, _ = q.shape
    q = q.reshape(b, s, H_GLA, GLA_DK) * (GLA_DK ** -0.5)
    k = k.reshape(b, s, H_GLA, GLA_DK)
    v = v.reshape(b, s, H_GLA, GLA_DV)
    log_a = jax.nn.log_sigmoid((a1 @ w_a2 + b_a).astype(F32)) / GLA_TAU
    log_a = log_a.reshape(b, s, H_GLA, GLA_DK)
    o = gla_chunked(q, k, v, log_a).astype(q.dtype)
    o = rms_norm(o, o_gain) * jax.nn.silu(r.reshape(b, s, H_GLA, GLA_DV))
    return o.reshape(b, s, GLA_V_DIM)


def token_mixer(h, w_in, fox_q_norm, fox_k_norm, fox_f_bias, fox_o_norm,
                gdn_conv, gdn_a_log, gdn_dt_bias, gdn_o_norm,
                gla_w_a2, gla_b_a, gla_o_norm, w_out):
    proj = h @ w_in
    (fq, fk, fv, ff, gqkv, ggate, ga, gb, lq, lk, lv, lr, la1) = split_columns(proj)
    o_fox = fox_branch(fq, fk, fv, ff, fox_q_norm, fox_k_norm, fox_f_bias, fox_o_norm)
    o_gdn = gdn_branch(gqkv, ggate, ga, gb, gdn_conv, gdn_a_log, gdn_dt_bias, gdn_o_norm)
    o_gla = gla_branch(lq, lk, lv, lr, la1, gla_w_a2, gla_b_a, gla_o_norm)
    return jnp.concatenate([o_fox, o_gdn, o_gla], axis=-1) @ w_out


def expert_dispatch(hf, expert_id, gate, w_g, w_u, w_d):
    n, d = hf.shape
    nk = n * TOP_K
    flat_e = expert_id.reshape(-1)
    flat_tok = jnp.repeat(jnp.arange(n, dtype=jnp.int32), TOP_K)
    flat_g = gate.reshape(-1).astype(hf.dtype)
    order = jnp.argsort(flat_e)
    e_sorted, tok_sorted, g_sorted = flat_e[order], flat_tok[order], flat_g[order]
    counts = jnp.bincount(flat_e, length=N_EXPERTS)
    starts = jnp.cumsum(counts) - counts
    padded = (counts + MOE_BLOCK - 1) // MOE_BLOCK * MOE_BLOCK
    pends = jnp.cumsum(padded)
    pstarts = pends - padded
    dest = pstarts[e_sorted] + (jnp.arange(nk) - starts[e_sorted])
    n_rows = -(-nk // MOE_BLOCK) * MOE_BLOCK + N_EXPERTS * MOE_BLOCK
    n_blocks = n_rows // MOE_BLOCK
    row_tok = jnp.full((n_rows,), n, jnp.int32).at[dest].set(tok_sorted)
    row_gate = jnp.zeros((n_rows,), hf.dtype).at[dest].set(g_sorted)
    block_e = jnp.minimum(jnp.searchsorted(pends, jnp.arange(n_blocks) * MOE_BLOCK, side='right'),
                          N_EXPERTS - 1)
    h_pad = jnp.concatenate([hf, jnp.zeros((1, d), hf.dtype)], axis=0)
    xb = h_pad[row_tok].reshape(n_blocks, MOE_BLOCK, d)

    def run_block(args):
        x_blk, e = args
        return (jax.nn.silu(x_blk @ w_g[e]) * (x_blk @ w_u[e])) @ w_d[e]

    yb = lax.map(run_block, (xb, block_e)).reshape(n_rows, d)
    y = jnp.zeros((n + 1, d), hf.dtype).at[row_tok].add(yb * row_gate[:, None])
    return y[:n]


def hierarchical_moe(h, w_rg, b_rg, w_re, b_re, w_g, w_u, w_d):
    b, s, d = h.shape
    n = b * s
    hf = h.reshape(n, d)
    group_prob = jax.nn.softmax((hf @ w_rg).astype(F32) + b_rg.astype(F32), axis=-1)
    group_p, group_idx = lax.top_k(group_prob, 1)
    expert_logits = ((hf @ w_re).astype(F32) + b_re.astype(F32)).reshape(n, N_GROUPS, EXPERTS_PER_GROUP)
    sel = jnp.broadcast_to(group_idx[:, :, None], (n, 1, EXPERTS_PER_GROUP))
    in_group = jnp.take_along_axis(expert_logits, sel, axis=1)[:, 0]
    top_p, top_idx = lax.top_k(jax.nn.softmax(in_group, axis=-1), TOP_K)
    gate = group_p * top_p / jnp.sum(top_p, axis=-1, keepdims=True)
    expert_id = group_idx * EXPERTS_PER_GROUP + top_idx
    return expert_dispatch(hf, expert_id, gate, w_g, w_u, w_d).reshape(b, s, d)


def setup_inputs(seed: int = 0) -> dict:
    key = jax.random.key(seed)
    ks = jax.random.split(key, 24)
    L = DEPTH

    def normal(k, shape, scale):
        return jax.random.normal(k, shape, F32) * scale

    def gain(k, shape):
        return 1.0 + 0.1 * jax.random.normal(k, shape, F32)

    dt = jnp.exp(jax.random.uniform(ks[9], (L, H_GDN), F32, math.log(1e-3), math.log(1e-1)))
    return {
        'x': normal(ks[0], (BATCH, SEQ, D_MODEL), 1.0),
        'attn_norm': gain(ks[1], (L, D_MODEL)),
        'w_in': normal(ks[2], (L, D_MODEL, N_IN), D_MODEL ** -0.5),
        'fox_q_norm': gain(ks[3], (L, HEAD_DIM)),
        'fox_k_norm': gain(ks[4], (L, HEAD_DIM)),
        'fox_f_bias': jax.random.uniform(ks[5], (L, H_FOX), F32, 1.0, 4.0),
        'fox_o_norm': gain(ks[6], (L, HEAD_DIM)),
        'gdn_conv': normal(ks[7], (L, CONV_K, 3 * GDN_DIM), CONV_K ** -0.5),
        'gdn_a_log': jnp.log(jax.random.uniform(ks[8], (L, H_GDN), F32, 1.0, 16.0)),
        'gdn_dt_bias': dt + jnp.log(-jnp.expm1(-dt)),
        'gdn_o_norm': gain(ks[10], (L, HEAD_DIM)),
        'gla_w_a2': normal(ks[11], (L, GLA_RANK, GLA_K_DIM), GLA_RANK ** -0.5),
        'gla_b_a': normal(ks[12], (L, GLA_K_DIM), 0.1),
        'gla_o_norm': gain(ks[13], (L, GLA_DV)),
        'w_out': normal(ks[14], (L, D_MIX, D_MODEL), D_MIX ** -0.5),
        'ffn_norm': gain(ks[15], (L, D_MODEL)),
        'w_router_group': normal(ks[16], (L, D_MODEL, N_GROUPS), D_MODEL ** -0.5),
        'b_router_group': normal(ks[17], (L, N_GROUPS), 0.01),
        'w_router_expert': normal(ks[18], (L, D_MODEL, N_EXPERTS), D_MODEL ** -0.5),
        'b_router_expert': normal(ks[19], (L, N_EXPERTS), 0.01),
        'w_expert_gate': normal(ks[20], (L, N_EXPERTS, D_MODEL, D_EXPERT), D_MODEL ** -0.5),
        'w_expert_up': normal(ks[21], (L, N_EXPERTS, D_MODEL, D_EXPERT), D_MODEL ** -0.5),
        'w_expert_down': normal(ks[22], (L, N_EXPERTS, D_EXPERT, D_MODEL), D_EXPERT ** -0.5),
    }


def reference(x, attn_norm, w_in, fox_q_norm, fox_k_norm, fox_f_bias, fox_o_norm,
              gdn_conv, gdn_a_log, gdn_dt_bias, gdn_o_norm,
              gla_w_a2, gla_b_a, gla_o_norm, w_out,
              ffn_norm, w_router_group, b_router_group, w_router_expert, b_router_expert,
              w_expert_gate, w_expert_up, w_expert_down):
    for layer in range(DEPTH):
        h = rms_norm(x, attn_norm[layer])
        x = x + token_mixer(h, w_in[layer], fox_q_norm[layer], fox_k_norm[layer],
                            fox_f_bias[layer], fox_o_norm[layer],
                            gdn_conv[layer], gdn_a_log[layer], gdn_dt_bias[layer], gdn_o_norm[layer],
                            gla_w_a2[layer], gla_b_a[layer], gla_o_norm[layer], w_out[layer])
        h = rms_norm(x, ffn_norm[layer])
        x = x + hierarchical_moe(h, w_router_group[layer], b_router_group[layer],
                                 w_router_expert[layer], b_router_expert[layer],
                                 w_expert_gate[layer], w_expert_up[layer], w_expert_down[layer])
    return x
```

```python
import functools

import jax
import jax.numpy as jnp
from jax import lax
from jax.experimental import pallas as pl
from jax.experimental.pallas import tpu as pltpu

F32 = jnp.float32
BF16 = jnp.bfloat16

HEAD_DIM = 64
H_FOX = 6
H_GDN = 6
H_GLA = 4
FOX_DIM = H_FOX * HEAD_DIM
GDN_DIM = H_GDN * HEAD_DIM
GLA_DK = 32
GLA_K_DIM = H_GLA * GLA_DK
GLA_V_DIM = H_GLA * HEAD_DIM
GLA_RANK = 16
GLA_TAU = 16.0
CONV_K = 4
CHUNK = 64
N_GROUPS = 4
EXPERTS_PER_GROUP = 8
N_EXPERTS = N_GROUPS * EXPERTS_PER_GROUP
TOP_K = 2
RMS_EPS = 1e-6

LANES = 128
SUBLANES = 8
VMEM_LIMIT = 56 * 1024 * 1024

_SIZES = [FOX_DIM, FOX_DIM, FOX_DIM, H_FOX, 3 * GDN_DIM, GDN_DIM, H_GDN, H_GDN,
          GLA_K_DIM, GLA_K_DIM, GLA_V_DIM, GLA_V_DIM, GLA_RANK]
_OFFS = [sum(_SIZES[:i]) for i in range(len(_SIZES) + 1)]
_WIDE = [0, 1, 2, 4, 5, 8, 9, 10, 11]
_WIDE_OFF = [0]
for _g in _WIDE:
    _WIDE_OFF.append(_WIDE_OFF[-1] + _SIZES[_g])
_SM_F, _SM_A, _SM_B, _SM_A1 = 0, 8, 16, 32
_R_GROUP, _R_EXPERT = 0, 32


def _dot(a, b):
    return jnp.dot(a, b, preferred_element_type=F32)


def _dot_nt(a, b):
    return lax.dot_general(a, b, (((1,), (1,)), ((), ())), preferred_element_type=F32)


def _dot_tn(a, b):
    return lax.dot_general(a, b, (((0,), (0,)), ((), ())), preferred_element_type=F32)


def _dot_f32(a, b):
    return jnp.dot(a, b, preferred_element_type=F32, precision=lax.Precision.HIGHEST)


def _seg_sum(sq, segm):
    hi = sq.astype(BF16)
    lo = (sq - hi.astype(F32)).astype(BF16)
    return _dot(hi, segm) + _dot(lo, segm)


def _sigmoid(x):
    return 1.0 / (1.0 + jnp.exp(-x))


def _softplus(x):
    return jnp.maximum(x, 0.0) + jnp.log1p(jnp.exp(-jnp.abs(x)))


def _log_sigmoid(x):
    return -_softplus(-x)


def _lane_cumsum(x, seg):
    lane = lax.broadcasted_iota(jnp.int32, x.shape, 1)
    pos = lane & (seg - 1)
    s = 1
    while s < seg:
        x = x + jnp.where(pos >= s, pltpu.roll(x, s, 1), 0.0)
        s *= 2
    return x


def _seg_matrix(n):
    i = jnp.arange(n) // HEAD_DIM
    return (i[:, None] == i[None, :]).astype(BF16)


def _proj_kernel(x_ref, gain_ref, wbig_ref, wsm_ref, segm_ref, qg_ref, kg_ref,
                 fb_ref, alog_ref, dtb_ref, wa2_ref, ba_ref,
                 fq_o, fk_o, fv_o, gqkv_o, ggate_o, lq_o, lk_o, lv_o, lr_o, la_o,
                 c_o, g_o, beta_o, carry_ref):
    tm = x_ref.shape[1]
    x = x_ref[0]
    ms = jnp.mean(x * x, axis=-1, keepdims=True)
    hb = (x * lax.rsqrt(ms + RMS_EPS) * gain_ref[...]).astype(BF16)

    def wide(i):
        return _dot(hb, wbig_ref[:, _WIDE_OFF[i]:_WIDE_OFF[i + 1]])

    segm = segm_ref[...]
    q = wide(0)
    q = q * lax.rsqrt(_seg_sum(q * q, segm) * (1.0 / HEAD_DIM) + RMS_EPS) * qg_ref[...]
    fq_o[0] = q.astype(BF16)
    k = wide(1)
    k = k * lax.rsqrt(_seg_sum(k * k, segm) * (1.0 / HEAD_DIM) + RMS_EPS) * kg_ref[...]
    fk_o[0] = k.astype(BF16)
    fv_o[0] = wide(2).astype(BF16)
    gqkv_o[0] = wide(3).astype(BF16)
    ggate_o[0] = wide(4).astype(BF16)
    lq_o[0] = wide(5).astype(BF16)
    lk_o[0] = wide(6).astype(BF16)
    lv_o[0] = wide(7).astype(BF16)
    lr_o[0] = wide(8).astype(BF16)

    sm = _dot(hb, wsm_ref[...])
    la_logit = _dot_f32(sm, wa2_ref[...]) + ba_ref[...]
    la_o[0] = _log_sigmoid(la_logit) * (1.0 / GLA_TAU)

    smt = sm.T
    log_f = _log_sigmoid(smt[_SM_F:_SM_F + 8] + fb_ref[...])

    @pl.when(pl.program_id(1) == 0)
    def _():
        carry_ref[...] = jnp.zeros_like(carry_ref)

    cum = _lane_cumsum(log_f, tm) + carry_ref[:, 0:1]
    c_o[0] = cum
    carry_ref[...] = jnp.broadcast_to(cum[:, tm - 1:tm], carry_ref.shape)
    g_o[0] = -jnp.exp(alog_ref[...]) * _softplus(smt[_SM_A:_SM_A + 8] + dtb_ref[...])
    beta_o[0] = _sigmoid(smt[_SM_B:_SM_B + 8])


def _proj_call(x, gain, wbig, wsm, segm, qg, kg, fb, alog, dtb, wa2, ba, tm):
    b, s, d = x.shape
    tm = min(tm, s)
    const = lambda shape: pl.BlockSpec(shape, lambda i, j: (0,) * len(shape))
    tok = lambda w: pl.BlockSpec((1, tm, w), lambda i, j: (i, j, 0))
    row = pl.BlockSpec((1, 8, tm), lambda i, j: (i, 0, j))
    widths = [_SIZES[g] for g in _WIDE]
    out_shape = ([jax.ShapeDtypeStruct((b, s, w), BF16) for w in widths]
                 + [jax.ShapeDtypeStruct((b, s, LANES), F32)]
                 + [jax.ShapeDtypeStruct((b, 8, s), F32)] * 3)
    out_specs = [tok(w) for w in widths] + [tok(LANES)] + [row] * 3
    return pl.pallas_call(
        _proj_kernel,
        grid=(b, s // tm),
        in_specs=[tok(d), const((1, d)), const(wbig.shape), const(wsm.shape), const(segm.shape),
                  const(qg.shape), const(kg.shape), const((8, 1)), const((8, 1)), const((8, 1)),
                  const(wa2.shape), const(ba.shape)],
        out_specs=out_specs,
        out_shape=out_shape,
        scratch_shapes=[pltpu.VMEM((8, LANES), F32)],
        compiler_params=pltpu.CompilerParams(
            dimension_semantics=("arbitrary", "arbitrary"), vmem_limit_bytes=VMEM_LIMIT),
        name="proj",
    )(x, gain, wbig, wsm, segm, qg, kg, fb, alog, dtb, wa2, ba)


def _fox_kernel(q_ref, k_ref, v_ref, c_ref, o_ref, m_ref, l_ref, acc_ref):
    tq = q_ref.shape[1]
    qi = pl.program_id(2)
    q = q_ref[0]
    lane = lax.broadcasted_iota(jnp.int32, q.shape, 1)
    first = lane < HEAD_DIM
    zero = jnp.zeros_like(q)
    qs = (jnp.where(first, q, zero), jnp.where(first, zero, q))
    q0 = pl.multiple_of(qi * tq, tq)
    cqt = c_ref[0, 0, :, pl.ds(q0, tq)].T
    cq = (cqt[:, 0:1], cqt[:, 1:2])
    m_ref[...] = jnp.full_like(m_ref, -jnp.inf)
    l_ref[...] = jnp.zeros_like(l_ref)
    acc_ref[...] = jnp.zeros_like(acc_ref)
    rows = lax.broadcasted_iota(jnp.int32, (tq, tq), 0)
    cols = lax.broadcasted_iota(jnp.int32, (tq, tq), 1)

    def step(j, masked):
        k0 = pl.multiple_of(j * tq, tq)
        k = k_ref[0, pl.ds(k0, tq), :]
        v = v_ref[0, pl.ds(k0, tq), :]
        ck = c_ref[0, 0, :, pl.ds(k0, tq)]
        for hh in range(2):
            s = _dot_nt(qs[hh], k) + cq[hh] - ck[hh:hh + 1, :]
            if masked:
                s = jnp.where(cols <= rows, s, -jnp.inf)
            m_prev = m_ref[hh]
            m_new = jnp.maximum(m_prev, jnp.max(s, axis=1, keepdims=True))
            alpha = jnp.exp(m_prev - m_new)
            p = jnp.exp(s - m_new)
            l_ref[hh] = alpha * l_ref[hh] + jnp.sum(p, axis=1, keepdims=True)
            acc_ref[hh] = alpha * acc_ref[hh] + _dot(p.astype(BF16), v)
            m_ref[hh] = m_new

    step(qi, True)

    def body(j, carry):
        step(j, False)
        return carry

    lax.fori_loop(0, qi, body, 0)
    o = jnp.where(first, acc_ref[0] / l_ref[0], acc_ref[1] / l_ref[1])
    o_ref[0] = o.astype(o_ref.dtype)


def _fox_call(q, k, v, c4, tq):
    b, s, _ = q.shape
    tq = min(tq, s)
    npair = H_FOX // 2
    return pl.pallas_call(
        _fox_kernel,
        grid=(b, npair, s // tq),
        in_specs=[pl.BlockSpec((1, tq, LANES), lambda i, p, j: (i, j, p)),
                  pl.BlockSpec((1, s, LANES), lambda i, p, j: (i, 0, p)),
                  pl.BlockSpec((1, s, LANES), lambda i, p, j: (i, 0, p)),
                  pl.BlockSpec((1, 1, 8, s), lambda i, p, j: (i, p, 0, 0))],
        out_specs=pl.BlockSpec((1, tq, LANES), lambda i, p, j: (i, j, p)),
        out_shape=jax.ShapeDtypeStruct((b, s, FOX_DIM), BF16),
        scratch_shapes=[pltpu.VMEM((2, tq, 1), F32), pltpu.VMEM((2, tq, 1), F32),
                        pltpu.VMEM((2, tq, LANES), F32)],
        compiler_params=pltpu.CompilerParams(
            dimension_semantics=("arbitrary", "arbitrary", "arbitrary"),
            vmem_limit_bytes=VMEM_LIMIT),
        name="fox",
    )(q, k, v, c4)


def _unit_lower_inverse(a_strict, eye):
    n = (-a_strict).astype(BF16)
    t = eye + n.astype(F32)
    size = 1
    while 2 * size < CHUNK:
        n2 = _dot(n, n)
        n = n2.astype(BF16)
        t = t + _dot(t.astype(BF16), n)
        size *= 2
    return t


def _gdn_kernel(qkv_ref, convw_ref, g_ref, beta_ref, segm_ref, o_ref, state_ref, xext_ref):
    tc = qkv_ref.shape[1]
    pad = SUBLANES

    @pl.when(pl.program_id(1) == 0)
    def _():
        state_ref[...] = jnp.zeros_like(state_ref)
        xext_ref[0:pad, :] = jnp.zeros((pad, xext_ref.shape[1]), F32)

    x = qkv_ref[0].astype(F32)
    xext_ref[pad:pad + tc, :] = x
    y = convw_ref[0:1, :] * xext_ref[pad - 3:pad - 3 + tc, :]
    for i in range(1, CONV_K):
        y = y + convw_ref[i:i + 1, :] * xext_ref[pad - 3 + i:pad - 3 + i + tc, :]
    xext_ref[0:pad, :] = x[tc - pad:tc, :]
    y = y * _sigmoid(y)

    segm = segm_ref[...]
    q = y[:, 0:GDN_DIM]
    k = y[:, GDN_DIM:2 * GDN_DIM]
    v = y[:, 2 * GDN_DIM:3 * GDN_DIM]
    q = q * lax.rsqrt(_seg_sum(q * q, segm) + RMS_EPS) * (HEAD_DIM ** -0.5)
    k = k * lax.rsqrt(_seg_sum(k * k, segm) + RMS_EPS)

    dec_row = _lane_cumsum(g_ref[0], CHUNK)
    dec_col = dec_row.T
    beta_col = beta_ref[0].T

    ri = lax.broadcasted_iota(jnp.int32, (CHUNK, CHUNK), 0)
    ci = lax.broadcasted_iota(jnp.int32, (CHUNK, CHUNK), 1)
    causal = ci <= ri
    strict = ci < ri
    eye = (ci == ri).astype(F32)

    states = [state_ref[h] for h in range(H_GDN)]
    for c in range(tc // CHUNK):
        r0, r1 = c * CHUNK, (c + 1) * CHUNK
        outs = []
        for h in range(H_GDN):
            l0, l1 = h * HEAD_DIM, (h + 1) * HEAD_DIM
            dcol = dec_col[r0:r1, h:h + 1]
            drow = dec_row[h:h + 1, r0:r1]
            bcol = beta_col[r0:r1, h:h + 1]
            qh, kh, vh = q[r0:r1, l0:l1], k[r0:r1, l0:l1], v[r0:r1, l0:l1]
            gamma = jnp.where(causal, jnp.exp(jnp.where(causal, dcol - drow, 0.0)), 0.0)
            edec = jnp.exp(dcol)
            kb = kh * bcol
            khb = kh.astype(BF16)
            a = jnp.where(strict, _dot_nt(kb.astype(BF16), khb) * gamma, 0.0)
            t = _unit_lower_inverse(a, eye)
            rhs = jnp.concatenate([vh * bcol, kb * edec], axis=1)
            sol = _dot(t.astype(BF16), rhs.astype(BF16))
            u, w = sol[:, :HEAD_DIM], sol[:, HEAD_DIM:]
            intra = _dot_nt(qh.astype(BF16), khb) * gamma
            sb = states[h].astype(BF16)
            v_new = u - _dot(w.astype(BF16), sb)
            vnb = v_new.astype(BF16)
            o = _dot((qh * edec).astype(BF16), sb) + _dot(intra.astype(BF16), vnb)
            dlast = dcol[CHUNK - 1:CHUNK, :]
            kd = kh * jnp.exp(dlast - dcol)
            states[h] = states[h] * jnp.exp(dlast) + _dot_tn(kd.astype(BF16), vnb)
            outs.append(o)
        o_ref[0, r0:r1, :] = jnp.concatenate(outs, axis=1).astype(o_ref.dtype)
    for h in range(H_GDN):
        state_ref[h] = states[h]


def _gdn_call(qkv, convw, g, beta, segm, tc):
    b, s, w = qkv.shape
    tc = min(tc, s)
    return pl.pallas_call(
        _gdn_kernel,
        grid=(b, s // tc),
        in_specs=[pl.BlockSpec((1, tc, w), lambda i, j: (i, j, 0)),
                  pl.BlockSpec(convw.shape, lambda i, j: (0, 0)),
                  pl.BlockSpec((1, 8, tc), lambda i, j: (i, 0, j)),
                  pl.BlockSpec((1, 8, tc), lambda i, j: (i, 0, j)),
                  pl.BlockSpec(segm.shape, lambda i, j: (0, 0))],
        out_specs=pl.BlockSpec((1, tc, GDN_DIM), lambda i, j: (i, j, 0)),
        out_shape=jax.ShapeDtypeStruct((b, s, GDN_DIM), BF16),
        scratch_shapes=[pltpu.VMEM((H_GDN, HEAD_DIM, HEAD_DIM), F32),
                        pltpu.VMEM((tc + SUBLANES, w), F32)],
        compiler_params=pltpu.CompilerParams(
            dimension_semantics=("arbitrary", "arbitrary"), vmem_limit_bytes=VMEM_LIMIT),
        name="gdn",
    )(qkv, convw, g, beta, segm)


def _gla_kernel(q_ref, k_ref, v_ref, la_ref, tri_ref, mh_ref, mt_ref, o_ref,
                st_ref, kf_ref, bq_ref, vf_ref):
    tc = q_ref.shape[1]

    @pl.when(pl.program_id(1) == 0)
    def _():
        st_ref[...] = jnp.zeros_like(st_ref)

    row = lax.broadcasted_iota(jnp.int32, (CHUNK, GLA_K_DIM), 0)
    mh = mh_ref[...]
    for c in range(tc // CHUNK):
        r0, r1 = c * CHUNK, (c + 1) * CHUNK
        bq = _dot_f32(tri_ref[...], la_ref[0, r0:r1, :])
        q = q_ref[0, r0:r1, :].astype(F32) * (GLA_DK ** -0.5)
        k = k_ref[0, r0:r1, :].astype(F32)
        v = v_ref[0, r0:r1, :].astype(F32)
        kf_ref[...] = k
        bq_ref[...] = bq
        vf_ref[...] = v
        st = st_ref[...]
        o_inter = _dot_nt((q * jnp.exp(bq)).astype(BF16), st.astype(BF16))

        def body(j, acc):
            kj = kf_ref[pl.ds(j, 1), :]
            bj = bq_ref[pl.ds(j, 1), :]
            vj = vf_ref[pl.ds(j, 1), :]
            e = jnp.where(row >= j, jnp.exp(jnp.minimum(bq - bj, 0.0)), 0.0) * (q * kj)
            return acc + _dot(e.astype(BF16), mh) * vj

        intra = lax.fori_loop(0, CHUNK, body, jnp.zeros((CHUNK, GLA_V_DIM), F32), unroll=4)
        o_ref[0, r0:r1, :] = (o_inter + intra).astype(o_ref.dtype)
        blast = bq[CHUNK - 1:CHUNK, :]
        kd = k * jnp.exp(blast - bq)
        upd = _dot_tn(v.astype(BF16), kd.astype(BF16))
        st_ref[...] = (st * jnp.exp(blast) + upd) * mt_ref[...]


def _gla_call(q, k, v, la, tc):
    b, s, _ = q.shape
    tc = min(tc, s)
    tri = (jnp.arange(CHUNK)[:, None] >= jnp.arange(CHUNK)[None, :]).astype(F32)
    hk = jnp.arange(GLA_K_DIM) // GLA_DK
    hv = jnp.arange(GLA_V_DIM) // HEAD_DIM
    mh = (hk[:, None] == hv[None, :]).astype(BF16)
    mt = (hv[:, None] == hk[None, :]).astype(F32)
    tok = lambda w: pl.BlockSpec((1, tc, w), lambda i, j: (i, j, 0))
    const = lambda a: pl.BlockSpec(a.shape, lambda i, j: (0, 0))
    return pl.pallas_call(
        _gla_kernel,
        grid=(b, s // tc),
        in_specs=[tok(GLA_K_DIM), tok(GLA_K_DIM), tok(GLA_V_DIM), tok(GLA_K_DIM),
                  const(tri), const(mh), const(mt)],
        out_specs=tok(GLA_V_DIM),
        out_shape=jax.ShapeDtypeStruct((b, s, GLA_V_DIM), BF16),
        scratch_shapes=[pltpu.VMEM((GLA_V_DIM, GLA_K_DIM), F32),
                        pltpu.VMEM((CHUNK, GLA_K_DIM), F32),
                        pltpu.VMEM((CHUNK, GLA_K_DIM), F32),
                        pltpu.VMEM((CHUNK, GLA_V_DIM), F32)],
        compiler_params=pltpu.CompilerParams(
            dimension_semantics=("arbitrary", "arbitrary"), vmem_limit_bytes=VMEM_LIMIT),
        name="gla",
    )(q, k, v, la, tri, mh, mt)


def _head_norm(o_ref, segm, gain):
    o = o_ref[...].astype(F32)
    return o * lax.rsqrt(_seg_sum(o * o, segm) * (1.0 / HEAD_DIM) + RMS_EPS) * gain


def _silu(x):
    return x * _sigmoid(x)


def _out_kernel(x_ref, ofox_ref, ogdn_ref, ggate_ref, ogla_ref, lr_ref, wout_ref, segm_ref,
                gf_ref, gg_ref, gl_ref, fgain_ref, wr_ref, br_ref, tri_ref,
                x1_o, h2_o, meta_o, cnt_o, carry_ref):
    tm = x_ref.shape[0]
    segm = segm_ref[...]
    a = _head_norm(ofox_ref, segm, gf_ref[...])
    bb = _head_norm(ogdn_ref, segm, gg_ref[...]) * _silu(ggate_ref[...].astype(F32))
    cc = (_head_norm(ogla_ref, segm[:GLA_V_DIM, :GLA_V_DIM], gl_ref[...])
          * _silu(lr_ref[...].astype(F32)))
    y = (x_ref[...]
         + _dot(a.astype(BF16), wout_ref[0:FOX_DIM, :])
         + _dot(bb.astype(BF16), wout_ref[FOX_DIM:FOX_DIM + GDN_DIM, :])
         + _dot(cc.astype(BF16), wout_ref[FOX_DIM + GDN_DIM:, :]))
    x1_o[...] = y
    ms = jnp.mean(y * y, axis=-1, keepdims=True)
    h2 = y * lax.rsqrt(ms + RMS_EPS) * fgain_ref[...]
    h2_o[...] = h2

    logits = _dot_f32(h2, wr_ref[...]) + br_ref[...]
    lane = lax.broadcasted_iota(jnp.int32, logits.shape, 1)
    big = jnp.int32(4 * LANES)
    ninf = -jnp.inf
    gl = jnp.where(lane < _R_GROUP + N_GROUPS, logits, ninf)
    gmax = jnp.max(gl, axis=1, keepdims=True)
    group_p = 1.0 / jnp.sum(jnp.exp(gl - gmax), axis=1, keepdims=True)
    gidx = jnp.min(jnp.where(gl == gmax, lane, big), axis=1, keepdims=True)
    elane = lane - _R_EXPERT
    in_group = (elane >= 0) & (elane < N_EXPERTS) & ((elane >> 3) == gidx)
    el = jnp.where(in_group, logits, ninf)
    m1 = jnp.max(el, axis=1, keepdims=True)
    i1 = jnp.min(jnp.where(el == m1, lane, big), axis=1, keepdims=True)
    el2 = jnp.where(lane == i1, ninf, el)
    m2 = jnp.max(el2, axis=1, keepdims=True)
    i2 = jnp.min(jnp.where(el2 == m2, lane, big), axis=1, keepdims=True)
    t = jnp.exp(m2 - m1)
    g1 = group_p / (1.0 + t)
    g2 = group_p * t / (1.0 + t)

    @pl.when(pl.program_id(0) == 0)
    def _():
        carry_ref[...] = jnp.zeros_like(carry_ref)

    sel = jnp.where((lane == i1) | (lane == i2), 1.0, 0.0)
    carry = carry_ref[0:1, :]
    rank = _dot(tri_ref[...], sel.astype(BF16)) + carry
    rank1 = jnp.sum(jnp.where(lane == i1, rank, 0.0), axis=1, keepdims=True)
    rank2 = jnp.sum(jnp.where(lane == i2, rank, 0.0), axis=1, keepdims=True)
    new_carry = carry + jnp.sum(sel, axis=0, keepdims=True)
    carry_ref[...] = jnp.broadcast_to(new_carry, carry_ref.shape)
    cnt_o[...] = jnp.broadcast_to(new_carry, cnt_o.shape)
    cols = [(i1 - _R_EXPERT).astype(F32), (i2 - _R_EXPERT).astype(F32), rank1, rank2, g1, g2]
    meta = jnp.zeros(logits.shape, F32)
    for idx, col in enumerate(cols):
        meta = jnp.where(lane == idx, col, meta)
    meta_o[...] = meta


def _out_call(x, ofox, ogdn, ggate, ogla, lr, wout, segm, gf, gg, gl, fgain, wr, br, tm):
    n, d = x.shape
    tm = min(tm, n)
    tri = (jnp.arange(tm)[:, None] > jnp.arange(tm)[None, :]).astype(BF16)
    tok = lambda w: pl.BlockSpec((tm, w), lambda i: (i, 0))
    const = lambda a: pl.BlockSpec(a.shape, lambda i: (0,) * a.ndim)
    return pl.pallas_call(
        _out_kernel,
        grid=(n // tm,),
        in_specs=[tok(d), tok(FOX_DIM), tok(GDN_DIM), tok(GDN_DIM), tok(GLA_V_DIM), tok(GLA_V_DIM),
                  const(wout), const(segm), const(gf), const(gg), const(gl), const(fgain),
                  const(wr), const(br), const(tri)],
        out_specs=[tok(d), tok(d), tok(LANES), pl.BlockSpec((8, LANES), lambda i: (0, 0))],
        out_shape=[jax.ShapeDtypeStruct((n, d), F32), jax.ShapeDtypeStruct((n, d), F32),
                   jax.ShapeDtypeStruct((n, LANES), F32), jax.ShapeDtypeStruct((8, LANES), F32)],
        scratch_shapes=[pltpu.VMEM((8, LANES), F32)],
        compiler_params=pltpu.CompilerParams(
            dimension_semantics=("arbitrary",), vmem_limit_bytes=VMEM_LIMIT),
        name="out_router",
    )(x, ofox, ogdn, ggate, ogla, lr, wout, segm, gf, gg, gl, fgain, wr, br, tri)


def _dispatch_kernel(dest_ref, h_ref, xb_in_ref, xb_ref, sem):
    del xb_in_ref
    td = h_ref.shape[0]

    def row_copy(t, d):
        return pltpu.make_async_copy(h_ref.at[pl.ds(t, 1), :], xb_ref.at[pl.ds(d, 1), :], sem)

    def issue(t, carry):
        for kk in range(TOP_K):
            row_copy(t, dest_ref[0, 0, TOP_K * t + kk]).start()
        return carry

    lax.fori_loop(0, td, issue, 0)

    def drain(t, carry):
        for kk in range(TOP_K):
            row_copy(0, 0).wait()
        return carry

    lax.fori_loop(0, td, drain, 0)


def _dispatch_call(dest, h2, n_rows, td):
    n, d = h2.shape
    td = min(td, n)
    dest3 = dest.reshape(n // td, 1, TOP_K * td)
    xb0 = jnp.zeros((n_rows, d), h2.dtype)
    return pl.pallas_call(
        _dispatch_kernel,
        grid=(n // td,),
        in_specs=[pl.BlockSpec((1, 1, TOP_K * td), lambda i: (i, 0, 0), memory_space=pltpu.SMEM),
                  pl.BlockSpec((td, d), lambda i: (i, 0)),
                  pl.BlockSpec(memory_space=pl.ANY)],
        out_specs=pl.BlockSpec(memory_space=pl.ANY),
        out_shape=jax.ShapeDtypeStruct((n_rows, d), h2.dtype),
        scratch_shapes=[pltpu.SemaphoreType.DMA(())],
        input_output_aliases={2: 0},
        compiler_params=pltpu.CompilerParams(
            dimension_semantics=("arbitrary",), has_side_effects=True),
        name="dispatch",
    )(dest3, h2, xb0)


def _expert_kernel(be_ref, nu_ref, x_ref, wg_ref, wu_ref, wd_ref, y_ref):
    del be_ref

    @pl.when(pl.program_id(0) < nu_ref[0])
    def _():
        x = x_ref[...].astype(BF16)
        a = _dot(x, wg_ref[0])
        u = _dot(x, wu_ref[0])
        hmid = (_silu(a) * u).astype(BF16)
        y_ref[...] = _dot(hmid, wd_ref[0])

    @pl.when(pl.program_id(0) >= nu_ref[0])
    def _():
        y_ref[...] = jnp.zeros_like(y_ref)


def _expert_call(block_e, n_used, xb, wg, wu, wd, tmb):
    n_rows, d = xb.shape
    de = wg.shape[-1]
    n_blocks = n_rows // tmb

    def xmap(i, be, nu):
        return (jnp.minimum(i, jnp.maximum(nu[0] - 1, 0)), 0)

    wmap = lambda i, be, nu: (be[i], 0, 0)
    return pl.pallas_call(
        _expert_kernel,
        grid_spec=pltpu.PrefetchScalarGridSpec(
            num_scalar_prefetch=2,
            grid=(n_blocks,),
            in_specs=[pl.BlockSpec((tmb, d), xmap),
                      pl.BlockSpec((1, d, de), wmap),
                      pl.BlockSpec((1, d, de), wmap),
                      pl.BlockSpec((1, de, d), wmap)],
            out_specs=pl.BlockSpec((tmb, d), lambda i, be, nu: (i, 0)),
        ),
        out_shape=jax.ShapeDtypeStruct((n_rows, d), F32),
        compiler_params=pltpu.CompilerParams(
            dimension_semantics=("arbitrary",), vmem_limit_bytes=VMEM_LIMIT),
        name="experts",
    )(block_e, n_used, xb, wg, wu, wd)


def _combine_kernel(dest_ref, x1_ref, meta_ref, yb_ref, o_ref, buf_ref, sem):
    td = x1_ref.shape[0]

    def row_copy(t, kk, d):
        return pltpu.make_async_copy(yb_ref.at[pl.ds(d, 1), :], buf_ref.at[kk, pl.ds(t, 1), :], sem)

    def issue(t, carry):
        for kk in range(TOP_K):
            row_copy(t, kk, dest_ref[0, 0, TOP_K * t + kk]).start()
        return carry

    lax.fori_loop(0, td, issue, 0)

    def drain(t, carry):
        for kk in range(TOP_K):
            row_copy(0, kk, 0).wait()
        return carry

    lax.fori_loop(0, td, drain, 0)
    meta = meta_ref[...]
    o_ref[...] = x1_ref[...] + meta[:, 4:5] * buf_ref[0] + meta[:, 5:6] * buf_ref[1]


def _combine_call(dest, x1, meta, yb, td):
    n, d = x1.shape
    td = min(td, n)
    dest3 = dest.reshape(n // td, 1, TOP_K * td)
    return pl.pallas_call(
        _combine_kernel,
        grid=(n // td,),
        in_specs=[pl.BlockSpec((1, 1, TOP_K * td), lambda i: (i, 0, 0), memory_space=pltpu.SMEM),
                  pl.BlockSpec((td, d), lambda i: (i, 0)),
                  pl.BlockSpec((td, LANES), lambda i: (i, 0)),
                  pl.BlockSpec(memory_space=pl.ANY)],
        out_specs=pl.BlockSpec((td, d), lambda i: (i, 0)),
        out_shape=jax.ShapeDtypeStruct((n, d), F32),
        scratch_shapes=[pltpu.VMEM((TOP_K, td, d), F32), pltpu.SemaphoreType.DMA(())],
        compiler_params=pltpu.CompilerParams(
            dimension_semantics=("arbitrary",), vmem_limit_bytes=VMEM_LIMIT),
        name="combine",
    )(dest3, x1, meta, yb)


TM_PROJ = 512
TQ_FOX = 256
TC_GDN = 256
TC_GLA = 256
TM_OUT = 512
TD_MOE = 256
TMB_EXPERT = 256


def _pad8(v):
    return jnp.zeros((8,), F32).at[:v.shape[0]].set(v.astype(F32)).reshape(8, 1)


def _token_mixer(x, attn_norm, w_in, fox_q_norm, fox_k_norm, fox_f_bias,
                 gdn_conv, gdn_a_log, gdn_dt_bias, gla_w_a2, gla_b_a):
    b, s, d = x.shape
    wbig = jnp.concatenate([w_in[:, _OFFS[g]:_OFFS[g + 1]] for g in _WIDE], axis=1).astype(BF16)
    wsm = jnp.zeros((d, LANES), F32)
    for pos, g in ((_SM_F, 3), (_SM_A, 6), (_SM_B, 7), (_SM_A1, 12)):
        wsm = wsm.at[:, pos:pos + _SIZES[g]].set(w_in[:, _OFFS[g]:_OFFS[g + 1]])
    wsm = wsm.astype(BF16)
    wa2 = jnp.zeros((LANES, GLA_K_DIM), F32).at[_SM_A1:_SM_A1 + GLA_RANK].set(gla_w_a2)
    segm = _seg_matrix(FOX_DIM)
    qg = (jnp.tile(fox_q_norm, H_FOX) * (HEAD_DIM ** -0.5)).reshape(1, FOX_DIM)
    kg = jnp.tile(fox_k_norm, H_FOX).reshape(1, FOX_DIM)
    outs = _proj_call(x, attn_norm.reshape(1, d), wbig, wsm, segm, qg, kg,
                      _pad8(fox_f_bias), _pad8(gdn_a_log), _pad8(gdn_dt_bias),
                      wa2, gla_b_a.reshape(1, GLA_K_DIM), TM_PROJ)
    fq, fk, fv, gqkv, ggate, lq, lk, lv, lr, la, c, g, beta = outs
    npair = H_FOX // 2
    c4 = jnp.zeros((b, npair, 8, s), F32).at[:, :, :2, :].set(
        c[:, :H_FOX].reshape(b, npair, 2, s))
    o_fox = _fox_call(fq, fk, fv, c4, TQ_FOX)
    o_gdn = _gdn_call(gqkv, gdn_conv.astype(F32), g, beta, segm, TC_GDN)
    o_gla = _gla_call(lq, lk, lv, la, TC_GLA)
    return o_fox, o_gdn, ggate, o_gla, lr


def _layer(x, p):
    b, s, d = x.shape
    n = b * s
    o_fox, o_gdn, ggate, o_gla, lr = _token_mixer(
        x, p['attn_norm'], p['w_in'], p['fox_q_norm'], p['fox_k_norm'], p['fox_f_bias'],
        p['gdn_conv'], p['gdn_a_log'], p['gdn_dt_bias'], p['gla_w_a2'], p['gla_b_a'])
    wr = jnp.zeros((d, LANES), F32)
    wr = wr.at[:, _R_GROUP:_R_GROUP + N_GROUPS].set(p['w_router_group'])
    wr = wr.at[:, _R_EXPERT:_R_EXPERT + N_EXPERTS].set(p['w_router_expert'])
    br = jnp.zeros((1, LANES), F32)
    br = br.at[0, _R_GROUP:_R_GROUP + N_GROUPS].set(p['b_router_group'])
    br = br.at[0, _R_EXPERT:_R_EXPERT + N_EXPERTS].set(p['b_router_expert'])
    flat = lambda a: a.reshape(n, a.shape[-1])
    x1, h2, meta, cnt = _out_call(
        flat(x), flat(o_fox), flat(o_gdn), flat(ggate), flat(o_gla), flat(lr),
        p['w_out'].astype(BF16), _seg_matrix(FOX_DIM),
        jnp.tile(p['fox_o_norm'], H_FOX).reshape(1, FOX_DIM),
        jnp.tile(p['gdn_o_norm'], H_GDN).reshape(1, GDN_DIM),
        jnp.tile(p['gla_o_norm'], H_GLA).reshape(1, GLA_V_DIM),
        p['ffn_norm'].reshape(1, d), wr, br, TM_OUT)

    tmb = TMB_EXPERT
    counts = cnt[0, _R_EXPERT:_R_EXPERT + N_EXPERTS].astype(jnp.int32)
    padded = (counts + tmb - 1) // tmb * tmb
    pends = jnp.cumsum(padded)
    pstarts = pends - padded
    eid = meta[:, 0:TOP_K].astype(jnp.int32)
    rank = meta[:, TOP_K:2 * TOP_K].astype(jnp.int32)
    dest = (pstarts[eid] + rank).reshape(-1)
    n_blocks = -(-(n * TOP_K) // tmb) + N_EXPERTS
    block_e = jnp.minimum(
        jnp.searchsorted(pends, jnp.arange(n_blocks, dtype=jnp.int32) * tmb, side='right'),
        N_EXPERTS - 1).astype(jnp.int32)
    n_used = (pends[-1:] // tmb).astype(jnp.int32)

    xb = _dispatch_call(dest, h2, n_blocks * tmb, TD_MOE)
    yb = _expert_call(block_e, n_used, xb, p['w_expert_gate'].astype(BF16),
                      p['w_expert_up'].astype(BF16), p['w_expert_down'].astype(BF16), tmb)
    x2 = _combine_call(dest, x1, meta, yb, TD_MOE)
    return x2.reshape(b, s, d)


_PARAM_NAMES = ['attn_norm', 'w_in', 'fox_q_norm', 'fox_k_norm', 'fox_f_bias', 'fox_o_norm',
                'gdn_conv', 'gdn_a_log', 'gdn_dt_bias', 'gdn_o_norm',
                'gla_w_a2', 'gla_b_a', 'gla_o_norm', 'w_out',
                'ffn_norm', 'w_router_group', 'b_router_group', 'w_router_expert',
                'b_router_expert', 'w_expert_gate', 'w_expert_up', 'w_expert_down']


def kernel(x, attn_norm, w_in, fox_q_norm, fox_k_norm, fox_f_bias, fox_o_norm, gdn_conv, gdn_a_log, gdn_dt_bias, gdn_o_norm, gla_w_a2, gla_b_a, gla_o_norm, w_out, ffn_norm, w_router_group, b_router_group, w_router_expert, b_router_expert, w_expert_gate, w_expert_up, w_expert_down):
    params = dict(zip(_PARAM_NAMES, (
        attn_norm, w_in, fox_q_norm, fox_k_norm, fox_f_bias, fox_o_norm, gdn_conv, gdn_a_log,
        gdn_dt_bias, gdn_o_norm, gla_w_a2, gla_b_a, gla_o_norm, w_out, ffn_norm,
        w_router_group, b_router_group, w_router_expert, b_router_expert,
        w_expert_gate, w_expert_up, w_expert_down)))
    for layer in range(attn_norm.shape[0]):
        x = _layer(x, {name: val[layer] for name, val in params.items()})
    return x
```

```python
import functools

import jax
import jax.numpy as jnp
from jax import lax
from jax.experimental import pallas as pl
from jax.experimental.pallas import tpu as pltpu

F32 = jnp.float32
BF16 = jnp.bfloat16

HEAD_DIM = 64
H_FOX = 6
H_GDN = 6
H_GLA = 4
FOX_DIM = H_FOX * HEAD_DIM
GDN_DIM = H_GDN * HEAD_DIM
GLA_DK = 32
GLA_K_DIM = H_GLA * GLA_DK
GLA_V_DIM = H_GLA * HEAD_DIM
GLA_RANK = 16
GLA_TAU = 16.0
CONV_K = 4
CHUNK = 64
N_GROUPS = 4
EXPERTS_PER_GROUP = 8
N_EXPERTS = N_GROUPS * EXPERTS_PER_GROUP
TOP_K = 2
RMS_EPS = 1e-6

LANES = 128
SUBLANES = 8
VMEM_LIMIT = 56 * 1024 * 1024

_SIZES = [FOX_DIM, FOX_DIM, FOX_DIM, H_FOX, 3 * GDN_DIM, GDN_DIM, H_GDN, H_GDN,
          GLA_K_DIM, GLA_K_DIM, GLA_V_DIM, GLA_V_DIM, GLA_RANK]
_OFFS = [sum(_SIZES[:i]) for i in range(len(_SIZES) + 1)]
_WIDE = [0, 1, 2, 4, 5, 8, 9, 10, 11]
_WIDE_OFF = [0]
for _g in _WIDE:
    _WIDE_OFF.append(_WIDE_OFF[-1] + _SIZES[_g])
_SM_F, _SM_A, _SM_B, _SM_A1 = 0, 8, 16, 32
_R_GROUP, _R_EXPERT = 0, 32


def _dot(a, b):
    return jnp.dot(a, b, preferred_element_type=F32)


def _dot_nt(a, b):
    return lax.dot_general(a, b, (((1,), (1,)), ((), ())), preferred_element_type=F32)


def _dot_tn(a, b):
    return lax.dot_general(a, b, (((0,), (0,)), ((), ())), preferred_element_type=F32)


def _dot_f32(a, b):
    return jnp.dot(a, b, preferred_element_type=F32, precision=lax.Precision.HIGHEST)


def _seg_sum(sq, segm):
    hi = sq.astype(BF16)
    lo = (sq - hi.astype(F32)).astype(BF16)
    return _dot(hi, segm) + _dot(lo, segm)


def _sigmoid(x):
    return 1.0 / (1.0 + jnp.exp(-x))


def _softplus(x):
    return jnp.maximum(x, 0.0) + jnp.log1p(jnp.exp(-jnp.abs(x)))


def _log_sigmoid(x):
    return -_softplus(-x)


def _lane_cumsum(x, seg):
    lane = lax.broadcasted_iota(jnp.int32, x.shape, 1)
    pos = lane & (seg - 1)
    s = 1
    while s < seg:
        x = x + jnp.where(pos >= s, pltpu.roll(x, s, 1), 0.0)
        s *= 2
    return x


def _seg_matrix(n):
    i = jnp.arange(n) // HEAD_DIM
    return (i[:, None] == i[None, :]).astype(BF16)


def _proj_kernel(x_ref, gain_ref, wbig_ref, wsm_ref, segm_ref, qg_ref, kg_ref,
                 fb_ref, alog_ref, dtb_ref, wa2_ref, ba_ref,
                 fq_o, fk_o, fv_o, gqkv_o, ggate_o, lq_o, lk_o, lv_o, lr_o, la_o,
                 c_o, g_o, beta_o, carry_ref):
    tm = x_ref.shape[1]
    x = x_ref[0]
    ms = jnp.mean(x * x, axis=-1, keepdims=True)
    hb = (x * lax.rsqrt(ms + RMS_EPS) * gain_ref[...]).astype(BF16)

    def wide(i):
        return _dot(hb, wbig_ref[:, _WIDE_OFF[i]:_WIDE_OFF[i + 1]])

    segm = segm_ref[...]
    q = wide(0)
    q = q * lax.rsqrt(_seg_sum(q * q, segm) * (1.0 / HEAD_DIM) + RMS_EPS) * qg_ref[...]
    fq_o[0] = q.astype(BF16).T
    k = wide(1)
    k = k * lax.rsqrt(_seg_sum(k * k, segm) * (1.0 / HEAD_DIM) + RMS_EPS) * kg_ref[...]
    fk_o[0] = k.astype(BF16)
    fv_o[0] = wide(2).astype(BF16).T
    gqkv_o[0] = wide(3).astype(BF16)
    ggate_o[0] = wide(4).astype(BF16)
    lq_o[0] = wide(5).astype(BF16)
    lk_o[0] = wide(6).astype(BF16)
    lv_o[0] = wide(7).astype(BF16)
    lr_o[0] = wide(8).astype(BF16)

    sm = _dot(hb, wsm_ref[...])
    la_logit = _dot_f32(sm, wa2_ref[...]) + ba_ref[...]
    la_o[0] = _log_sigmoid(la_logit) * (1.0 / GLA_TAU)

    smt = sm.T
    log_f = _log_sigmoid(smt[_SM_F:_SM_F + 8] + fb_ref[...])

    @pl.when(pl.program_id(1) == 0)
    def _():
        carry_ref[...] = jnp.zeros_like(carry_ref)

    cum = _lane_cumsum(log_f, tm) + carry_ref[:, 0:1]
    c_o[0] = cum
    carry_ref[...] = jnp.broadcast_to(cum[:, tm - 1:tm], carry_ref.shape)
    g_o[0] = -jnp.exp(alog_ref[...]) * _softplus(smt[_SM_A:_SM_A + 8] + dtb_ref[...])
    beta_o[0] = _sigmoid(smt[_SM_B:_SM_B + 8])


def _proj_call(x, gain, wbig, wsm, segm, qg, kg, fb, alog, dtb, wa2, ba, tm):
    b, s, d = x.shape
    tm = min(tm, s)
    const = lambda shape: pl.BlockSpec(shape, lambda i, j: (0,) * len(shape))
    tok = lambda w: pl.BlockSpec((1, tm, w), lambda i, j: (i, j, 0))
    row = pl.BlockSpec((1, 8, tm), lambda i, j: (i, 0, j))
    widths = [_SIZES[g] for g in _WIDE]
    out_shape = ([jax.ShapeDtypeStruct((b, s, w), BF16) for w in widths]
                 + [jax.ShapeDtypeStruct((b, s, LANES), F32)]
                 + [jax.ShapeDtypeStruct((b, 8, s), F32)] * 3)
    out_specs = [tok(w) for w in widths] + [tok(LANES)] + [row] * 3
    for i in (0, 2):
        out_shape[i] = jax.ShapeDtypeStruct((b, widths[i], s), BF16)
        out_specs[i] = pl.BlockSpec((1, widths[i], tm), lambda i, j: (i, 0, j))
    return pl.pallas_call(
        _proj_kernel,
        grid=(b, s // tm),
        in_specs=[tok(d), const((1, d)), const(wbig.shape), const(wsm.shape), const(segm.shape),
                  const(qg.shape), const(kg.shape), const((8, 1)), const((8, 1)), const((8, 1)),
                  const(wa2.shape), const(ba.shape)],
        out_specs=out_specs,
        out_shape=out_shape,
        scratch_shapes=[pltpu.VMEM((8, LANES), F32)],
        compiler_params=pltpu.CompilerParams(
            dimension_semantics=("arbitrary", "arbitrary"), vmem_limit_bytes=VMEM_LIMIT),
        name="proj",
    )(x, gain, wbig, wsm, segm, qg, kg, fb, alog, dtb, wa2, ba)


_C_PIECES = 3
_ACC_ROWS = HEAD_DIM + 16


def _fox_kernel(qt_ref, k_ref, vt_ref, aug_ref, ones_ref, o_ref, m_ref, acc_ref):
    tq = qt_ref.shape[2]
    qi = pl.program_id(1)
    heads = range(H_FOX)
    row = lax.broadcasted_iota(jnp.int32, (LANES, tq), 0)
    first = row < HEAD_DIM
    zero = jnp.zeros((LANES, tq), BF16)
    qts = []
    for h in heads:
        qt = qt_ref[0, (h // 2) * LANES:(h // 2 + 1) * LANES, :]
        keep = first if h % 2 == 0 else jnp.logical_not(first)
        qts.append(jnp.concatenate([jnp.where(keep, qt, zero), aug_ref[h % 2]], axis=0))
    m_ref[...] = jnp.full_like(m_ref, -jnp.inf)
    acc_ref[...] = jnp.zeros_like(acc_ref)
    krow = lax.broadcasted_iota(jnp.int32, (tq, tq), 0)
    qcol = lax.broadcasted_iota(jnp.int32, (tq, tq), 1)
    ones = ones_ref[...]

    def step(j, masked):
        k0 = pl.multiple_of(j * tq, tq)
        ss = [_dot(k_ref[0, h // 2, pl.ds(k0, tq), :], qts[h]) for h in heads]
        if masked:
            ss = [jnp.where(krow <= qcol, s, -jnp.inf) for s in ss]
        m_prev = [m_ref[h] for h in heads]
        m_new = [jnp.maximum(m_prev[h], jnp.max(ss[h], axis=0, keepdims=True)) for h in heads]
        ps = [jnp.exp(ss[h] - m_new[h]).astype(BF16) for h in heads]
        for h in heads:
            vta = jnp.concatenate(
                [vt_ref[0, h * HEAD_DIM:(h + 1) * HEAD_DIM, pl.ds(k0, tq)], ones], axis=0)
            acc_ref[h] = jnp.exp(m_prev[h] - m_new[h]) * acc_ref[h] + _dot(vta, ps[h])
            m_ref[h] = m_new[h]

    step(qi, True)

    def body(j, carry):
        step(j, False)
        return carry

    lax.fori_loop(0, qi, body, 0)
    outs = []
    for h in heads:
        acc = acc_ref[h]
        outs.append(acc[0:HEAD_DIM] / acc[HEAD_DIM:HEAD_DIM + 1])
    o_ref[0] = jnp.concatenate(outs, axis=0).T.astype(o_ref.dtype)


def _fox_call(qt, k, vt, c, tq):
    b, _, s = qt.shape
    tq = min(tq, s)
    npair = H_FOX // 2
    ct = c[:, :H_FOX, :].transpose(0, 2, 1).reshape(b, s, npair, 2).transpose(0, 2, 1, 3)
    pieces = []
    rem = ct
    for _ in range(_C_PIECES):
        bits = lax.bitcast_convert_type(rem, jnp.uint32) & jnp.uint32(0xFFFF0000)
        piece = lax.bitcast_convert_type(bits, F32)
        pieces.append(piece.astype(BF16))
        rem = rem - piece
    cp = jnp.stack(pieces, axis=-1).reshape(b, npair, s, 2 * _C_PIECES)
    kp = k.reshape(b, s, npair, LANES).transpose(0, 2, 1, 3)
    ka = jnp.concatenate(
        [kp, cp, jnp.zeros((b, npair, s, LANES - 2 * _C_PIECES), BF16)], axis=-1)
    r = jnp.arange(LANES)[None, :, None]
    hh = jnp.arange(2)[:, None, None]
    aug = jnp.where((r >= hh * _C_PIECES) & (r < (hh + 1) * _C_PIECES), -1.0, 0.0)
    aug = jnp.broadcast_to(aug, (2, LANES, tq)).astype(BF16)
    ones = jnp.broadcast_to(jnp.where(jnp.arange(16)[:, None] == 0, 1.0, 0.0), (16, tq)).astype(BF16)
    return pl.pallas_call(
        _fox_kernel,
        grid=(b, s // tq),
        in_specs=[pl.BlockSpec((1, FOX_DIM, tq), lambda i, j: (i, 0, j)),
                  pl.BlockSpec((1, npair, s, 2 * LANES), lambda i, j: (i, 0, 0, 0)),
                  pl.BlockSpec((1, FOX_DIM, s), lambda i, j: (i, 0, 0)),
                  pl.BlockSpec(aug.shape, lambda i, j: (0, 0, 0)),
                  pl.BlockSpec(ones.shape, lambda i, j: (0, 0))],
        out_specs=pl.BlockSpec((1, tq, FOX_DIM), lambda i, j: (i, j, 0)),
        out_shape=jax.ShapeDtypeStruct((b, s, FOX_DIM), BF16),
        scratch_shapes=[pltpu.VMEM((H_FOX, 1, tq), F32),
                        pltpu.VMEM((H_FOX, _ACC_ROWS, tq), F32)],
        compiler_params=pltpu.CompilerParams(
            dimension_semantics=("arbitrary", "arbitrary"),
            vmem_limit_bytes=VMEM_LIMIT),
        name="fox",
    )(qt, ka, vt, aug, ones)


def _bmm(a, b):
    return jnp.einsum('gmk,gkn->gmn', a.astype(BF16), b.astype(BF16),
                      preferred_element_type=F32)


def _bmm_nt(a, b):
    return jnp.einsum('gmk,gnk->gmn', a.astype(BF16), b.astype(BF16),
                      preferred_element_type=F32)


def _unit_lower_inverse(a_strict, eye):
    n = -a_strict
    t = eye + n
    size = 1
    while 2 * size < CHUNK:
        n = _bmm(n, n)
        t = t + _bmm(t, n)
        size *= 2
    return t


def _gdn_kernel(qkv_ref, convw_ref, g_ref, beta_ref, segm_ref, o_ref, state_ref, xext_ref):
    tc = qkv_ref.shape[1]
    pad = SUBLANES

    @pl.when(pl.program_id(1) == 0)
    def _():
        state_ref[...] = jnp.zeros_like(state_ref)
        xext_ref[0:pad, :] = jnp.zeros((pad, xext_ref.shape[1]), F32)

    x = qkv_ref[0].astype(F32)
    xext_ref[pad:pad + tc, :] = x
    y = convw_ref[0:1, :] * xext_ref[pad - 3:pad - 3 + tc, :]
    for i in range(1, CONV_K):
        y = y + convw_ref[i:i + 1, :] * xext_ref[pad - 3 + i:pad - 3 + i + tc, :]
    xext_ref[0:pad, :] = x[tc - pad:tc, :]
    y = y * _sigmoid(y)

    segm = segm_ref[...]
    q = y[:, 0:GDN_DIM]
    k = y[:, GDN_DIM:2 * GDN_DIM]
    v = y[:, 2 * GDN_DIM:3 * GDN_DIM]
    q = q * lax.rsqrt(_seg_sum(q * q, segm) + RMS_EPS) * (HEAD_DIM ** -0.5)
    k = k * lax.rsqrt(_seg_sum(k * k, segm) + RMS_EPS)

    kt = k.T

    dec_row = _lane_cumsum(g_ref[0], CHUNK)
    dec_col = dec_row.T
    beta_col = beta_ref[0].T

    nc = tc // CHUNK
    index = [(c, h) for c in range(nc) for h in range(H_GDN)]

    def split(a):
        return jnp.stack([a[c * CHUNK:(c + 1) * CHUNK, h * HEAD_DIM:(h + 1) * HEAD_DIM]
                          for c, h in index])

    def split_col(a):
        return jnp.stack([a[c * CHUNK:(c + 1) * CHUNK, h:h + 1] for c, h in index])

    q3, k3, v3 = split(q), split(k), split(v)
    kt3 = jnp.stack([kt[h * HEAD_DIM:(h + 1) * HEAD_DIM, c * CHUNK:(c + 1) * CHUNK]
                     for c, h in index])
    dcol = split_col(dec_col)
    bcol = split_col(beta_col)
    drow = jnp.stack([dec_row[h:h + 1, c * CHUNK:(c + 1) * CHUNK] for c, h in index])
    dlast = dcol[:, CHUNK - 1:CHUNK, :]

    ri = lax.broadcasted_iota(jnp.int32, (1, CHUNK, CHUNK), 1)
    ci = lax.broadcasted_iota(jnp.int32, (1, CHUNK, CHUNK), 2)
    causal = ci <= ri
    eye = (ci == ri).astype(F32)
    gamma = jnp.where(causal, jnp.exp(jnp.where(causal, dcol - drow, 0.0)), 0.0)
    edec = jnp.exp(dcol)
    kb = k3 * bcol
    a = jnp.where(ci < ri, _bmm_nt(kb, k3) * gamma, 0.0)
    t = _unit_lower_inverse(a, eye)
    u3 = _bmm(t, v3 * bcol)
    w3 = _bmm(t, kb * edec)
    intra = _bmm_nt(q3, k3) * gamma
    qd = q3 * edec
    kdt = kt3 * jnp.exp(dlast - drow)
    elast = jnp.exp(dlast)

    s = state_ref[...]
    for c in range(nc):
        sl = slice(c * H_GDN, (c + 1) * H_GDN)
        v_new = u3[sl] - _bmm(w3[sl], s)
        o = _bmm(qd[sl], s) + _bmm(intra[sl], v_new)
        s = s * elast[sl] + _bmm(kdt[sl], v_new)
        o_ref[0, c * CHUNK:(c + 1) * CHUNK, :] = jnp.concatenate(
            [o[h] for h in range(H_GDN)], axis=1).astype(o_ref.dtype)
    state_ref[...] = s


def _gdn_call(qkv, convw, g, beta, segm, tc):
    b, s, w = qkv.shape
    tc = min(tc, s)
    return pl.pallas_call(
        _gdn_kernel,
        grid=(b, s // tc),
        in_specs=[pl.BlockSpec((1, tc, w), lambda i, j: (i, j, 0)),
                  pl.BlockSpec(convw.shape, lambda i, j: (0, 0)),
                  pl.BlockSpec((1, 8, tc), lambda i, j: (i, 0, j)),
                  pl.BlockSpec((1, 8, tc), lambda i, j: (i, 0, j)),
                  pl.BlockSpec(segm.shape, lambda i, j: (0, 0))],
        out_specs=pl.BlockSpec((1, tc, GDN_DIM), lambda i, j: (i, j, 0)),
        out_shape=jax.ShapeDtypeStruct((b, s, GDN_DIM), BF16),
        scratch_shapes=[pltpu.VMEM((H_GDN, HEAD_DIM, HEAD_DIM), F32),
                        pltpu.VMEM((tc + SUBLANES, w), F32)],
        compiler_params=pltpu.CompilerParams(
            dimension_semantics=("arbitrary", "arbitrary"), vmem_limit_bytes=VMEM_LIMIT),
        name="gdn",
    )(qkv, convw, g, beta, segm)


def _gla_kernel(q_ref, k_ref, v_ref, la_ref, tri_ref, mh_ref, mt_ref, o_ref,
                st_ref, kf_ref, bq_ref, vf_ref):
    tc = q_ref.shape[1]

    @pl.when(pl.program_id(1) == 0)
    def _():
        st_ref[...] = jnp.zeros_like(st_ref)

    row = lax.broadcasted_iota(jnp.int32, (CHUNK, GLA_K_DIM), 0)
    mh = mh_ref[...]
    for c in range(tc // CHUNK):
        r0, r1 = c * CHUNK, (c + 1) * CHUNK
        bq = _dot_f32(tri_ref[...], la_ref[0, r0:r1, :])
        q = q_ref[0, r0:r1, :].astype(F32) * (GLA_DK ** -0.5)
        k = k_ref[0, r0:r1, :].astype(F32)
        v = v_ref[0, r0:r1, :].astype(F32)
        kf_ref[...] = k
        bq_ref[...] = bq
        vf_ref[...] = v
        st = st_ref[...]
        o_inter = _dot_nt((q * jnp.exp(bq)).astype(BF16), st.astype(BF16))

        group_out = []
        for g0 in range(0, CHUNK, SUBLANES):
            n = CHUNK - g0
            qg, bg, rg = q[g0:, :], bq[g0:, :], row[g0:, :]
            es = []
            for j in range(g0, g0 + SUBLANES):
                kj = kf_ref[j:j + 1, :]
                bj = bq_ref[j:j + 1, :]
                es.append(jnp.where(rg >= j, jnp.exp(bg - bj), 0.0) * (qg * kj))
            p = _dot(jnp.concatenate(es, axis=0).astype(BF16), mh)
            acc = p[0:n] * vf_ref[g0:g0 + 1, :]
            for jj in range(1, SUBLANES):
                acc = acc + p[jj * n:(jj + 1) * n] * vf_ref[g0 + jj:g0 + jj + 1, :]
            group_out.append(acc)
        pieces = []
        for r in range(0, CHUNK, SUBLANES):
            piece = o_inter[r:r + SUBLANES]
            for gi, g0 in enumerate(range(0, r + SUBLANES, SUBLANES)):
                piece = piece + group_out[gi][r - g0:r - g0 + SUBLANES]
            pieces.append(piece)
        o_ref[0, r0:r1, :] = jnp.concatenate(pieces, axis=0).astype(o_ref.dtype)
        blast = bq[CHUNK - 1:CHUNK, :]
        kd = k * jnp.exp(blast - bq)
        upd = _dot_tn(v.astype(BF16), kd.astype(BF16))
        st_ref[...] = (st * jnp.exp(blast) + upd) * mt_ref[...]


def _gla_call(q, k, v, la, tc):
    b, s, _ = q.shape
    tc = min(tc, s)
    tri = (jnp.arange(CHUNK)[:, None] >= jnp.arange(CHUNK)[None, :]).astype(F32)
    hk = jnp.arange(GLA_K_DIM) // GLA_DK
    hv = jnp.arange(GLA_V_DIM) // HEAD_DIM
    mh = (hk[:, None] == hv[None, :]).astype(BF16)
    mt = (hv[:, None] == hk[None, :]).astype(F32)
    tok = lambda w: pl.BlockSpec((1, tc, w), lambda i, j: (i, j, 0))
    const = lambda a: pl.BlockSpec(a.shape, lambda i, j: (0, 0))
    return pl.pallas_call(
        _gla_kernel,
        grid=(b, s // tc),
        in_specs=[tok(GLA_K_DIM), tok(GLA_K_DIM), tok(GLA_V_DIM), tok(GLA_K_DIM),
                  const(tri), const(mh), const(mt)],
        out_specs=tok(GLA_V_DIM),
        out_shape=jax.ShapeDtypeStruct((b, s, GLA_V_DIM), BF16),
        scratch_shapes=[pltpu.VMEM((GLA_V_DIM, GLA_K_DIM), F32),
                        pltpu.VMEM((CHUNK, GLA_K_DIM), F32),
                        pltpu.VMEM((CHUNK, GLA_K_DIM), F32),
                        pltpu.VMEM((CHUNK, GLA_V_DIM), F32)],
        compiler_params=pltpu.CompilerParams(
            dimension_semantics=("arbitrary", "arbitrary"), vmem_limit_bytes=VMEM_LIMIT),
        name="gla",
    )(q, k, v, la, tri, mh, mt)


def _head_norm(o_ref, segm, gain):
    o = o_ref[...].astype(F32)
    return o * lax.rsqrt(_seg_sum(o * o, segm) * (1.0 / HEAD_DIM) + RMS_EPS) * gain


def _silu(x):
    return x * _sigmoid(x)


def _out_kernel(x_ref, ofox_ref, ogdn_ref, ggate_ref, ogla_ref, lr_ref, wout_ref, segm_ref,
                gf_ref, gg_ref, gl_ref, fgain_ref, wr_ref, br_ref, tri_ref,
                x1_o, h2_o, meta_o, cnt_o, carry_ref):
    tm = x_ref.shape[0]
    segm = segm_ref[...]
    a = _head_norm(ofox_ref, segm, gf_ref[...])
    bb = _head_norm(ogdn_ref, segm, gg_ref[...]) * _silu(ggate_ref[...].astype(F32))
    cc = (_head_norm(ogla_ref, segm[:GLA_V_DIM, :GLA_V_DIM], gl_ref[...])
          * _silu(lr_ref[...].astype(F32)))
    y = (x_ref[...]
         + _dot(a.astype(BF16), wout_ref[0:FOX_DIM, :])
         + _dot(bb.astype(BF16), wout_ref[FOX_DIM:FOX_DIM + GDN_DIM, :])
         + _dot(cc.astype(BF16), wout_ref[FOX_DIM + GDN_DIM:, :]))
    x1_o[...] = y
    ms = jnp.mean(y * y, axis=-1, keepdims=True)
    h2 = y * lax.rsqrt(ms + RMS_EPS) * fgain_ref[...]
    h2_o[...] = h2

    logits = _dot_f32(h2, wr_ref[...]) + br_ref[...]
    lane = lax.broadcasted_iota(jnp.int32, logits.shape, 1)
    big = jnp.int32(4 * LANES)
    ninf = -jnp.inf
    gl = jnp.where(lane < _R_GROUP + N_GROUPS, logits, ninf)
    gmax = jnp.max(gl, axis=1, keepdims=True)
    group_p = 1.0 / jnp.sum(jnp.exp(gl - gmax), axis=1, keepdims=True)
    gidx = jnp.min(jnp.where(gl == gmax, lane, big), axis=1, keepdims=True)
    elane = lane - _R_EXPERT
    in_group = (elane >= 0) & (elane < N_EXPERTS) & ((elane >> 3) == gidx)
    el = jnp.where(in_group, logits, ninf)
    m1 = jnp.max(el, axis=1, keepdims=True)
    i1 = jnp.min(jnp.where(el == m1, lane, big), axis=1, keepdims=True)
    el2 = jnp.where(lane == i1, ninf, el)
    m2 = jnp.max(el2, axis=1, keepdims=True)
    i2 = jnp.min(jnp.where(el2 == m2, lane, big), axis=1, keepdims=True)
    t = jnp.exp(m2 - m1)
    g1 = group_p / (1.0 + t)
    g2 = group_p * t / (1.0 + t)

    @pl.when(pl.program_id(0) == 0)
    def _():
        carry_ref[...] = jnp.zeros_like(carry_ref)

    sel = jnp.where((lane == i1) | (lane == i2), 1.0, 0.0)
    carry = carry_ref[0:1, :]
    rank = _dot(tri_ref[...], sel.astype(BF16)) + carry
    rank1 = jnp.sum(jnp.where(lane == i1, rank, 0.0), axis=1, keepdims=True)
    rank2 = jnp.sum(jnp.where(lane == i2, rank, 0.0), axis=1, keepdims=True)
    new_carry = carry + jnp.sum(sel, axis=0, keepdims=True)
    carry_ref[...] = jnp.broadcast_to(new_carry, carry_ref.shape)
    cnt_o[...] = jnp.broadcast_to(new_carry, cnt_o.shape)
    cols = [(i1 - _R_EXPERT).astype(F32), (i2 - _R_EXPERT).astype(F32), rank1, rank2, g1, g2]
    meta = jnp.zeros(logits.shape, F32)
    for idx, col in enumerate(cols):
        meta = jnp.where(lane == idx, col, meta)
    meta_o[...] = meta


def _out_call(x, ofox, ogdn, ggate, ogla, lr, wout, segm, gf, gg, gl, fgain, wr, br, tm):
    n, d = x.shape
    tm = min(tm, n)
    tri = (jnp.arange(tm)[:, None] > jnp.arange(tm)[None, :]).astype(BF16)
    tok = lambda w: pl.BlockSpec((tm, w), lambda i: (i, 0))
    const = lambda a: pl.BlockSpec(a.shape, lambda i: (0,) * a.ndim)
    return pl.pallas_call(
        _out_kernel,
        grid=(n // tm,),
        in_specs=[tok(d), tok(FOX_DIM), tok(GDN_DIM), tok(GDN_DIM), tok(GLA_V_DIM), tok(GLA_V_DIM),
                  const(wout), const(segm), const(gf), const(gg), const(gl), const(fgain),
                  const(wr), const(br), const(tri)],
        out_specs=[tok(d), tok(d), tok(LANES), pl.BlockSpec((8, LANES), lambda i: (0, 0))],
        out_shape=[jax.ShapeDtypeStruct((n, d), F32), jax.ShapeDtypeStruct((n, d), F32),
                   jax.ShapeDtypeStruct((n, LANES), F32), jax.ShapeDtypeStruct((8, LANES), F32)],
        scratch_shapes=[pltpu.VMEM((8, LANES), F32)],
        compiler_params=pltpu.CompilerParams(
            dimension_semantics=("arbitrary",), vmem_limit_bytes=VMEM_LIMIT),
        name="out_router",
    )(x, ofox, ogdn, ggate, ogla, lr, wout, segm, gf, gg, gl, fgain, wr, br, tri)


def _dispatch_kernel(dest_ref, h_ref, xb_in_ref, xb_ref, sem):
    del xb_in_ref
    td = h_ref.shape[0]

    def row_copy(t, d):
        return pltpu.make_async_copy(h_ref.at[pl.ds(t, 1), :], xb_ref.at[pl.ds(d, 1), :], sem)

    def issue(t, carry):
        for kk in range(TOP_K):
            row_copy(t, dest_ref[0, 0, TOP_K * t + kk]).start()
        return carry

    lax.fori_loop(0, td, issue, 0)

    def drain(t, carry):
        for kk in range(TOP_K):
            row_copy(0, 0).wait()
        return carry

    lax.fori_loop(0, td, drain, 0)


def _dispatch_call(dest, h2, n_rows, td):
    n, d = h2.shape
    td = min(td, n)
    dest3 = dest.reshape(n // td, 1, TOP_K * td)
    xb0 = jnp.zeros((n_rows, d), h2.dtype)
    return pl.pallas_call(
        _dispatch_kernel,
        grid=(n // td,),
        in_specs=[pl.BlockSpec((1, 1, TOP_K * td), lambda i: (i, 0, 0), memory_space=pltpu.SMEM),
                  pl.BlockSpec((td, d), lambda i: (i, 0)),
                  pl.BlockSpec(memory_space=pl.ANY)],
        out_specs=pl.BlockSpec(memory_space=pl.ANY),
        out_shape=jax.ShapeDtypeStruct((n_rows, d), h2.dtype),
        scratch_shapes=[pltpu.SemaphoreType.DMA(())],
        input_output_aliases={2: 0},
        compiler_params=pltpu.CompilerParams(
            dimension_semantics=("arbitrary",), has_side_effects=True),
        name="dispatch",
    )(dest3, h2, xb0)


def _expert_kernel(be_ref, nu_ref, x_ref, wg_ref, wu_ref, wd_ref, y_ref):
    del be_ref

    @pl.when(pl.program_id(0) < nu_ref[0])
    def _():
        x = x_ref[...].astype(BF16)
        a = _dot(x, wg_ref[0])
        u = _dot(x, wu_ref[0])
        hmid = (_silu(a) * u).astype(BF16)
        y_ref[...] = _dot(hmid, wd_ref[0])

    @pl.when(pl.program_id(0) >= nu_ref[0])
    def _():
        y_ref[...] = jnp.zeros_like(y_ref)


def _expert_call(block_e, n_used, xb, wg, wu, wd, tmb):
    n_rows, d = xb.shape
    de = wg.shape[-1]
    n_blocks = n_rows // tmb

    def xmap(i, be, nu):
        return (jnp.minimum(i, jnp.maximum(nu[0] - 1, 0)), 0)

    wmap = lambda i, be, nu: (be[i], 0, 0)
    return pl.pallas_call(
        _expert_kernel,
        grid_spec=pltpu.PrefetchScalarGridSpec(
            num_scalar_prefetch=2,
            grid=(n_blocks,),
            in_specs=[pl.BlockSpec((tmb, d), xmap),
                      pl.BlockSpec((1, d, de), wmap),
                      pl.BlockSpec((1, d, de), wmap),
                      pl.BlockSpec((1, de, d), wmap)],
            out_specs=pl.BlockSpec((tmb, d), lambda i, be, nu: (i, 0)),
        ),
        out_shape=jax.ShapeDtypeStruct((n_rows, d), F32),
        compiler_params=pltpu.CompilerParams(
            dimension_semantics=("arbitrary",), vmem_limit_bytes=VMEM_LIMIT),
        name="experts",
    )(block_e, n_used, xb, wg, wu, wd)


def _combine_kernel(dest_ref, x1_ref, meta_ref, yb_ref, o_ref, buf_ref, sem):
    td = x1_ref.shape[0]

    def row_copy(t, kk, d):
        return pltpu.make_async_copy(yb_ref.at[pl.ds(d, 1), :], buf_ref.at[kk, pl.ds(t, 1), :], sem)

    def issue(t, carry):
        for kk in range(TOP_K):
            row_copy(t, kk, dest_ref[0, 0, TOP_K * t + kk]).start()
        return carry

    lax.fori_loop(0, td, issue, 0)

    def drain(t, carry):
        for kk in range(TOP_K):
            row_copy(0, kk, 0).wait()
        return carry

    lax.fori_loop(0, td, drain, 0)
    meta = meta_ref[...]
    o_ref[...] = x1_ref[...] + meta[:, 4:5] * buf_ref[0] + meta[:, 5:6] * buf_ref[1]


def _combine_call(dest, x1, meta, yb, td):
    n, d = x1.shape
    td = min(td, n)
    dest3 = dest.reshape(n // td, 1, TOP_K * td)
    return pl.pallas_call(
        _combine_kernel,
        grid=(n // td,),
        in_specs=[pl.BlockSpec((1, 1, TOP_K * td), lambda i: (i, 0, 0), memory_space=pltpu.SMEM),
                  pl.BlockSpec((td, d), lambda i: (i, 0)),
                  pl.BlockSpec((td, LANES), lambda i: (i, 0)),
                  pl.BlockSpec(memory_space=pl.ANY)],
        out_specs=pl.BlockSpec((td, d), lambda i: (i, 0)),
        out_shape=jax.ShapeDtypeStruct((n, d), F32),
        scratch_shapes=[pltpu.VMEM((TOP_K, td, d), F32), pltpu.SemaphoreType.DMA(())],
        compiler_params=pltpu.CompilerParams(
            dimension_semantics=("arbitrary",), vmem_limit_bytes=VMEM_LIMIT),
        name="combine",
    )(dest3, x1, meta, yb)


TM_PROJ = 512
TQ_FOX = 256
TC_GDN = 256
TC_GLA = 256
TM_OUT = 512
TD_MOE = 256
TMB_EXPERT = 256


def _pad8(v):
    return jnp.zeros((8,), F32).at[:v.shape[0]].set(v.astype(F32)).reshape(8, 1)


def _token_mixer(x, attn_norm, w_in, fox_q_norm, fox_k_norm, fox_f_bias,
                 gdn_conv, gdn_a_log, gdn_dt_bias, gla_w_a2, gla_b_a):
    b, s, d = x.shape
    wbig = jnp.concatenate([w_in[:, _OFFS[g]:_OFFS[g + 1]] for g in _WIDE], axis=1).astype(BF16)
    wsm = jnp.zeros((d, LANES), F32)
    for pos, g in ((_SM_F, 3), (_SM_A, 6), (_SM_B, 7), (_SM_A1, 12)):
        wsm = wsm.at[:, pos:pos + _SIZES[g]].set(w_in[:, _OFFS[g]:_OFFS[g + 1]])
    wsm = wsm.astype(BF16)
    wa2 = jnp.zeros((LANES, GLA_K_DIM), F32).at[_SM_A1:_SM_A1 + GLA_RANK].set(gla_w_a2)
    segm = _seg_matrix(FOX_DIM)
    qg = (jnp.tile(fox_q_norm, H_FOX) * (HEAD_DIM ** -0.5)).reshape(1, FOX_DIM)
    kg = jnp.tile(fox_k_norm, H_FOX).reshape(1, FOX_DIM)
    outs = _proj_call(x, attn_norm.reshape(1, d), wbig, wsm, segm, qg, kg,
                      _pad8(fox_f_bias), _pad8(gdn_a_log), _pad8(gdn_dt_bias),
                      wa2, gla_b_a.reshape(1, GLA_K_DIM), TM_PROJ)
    fq, fk, fv, gqkv, ggate, lq, lk, lv, lr, la, c, g, beta = outs
    o_fox = _fox_call(fq, fk, fv, c, TQ_FOX)
    o_gdn = _gdn_call(gqkv, gdn_conv.astype(F32), g, beta, segm, TC_GDN)
    o_gla = _gla_call(lq, lk, lv, la, TC_GLA)
    return o_fox, o_gdn, ggate, o_gla, lr


def _layer(x, p):
    b, s, d = x.shape
    n = b * s
    o_fox, o_gdn, ggate, o_gla, lr = _token_mixer(
        x, p['attn_norm'], p['w_in'], p['fox_q_norm'], p['fox_k_norm'], p['fox_f_bias'],
        p['gdn_conv'], p['gdn_a_log'], p['gdn_dt_bias'], p['gla_w_a2'], p['gla_b_a'])
    wr = jnp.zeros((d, LANES), F32)
    wr = wr.at[:, _R_GROUP:_R_GROUP + N_GROUPS].set(p['w_router_group'])
    wr = wr.at[:, _R_EXPERT:_R_EXPERT + N_EXPERTS].set(p['w_router_expert'])
    br = jnp.zeros((1, LANES), F32)
    br = br.at[0, _R_GROUP:_R_GROUP + N_GROUPS].set(p['b_router_group'])
    br = br.at[0, _R_EXPERT:_R_EXPERT + N_EXPERTS].set(p['b_router_expert'])
    flat = lambda a: a.reshape(n, a.shape[-1])
    x1, h2, meta, cnt = _out_call(
        flat(x), flat(o_fox), flat(o_gdn), flat(ggate), flat(o_gla), flat(lr),
        p['w_out'].astype(BF16), _seg_matrix(FOX_DIM),
        jnp.tile(p['fox_o_norm'], H_FOX).reshape(1, FOX_DIM),
        jnp.tile(p['gdn_o_norm'], H_GDN).reshape(1, GDN_DIM),
        jnp.tile(p['gla_o_norm'], H_GLA).reshape(1, GLA_V_DIM),
        p['ffn_norm'].reshape(1, d), wr, br, TM_OUT)

    tmb = TMB_EXPERT
    counts = cnt[0, _R_EXPERT:_R_EXPERT + N_EXPERTS].astype(jnp.int32)
    padded = (counts + tmb - 1) // tmb * tmb
    pends = jnp.cumsum(padded)
    pstarts = pends - padded
    eid = meta[:, 0:TOP_K].astype(jnp.int32)
    rank = meta[:, TOP_K:2 * TOP_K].astype(jnp.int32)
    dest = (pstarts[eid] + rank).reshape(-1)
    n_blocks = -(-(n * TOP_K) // tmb) + N_EXPERTS
    block_e = jnp.minimum(
        jnp.searchsorted(pends, jnp.arange(n_blocks, dtype=jnp.int32) * tmb, side='right'),
        N_EXPERTS - 1).astype(jnp.int32)
    n_used = (pends[-1:] // tmb).astype(jnp.int32)

    xb = _dispatch_call(dest, h2, n_blocks * tmb, TD_MOE)
    yb = _expert_call(block_e, n_used, xb, p['w_expert_gate'].astype(BF16),
                      p['w_expert_up'].astype(BF16), p['w_expert_down'].astype(BF16), tmb)
    x2 = _combine_call(dest, x1, meta, yb, TD_MOE)
    return x2.reshape(b, s, d)


_PARAM_NAMES = ['attn_norm', 'w_in', 'fox_q_norm', 'fox_k_norm', 'fox_f_bias', 'fox_o_norm',
                'gdn_conv', 'gdn_a_log', 'gdn_dt_bias', 'gdn_o_norm',
                'gla_w_a2', 'gla_b_a', 'gla_o_norm', 'w_out',
                'ffn_norm', 'w_router_group', 'b_router_group', 'w_router_expert',
                'b_router_expert', 'w_expert_gate', 'w_expert_up', 'w_expert_down']


def kernel(x, attn_norm, w_in, fox_q_norm, fox_k_norm, fox_f_bias, fox_o_norm, gdn_conv, gdn_a_log, gdn_dt_bias, gdn_o_norm, gla_w_a2, gla_b_a, gla_o_norm, w_out, ffn_norm, w_router_group, b_router_group, w_router_expert, b_router_expert, w_expert_gate, w_expert_up, w_expert_down):
    params = dict(zip(_PARAM_NAMES, (
        attn_norm, w_in, fox_q_norm, fox_k_norm, fox_f_bias, fox_o_norm, gdn_conv, gdn_a_log,
        gdn_dt_bias, gdn_o_norm, gla_w_a2, gla_b_a, gla_o_norm, w_out, ffn_norm,
        w_router_group, b_router_group, w_router_expert, b_router_expert,
        w_expert_gate, w_expert_up, w_expert_down)))
    for layer in range(attn_norm.shape[0]):
        x = _layer(x, {name: val[layer] for name, val in params.items()})
    return x
```

```python
import functools

import jax
import jax.numpy as jnp
from jax import lax
from jax.experimental import pallas as pl
from jax.experimental.pallas import tpu as pltpu

F32 = jnp.float32
BF16 = jnp.bfloat16

HEAD_DIM = 64
H_FOX = 6
H_GDN = 6
H_GLA = 4
FOX_DIM = H_FOX * HEAD_DIM
GDN_DIM = H_GDN * HEAD_DIM
GLA_DK = 32
GLA_K_DIM = H_GLA * GLA_DK
GLA_V_DIM = H_GLA * HEAD_DIM
GLA_RANK = 16
GLA_TAU = 16.0
CONV_K = 4
CHUNK = 64
N_GROUPS = 4
EXPERTS_PER_GROUP = 8
N_EXPERTS = N_GROUPS * EXPERTS_PER_GROUP
TOP_K = 2
RMS_EPS = 1e-6

LANES = 128
SUBLANES = 8
VMEM_LIMIT = 56 * 1024 * 1024

_SIZES = [FOX_DIM, FOX_DIM, FOX_DIM, H_FOX, 3 * GDN_DIM, GDN_DIM, H_GDN, H_GDN,
          GLA_K_DIM, GLA_K_DIM, GLA_V_DIM, GLA_V_DIM, GLA_RANK]
_OFFS = [sum(_SIZES[:i]) for i in range(len(_SIZES) + 1)]
_WIDE = [0, 1, 2, 4, 5, 8, 9, 10, 11]
_WIDE_OFF = [0]
for _g in _WIDE:
    _WIDE_OFF.append(_WIDE_OFF[-1] + _SIZES[_g])
_SM_F, _SM_A, _SM_B, _SM_A1 = 0, 8, 16, 32
_R_GROUP, _R_EXPERT = 0, 32


def _dot(a, b):
    return jnp.dot(a, b, preferred_element_type=F32)


def _dot_nt(a, b):
    return lax.dot_general(a, b, (((1,), (1,)), ((), ())), preferred_element_type=F32)


def _dot_tn(a, b):
    return lax.dot_general(a, b, (((0,), (0,)), ((), ())), preferred_element_type=F32)


def _dot_f32(a, b):
    return jnp.dot(a, b, preferred_element_type=F32, precision=lax.Precision.HIGHEST)


def _seg_sum(sq, segm):
    hi = sq.astype(BF16)
    lo = (sq - hi.astype(F32)).astype(BF16)
    return _dot(hi, segm) + _dot(lo, segm)


def _sigmoid(x):
    return 1.0 / (1.0 + jnp.exp(-x))


def _softplus(x):
    return jnp.maximum(x, 0.0) + jnp.log1p(jnp.exp(-jnp.abs(x)))


def _log_sigmoid(x):
    return -_softplus(-x)


def _lane_cumsum(x, seg):
    lane = lax.broadcasted_iota(jnp.int32, x.shape, 1)
    pos = lane & (seg - 1)
    s = 1
    while s < seg:
        x = x + jnp.where(pos >= s, pltpu.roll(x, s, 1), 0.0)
        s *= 2
    return x


def _seg_matrix(n):
    i = jnp.arange(n) // HEAD_DIM
    return (i[:, None] == i[None, :]).astype(BF16)


_W_ROWS = 128


def _regroup_w_in(win_ref, wbf_ref):
    def body(r, carry):
        rows = pl.ds(pl.multiple_of(r * _W_ROWS, _W_ROWS), _W_ROWS)
        for i, g in enumerate(_WIDE):
            wbf_ref[rows, _WIDE_OFF[i]:_WIDE_OFF[i + 1]] = (
                win_ref[0, rows, _OFFS[g]:_OFFS[g + 1]].astype(BF16))
        small = _WIDE_OFF[-1]
        wbf_ref[rows, small:small + LANES] = jnp.zeros((_W_ROWS, LANES), BF16)
        for pos, g in ((_SM_F, 3), (_SM_A, 6), (_SM_B, 7), (_SM_A1, 12)):
            wbf_ref[rows, small + pos:small + pos + _SIZES[g]] = (
                win_ref[0, rows, _OFFS[g]:_OFFS[g + 1]].astype(BF16))
        return carry

    lax.fori_loop(0, win_ref.shape[1] // _W_ROWS, body, 0)


def _proj_kernel(x_ref, gain_ref, win_ref, segm_ref, qg_ref, kg_ref,
                 fb_ref, alog_ref, dtb_ref, wa2_ref, ba_ref, sel_ref,
                 fq_o, ka_o, fv_o, gqkv_o, ggate_o, lq_o, lk_o, lv_o, lr_o, la_o,
                 g_o, beta_o, carry_ref, wbf_ref):
    tm = x_ref.shape[1]

    @pl.when((pl.program_id(0) == 0) & (pl.program_id(1) == 0))
    def _():
        _regroup_w_in(win_ref, wbf_ref)

    x = x_ref[0]
    ms = jnp.mean(x * x, axis=-1, keepdims=True)
    hb = (x * lax.rsqrt(ms + RMS_EPS) * gain_ref[...]).astype(BF16)

    def wide(i):
        return _dot(hb, wbf_ref[:, _WIDE_OFF[i]:_WIDE_OFF[i + 1]])

    segm = segm_ref[...]
    q = wide(0)
    q = q * lax.rsqrt(_seg_sum(q * q, segm) * (1.0 / HEAD_DIM) + RMS_EPS) * qg_ref[...]
    fq_o[0] = q.astype(BF16).T
    k = wide(1)
    k = k * lax.rsqrt(_seg_sum(k * k, segm) * (1.0 / HEAD_DIM) + RMS_EPS) * kg_ref[...]
    for p in range(H_FOX // 2):
        ka_o[0, p, :, 0:LANES] = k[:, p * LANES:(p + 1) * LANES].astype(BF16)
    fv_o[0] = wide(2).astype(BF16).T
    gqkv_o[0] = wide(3).astype(BF16)
    ggate_o[0] = wide(4).astype(BF16)
    lq_o[0] = wide(5).astype(BF16)
    lk_o[0] = wide(6).astype(BF16)
    lv_o[0] = wide(7).astype(BF16)
    lr_o[0] = wide(8).astype(BF16)

    sm = _dot(hb, wbf_ref[:, _WIDE_OFF[-1]:_WIDE_OFF[-1] + LANES])
    la_logit = _dot_f32(sm, wa2_ref[...]) + ba_ref[...]
    la_o[0] = _log_sigmoid(la_logit) * (1.0 / GLA_TAU)

    smt = sm.T
    log_f = _log_sigmoid(smt[_SM_F:_SM_F + 8] + fb_ref[...])

    @pl.when(pl.program_id(1) == 0)
    def _():
        carry_ref[...] = jnp.zeros_like(carry_ref)

    cum = _lane_cumsum(log_f, tm) + carry_ref[:, 0:1]
    carry_ref[...] = jnp.broadcast_to(cum[:, tm - 1:tm], carry_ref.shape)
    pieces = []
    rem = cum
    for _ in range(_C_PIECES):
        piece = rem.astype(BF16).astype(F32)
        pieces.append(piece)
        rem = rem - piece
    pieces.append(jnp.zeros_like(cum))
    pt = jnp.concatenate(pieces, axis=0).T.astype(BF16)
    for p in range(H_FOX // 2):
        ka_o[0, p, :, LANES:2 * LANES] = _dot(pt, sel_ref[p]).astype(BF16)
    g_o[0] = -jnp.exp(alog_ref[...]) * _softplus(smt[_SM_A:_SM_A + 8] + dtb_ref[...])
    beta_o[0] = _sigmoid(smt[_SM_B:_SM_B + 8])


def _proj_call(x, gain, w_in_all, layer, segm, qg, kg, fb, alog, dtb, wa2, ba, tm):
    b, s, d = x.shape
    tm = min(tm, s)
    const = lambda shape: pl.BlockSpec(shape, lambda i, j: (0,) * len(shape),
                                       pipeline_mode=pl.Buffered(1))
    tok = lambda w: pl.BlockSpec((1, tm, w), lambda i, j: (i, j, 0))
    row = pl.BlockSpec((1, 8, tm), lambda i, j: (i, 0, j))
    widths = [_SIZES[g] for g in _WIDE]
    out_shape = ([jax.ShapeDtypeStruct((b, s, w), BF16) for w in widths]
                 + [jax.ShapeDtypeStruct((b, s, LANES), F32)]
                 + [jax.ShapeDtypeStruct((b, 8, s), F32)] * 2)
    out_specs = [tok(w) for w in widths] + [tok(LANES)] + [row] * 2
    for i in (0, 2):
        out_shape[i] = jax.ShapeDtypeStruct((b, widths[i], s), BF16)
        out_specs[i] = pl.BlockSpec((1, widths[i], tm), lambda i, j: (i, 0, j))
    npair = H_FOX // 2
    out_shape[1] = jax.ShapeDtypeStruct((b, npair, s, 2 * LANES), BF16)
    out_specs[1] = pl.BlockSpec((1, npair, tm, 2 * LANES), lambda i, j: (i, 0, j, 0))
    src = jnp.arange(4 * SUBLANES)
    piece, head = src // SUBLANES, src % SUBLANES
    pair = jnp.arange(npair)[:, None, None]
    lane = jnp.arange(LANES)[None, None, :]
    sel = ((head[None, :, None] // 2 == pair) & (piece[None, :, None] < _C_PIECES)
           & (lane == (head % 2 * _C_PIECES + piece)[None, :, None])).astype(BF16)
    return pl.pallas_call(
        _proj_kernel,
        grid=(b, s // tm),
        in_specs=[tok(d), const((1, d)),
                  pl.BlockSpec((1,) + w_in_all.shape[1:], lambda i, j: (layer, 0, 0),
                               pipeline_mode=pl.Buffered(1)),
                  const(segm.shape),
                  const(qg.shape), const(kg.shape), const((8, 1)), const((8, 1)), const((8, 1)),
                  const(wa2.shape), const(ba.shape), const(sel.shape)],
        out_specs=out_specs,
        out_shape=out_shape,
        scratch_shapes=[pltpu.VMEM((8, LANES), F32),
                        pltpu.VMEM((d, _WIDE_OFF[-1] + LANES), BF16)],
        compiler_params=pltpu.CompilerParams(
            dimension_semantics=("arbitrary", "arbitrary"), vmem_limit_bytes=VMEM_LIMIT),
        name="proj",
    )(x, gain, w_in_all, segm, qg, kg, fb, alog, dtb, wa2, ba, sel)


_C_PIECES = 3
_ACC_ROWS = HEAD_DIM + 16


def _fox_kernel(qt_ref, k_ref, vt_ref, aug_ref, ones_ref, o_ref, m_ref, acc_ref):
    tq = qt_ref.shape[2]
    qi = pl.program_id(1)
    heads = range(H_FOX)
    row = lax.broadcasted_iota(jnp.int32, (LANES, tq), 0)
    first = row < HEAD_DIM
    zero = jnp.zeros((LANES, tq), BF16)
    qts = []
    for h in heads:
        qt = qt_ref[0, (h // 2) * LANES:(h // 2 + 1) * LANES, :]
        keep = first if h % 2 == 0 else jnp.logical_not(first)
        qts.append(jnp.concatenate([jnp.where(keep, qt, zero), aug_ref[h % 2]], axis=0))
    m_ref[...] = jnp.full_like(m_ref, -jnp.inf)
    acc_ref[...] = jnp.zeros_like(acc_ref)
    krow = lax.broadcasted_iota(jnp.int32, (tq, tq), 0)
    qcol = lax.broadcasted_iota(jnp.int32, (tq, tq), 1)
    ones = ones_ref[...]

    def step(j, masked):
        k0 = pl.multiple_of(j * tq, tq)
        ss = [_dot(k_ref[0, h // 2, pl.ds(k0, tq), :], qts[h]) for h in heads]
        if masked:
            ss = [jnp.where(krow <= qcol, s, -jnp.inf) for s in ss]
        m_prev = [m_ref[h] for h in heads]
        m_new = [jnp.maximum(m_prev[h], jnp.max(ss[h], axis=0, keepdims=True)) for h in heads]
        ps = [jnp.exp(ss[h] - m_new[h]).astype(BF16) for h in heads]
        for h in heads:
            vta = jnp.concatenate(
                [vt_ref[0, h * HEAD_DIM:(h + 1) * HEAD_DIM, pl.ds(k0, tq)], ones], axis=0)
            acc_ref[h] = jnp.exp(m_prev[h] - m_new[h]) * acc_ref[h] + _dot(vta, ps[h])
            m_ref[h] = m_new[h]

    step(qi, True)

    def body(j, carry):
        step(j, False)
        return carry

    lax.fori_loop(0, qi, body, 0)
    outs = []
    for h in heads:
        acc = acc_ref[h]
        outs.append(acc[0:HEAD_DIM] / acc[HEAD_DIM:HEAD_DIM + 1])
    o_ref[0] = jnp.concatenate(outs, axis=0).T.astype(o_ref.dtype)


def _fox_call(qt, ka, vt, tq):
    b, _, s = qt.shape
    tq = min(tq, s)
    npair = H_FOX // 2
    r = jnp.arange(LANES)[None, :, None]
    hh = jnp.arange(2)[:, None, None]
    aug = jnp.where((r >= hh * _C_PIECES) & (r < (hh + 1) * _C_PIECES), -1.0, 0.0)
    aug = jnp.broadcast_to(aug, (2, LANES, tq)).astype(BF16)
    ones = jnp.broadcast_to(jnp.where(jnp.arange(16)[:, None] == 0, 1.0, 0.0), (16, tq)).astype(BF16)
    return pl.pallas_call(
        _fox_kernel,
        grid=(b, s // tq),
        in_specs=[pl.BlockSpec((1, FOX_DIM, tq), lambda i, j: (i, 0, j)),
                  pl.BlockSpec((1, npair, s, 2 * LANES), lambda i, j: (i, 0, 0, 0)),
                  pl.BlockSpec((1, FOX_DIM, s), lambda i, j: (i, 0, 0)),
                  pl.BlockSpec(aug.shape, lambda i, j: (0, 0, 0)),
                  pl.BlockSpec(ones.shape, lambda i, j: (0, 0))],
        out_specs=pl.BlockSpec((1, tq, FOX_DIM), lambda i, j: (i, j, 0)),
        out_shape=jax.ShapeDtypeStruct((b, s, FOX_DIM), BF16),
        scratch_shapes=[pltpu.VMEM((H_FOX, 1, tq), F32),
                        pltpu.VMEM((H_FOX, _ACC_ROWS, tq), F32)],
        compiler_params=pltpu.CompilerParams(
            dimension_semantics=("arbitrary", "arbitrary"),
            vmem_limit_bytes=VMEM_LIMIT),
        name="fox",
    )(qt, ka, vt, aug, ones)


def _bmm(a, b):
    return jnp.einsum('gmk,gkn->gmn', a.astype(BF16), b.astype(BF16),
                      preferred_element_type=F32)


def _bmm_nt(a, b):
    return jnp.einsum('gmk,gnk->gmn', a.astype(BF16), b.astype(BF16),
                      preferred_element_type=F32)


def _unit_lower_inverse(a_strict, eye):
    n = -a_strict
    t = eye + n
    size = 1
    while 2 * size < CHUNK:
        n = _bmm(n, n)
        t = t + _bmm(t, n)
        size *= 2
    return t


def _gdn_kernel(qkv_ref, convw_ref, g_ref, beta_ref, segm_ref, o_ref, state_ref, xext_ref):
    tc = qkv_ref.shape[1]
    pad = SUBLANES

    @pl.when(pl.program_id(1) == 0)
    def _():
        state_ref[...] = jnp.zeros_like(state_ref)
        xext_ref[0:pad, :] = jnp.zeros((pad, xext_ref.shape[1]), F32)

    x = qkv_ref[0].astype(F32)
    xext_ref[pad:pad + tc, :] = x
    y = convw_ref[0:1, :] * xext_ref[pad - 3:pad - 3 + tc, :]
    for i in range(1, CONV_K):
        y = y + convw_ref[i:i + 1, :] * xext_ref[pad - 3 + i:pad - 3 + i + tc, :]
    xext_ref[0:pad, :] = x[tc - pad:tc, :]
    y = y * _sigmoid(y)

    segm = segm_ref[...]
    q = y[:, 0:GDN_DIM]
    k = y[:, GDN_DIM:2 * GDN_DIM]
    v = y[:, 2 * GDN_DIM:3 * GDN_DIM]
    q = q * lax.rsqrt(_seg_sum(q * q, segm) + RMS_EPS) * (HEAD_DIM ** -0.5)
    k = k * lax.rsqrt(_seg_sum(k * k, segm) + RMS_EPS)

    kt = k.T

    dec_row = _lane_cumsum(g_ref[0], CHUNK)
    dec_col = dec_row.T
    beta_col = beta_ref[0].T

    nc = tc // CHUNK
    index = [(c, h) for c in range(nc) for h in range(H_GDN)]

    def split(a):
        return jnp.stack([a[c * CHUNK:(c + 1) * CHUNK, h * HEAD_DIM:(h + 1) * HEAD_DIM]
                          for c, h in index])

    def split_col(a):
        return jnp.stack([a[c * CHUNK:(c + 1) * CHUNK, h:h + 1] for c, h in index])

    q3, k3, v3 = split(q), split(k), split(v)
    kt3 = jnp.stack([kt[h * HEAD_DIM:(h + 1) * HEAD_DIM, c * CHUNK:(c + 1) * CHUNK]
                     for c, h in index])
    dcol = split_col(dec_col)
    bcol = split_col(beta_col)
    drow = jnp.stack([dec_row[h:h + 1, c * CHUNK:(c + 1) * CHUNK] for c, h in index])
    dlast = dcol[:, CHUNK - 1:CHUNK, :]

    ri = lax.broadcasted_iota(jnp.int32, (1, CHUNK, CHUNK), 1)
    ci = lax.broadcasted_iota(jnp.int32, (1, CHUNK, CHUNK), 2)
    causal = ci <= ri
    eye = (ci == ri).astype(F32)
    gamma = jnp.where(causal, jnp.exp(jnp.where(causal, dcol - drow, 0.0)), 0.0)
    edec = jnp.exp(dcol)
    kb = k3 * bcol
    a = jnp.where(ci < ri, _bmm_nt(kb, k3) * gamma, 0.0)
    t = _unit_lower_inverse(a, eye)
    u3 = _bmm(t, v3 * bcol)
    w3 = _bmm(t, kb * edec)
    intra = _bmm_nt(q3, k3) * gamma
    qd = q3 * edec
    kdt = kt3 * jnp.exp(dlast - drow)
    elast = jnp.exp(dlast)

    s = state_ref[...]
    for c in range(nc):
        sl = slice(c * H_GDN, (c + 1) * H_GDN)
        v_new = u3[sl] - _bmm(w3[sl], s)
        o = _bmm(qd[sl], s) + _bmm(intra[sl], v_new)
        s = s * elast[sl] + _bmm(kdt[sl], v_new)
        o_ref[0, c * CHUNK:(c + 1) * CHUNK, :] = jnp.concatenate(
            [o[h] for h in range(H_GDN)], axis=1).astype(o_ref.dtype)
    state_ref[...] = s


def _gdn_call(qkv, convw, g, beta, segm, tc):
    b, s, w = qkv.shape
    tc = min(tc, s)
    return pl.pallas_call(
        _gdn_kernel,
        grid=(b, s // tc),
        in_specs=[pl.BlockSpec((1, tc, w), lambda i, j: (i, j, 0)),
                  pl.BlockSpec(convw.shape, lambda i, j: (0, 0)),
                  pl.BlockSpec((1, 8, tc), lambda i, j: (i, 0, j)),
                  pl.BlockSpec((1, 8, tc), lambda i, j: (i, 0, j)),
                  pl.BlockSpec(segm.shape, lambda i, j: (0, 0))],
        out_specs=pl.BlockSpec((1, tc, GDN_DIM), lambda i, j: (i, j, 0)),
        out_shape=jax.ShapeDtypeStruct((b, s, GDN_DIM), BF16),
        scratch_shapes=[pltpu.VMEM((H_GDN, HEAD_DIM, HEAD_DIM), F32),
                        pltpu.VMEM((tc + SUBLANES, w), F32)],
        compiler_params=pltpu.CompilerParams(
            dimension_semantics=("arbitrary", "arbitrary"), vmem_limit_bytes=VMEM_LIMIT),
        name="gdn",
    )(qkv, convw, g, beta, segm)


def _gla_kernel(q_ref, k_ref, v_ref, la_ref, tri_ref, mh_ref, mt_ref, o_ref,
                st_ref, kf_ref, bq_ref, vf_ref):
    tc = q_ref.shape[1]

    @pl.when(pl.program_id(1) == 0)
    def _():
        st_ref[...] = jnp.zeros_like(st_ref)

    row = lax.broadcasted_iota(jnp.int32, (CHUNK, GLA_K_DIM), 0)
    mh = mh_ref[...]
    for c in range(tc // CHUNK):
        r0, r1 = c * CHUNK, (c + 1) * CHUNK
        bq = _dot_f32(tri_ref[...], la_ref[0, r0:r1, :])
        q = q_ref[0, r0:r1, :].astype(F32) * (GLA_DK ** -0.5)
        k = k_ref[0, r0:r1, :].astype(F32)
        v = v_ref[0, r0:r1, :].astype(F32)
        kf_ref[...] = k
        bq_ref[...] = bq
        vf_ref[...] = v
        st = st_ref[...]
        o_inter = _dot_nt((q * jnp.exp(bq)).astype(BF16), st.astype(BF16))

        group_out = []
        for g0 in range(0, CHUNK, SUBLANES):
            n = CHUNK - g0
            qg, bg, rg = q[g0:, :], bq[g0:, :], row[g0:, :]
            es = []
            for j in range(g0, g0 + SUBLANES):
                kj = kf_ref[j:j + 1, :]
                bj = bq_ref[j:j + 1, :]
                es.append(jnp.where(rg >= j, jnp.exp(bg - bj), 0.0) * (qg * kj))
            p = _dot(jnp.concatenate(es, axis=0).astype(BF16), mh)
            acc = p[0:n] * vf_ref[g0:g0 + 1, :]
            for jj in range(1, SUBLANES):
                acc = acc + p[jj * n:(jj + 1) * n] * vf_ref[g0 + jj:g0 + jj + 1, :]
            group_out.append(acc)
        pieces = []
        for r in range(0, CHUNK, SUBLANES):
            piece = o_inter[r:r + SUBLANES]
            for gi, g0 in enumerate(range(0, r + SUBLANES, SUBLANES)):
                piece = piece + group_out[gi][r - g0:r - g0 + SUBLANES]
            pieces.append(piece)
        o_ref[0, r0:r1, :] = jnp.concatenate(pieces, axis=0).astype(o_ref.dtype)
        blast = bq[CHUNK - 1:CHUNK, :]
        kd = k * jnp.exp(blast - bq)
        upd = _dot_tn(v.astype(BF16), kd.astype(BF16))
        st_ref[...] = (st * jnp.exp(blast) + upd) * mt_ref[...]


def _gla_call(q, k, v, la, tc):
    b, s, _ = q.shape
    tc = min(tc, s)
    tri = (jnp.arange(CHUNK)[:, None] >= jnp.arange(CHUNK)[None, :]).astype(F32)
    hk = jnp.arange(GLA_K_DIM) // GLA_DK
    hv = jnp.arange(GLA_V_DIM) // HEAD_DIM
    mh = (hk[:, None] == hv[None, :]).astype(BF16)
    mt = (hv[:, None] == hk[None, :]).astype(F32)
    tok = lambda w: pl.BlockSpec((1, tc, w), lambda i, j: (i, j, 0))
    const = lambda a: pl.BlockSpec(a.shape, lambda i, j: (0, 0))
    return pl.pallas_call(
        _gla_kernel,
        grid=(b, s // tc),
        in_specs=[tok(GLA_K_DIM), tok(GLA_K_DIM), tok(GLA_V_DIM), tok(GLA_K_DIM),
                  const(tri), const(mh), const(mt)],
        out_specs=tok(GLA_V_DIM),
        out_shape=jax.ShapeDtypeStruct((b, s, GLA_V_DIM), BF16),
        scratch_shapes=[pltpu.VMEM((GLA_V_DIM, GLA_K_DIM), F32),
                        pltpu.VMEM((CHUNK, GLA_K_DIM), F32),
                        pltpu.VMEM((CHUNK, GLA_K_DIM), F32),
                        pltpu.VMEM((CHUNK, GLA_V_DIM), F32)],
        compiler_params=pltpu.CompilerParams(
            dimension_semantics=("arbitrary", "arbitrary"), vmem_limit_bytes=VMEM_LIMIT),
        name="gla",
    )(q, k, v, la, tri, mh, mt)


def _head_norm(o_ref, segm, gain):
    o = o_ref[...].astype(F32)
    return o * lax.rsqrt(_seg_sum(o * o, segm) * (1.0 / HEAD_DIM) + RMS_EPS) * gain


def _silu(x):
    return x * _sigmoid(x)


def _out_kernel(x_ref, ofox_ref, ogdn_ref, ggate_ref, ogla_ref, lr_ref, wout_ref, segm_ref,
                gf_ref, gg_ref, gl_ref, fgain_ref, wr_ref, br_ref, tri_ref,
                x1_o, h2_o, meta_o, cnt_o, carry_ref):
    tm = x_ref.shape[0]
    segm = segm_ref[...]
    a = _head_norm(ofox_ref, segm, gf_ref[...])
    bb = _head_norm(ogdn_ref, segm, gg_ref[...]) * _silu(ggate_ref[...].astype(F32))
    cc = (_head_norm(ogla_ref, segm[:GLA_V_DIM, :GLA_V_DIM], gl_ref[...])
          * _silu(lr_ref[...].astype(F32)))
    y = (x_ref[...]
         + _dot(a.astype(BF16), wout_ref[0:FOX_DIM, :].astype(BF16))
         + _dot(bb.astype(BF16), wout_ref[FOX_DIM:FOX_DIM + GDN_DIM, :].astype(BF16))
         + _dot(cc.astype(BF16), wout_ref[FOX_DIM + GDN_DIM:, :].astype(BF16)))
    x1_o[...] = y
    ms = jnp.mean(y * y, axis=-1, keepdims=True)
    h2 = y * lax.rsqrt(ms + RMS_EPS) * fgain_ref[...]
    h2_o[...] = h2

    wr = wr_ref[...]
    wr_hi = wr.astype(BF16)
    wr_lo = (wr - wr_hi.astype(F32)).astype(BF16)
    h2_hi = h2.astype(BF16)
    h2_lo = (h2 - h2_hi.astype(F32)).astype(BF16)
    logits = (_dot(h2_hi, wr_hi) + (_dot(h2_hi, wr_lo) + _dot(h2_lo, wr_hi))
              + br_ref[...])
    lane = lax.broadcasted_iota(jnp.int32, logits.shape, 1).astype(F32)
    big = float(4 * LANES)
    ninf = -jnp.inf
    gl = jnp.where(lane < _R_GROUP + N_GROUPS, logits, ninf)
    gmax = jnp.max(gl, axis=1, keepdims=True)
    group_p = 1.0 / jnp.sum(jnp.exp(gl - gmax), axis=1, keepdims=True)
    gidx = jnp.min(jnp.where(gl == gmax, lane, big), axis=1, keepdims=True)
    elane = lane - _R_EXPERT
    group_of_lane = jnp.floor(elane * (1.0 / EXPERTS_PER_GROUP))
    in_group = (elane >= 0) & (elane < N_EXPERTS) & (group_of_lane == gidx)
    el = jnp.where(in_group, logits, ninf)
    m1 = jnp.max(el, axis=1, keepdims=True)
    i1 = jnp.min(jnp.where(el == m1, lane, big), axis=1, keepdims=True)
    el2 = jnp.where(lane == i1, ninf, el)
    m2 = jnp.max(el2, axis=1, keepdims=True)
    i2 = jnp.min(jnp.where(el2 == m2, lane, big), axis=1, keepdims=True)
    t = jnp.exp(m2 - m1)
    g1 = group_p / (1.0 + t)
    g2 = group_p * t / (1.0 + t)

    @pl.when(pl.program_id(0) == 0)
    def _():
        carry_ref[...] = jnp.zeros_like(carry_ref)

    sel = jnp.where((lane == i1) | (lane == i2), 1.0, 0.0)
    carry = carry_ref[0:1, :]
    rank = _dot(tri_ref[...], sel.astype(BF16)) + carry
    rank1 = jnp.sum(jnp.where(lane == i1, rank, 0.0), axis=1, keepdims=True)
    rank2 = jnp.sum(jnp.where(lane == i2, rank, 0.0), axis=1, keepdims=True)
    new_carry = carry + jnp.sum(sel, axis=0, keepdims=True)
    carry_ref[...] = jnp.broadcast_to(new_carry, carry_ref.shape)
    cnt_o[...] = jnp.broadcast_to(new_carry, cnt_o.shape)
    cols = [i1 - _R_EXPERT, i2 - _R_EXPERT, rank1, rank2, g1, g2]
    meta = jnp.zeros(logits.shape, F32)
    for idx, col in enumerate(cols):
        meta = jnp.where(lane == idx, col, meta)
    meta_o[...] = meta


def _out_call(x, ofox, ogdn, ggate, ogla, lr, wout, segm, gf, gg, gl, fgain, wr, br, tm):
    n, d = x.shape
    tm = min(tm, n)
    tri = (jnp.arange(tm)[:, None] > jnp.arange(tm)[None, :]).astype(BF16)
    tok = lambda w: pl.BlockSpec((tm, w), lambda i: (i, 0))
    const = lambda a: pl.BlockSpec(a.shape, lambda i: (0,) * a.ndim,
                                   pipeline_mode=pl.Buffered(1))
    return pl.pallas_call(
        _out_kernel,
        grid=(n // tm,),
        in_specs=[tok(d), tok(FOX_DIM), tok(GDN_DIM), tok(GDN_DIM), tok(GLA_V_DIM), tok(GLA_V_DIM),
                  const(wout), const(segm), const(gf), const(gg), const(gl), const(fgain),
                  const(wr), const(br), const(tri)],
        out_specs=[tok(d), tok(d), tok(LANES), pl.BlockSpec((8, LANES), lambda i: (0, 0))],
        out_shape=[jax.ShapeDtypeStruct((n, d), F32), jax.ShapeDtypeStruct((n, d), F32),
                   jax.ShapeDtypeStruct((n, LANES), F32), jax.ShapeDtypeStruct((8, LANES), F32)],
        scratch_shapes=[pltpu.VMEM((8, LANES), F32)],
        compiler_params=pltpu.CompilerParams(
            dimension_semantics=("arbitrary",), vmem_limit_bytes=VMEM_LIMIT),
        name="out_router",
    )(x, ofox, ogdn, ggate, ogla, lr, wout, segm, gf, gg, gl, fgain, wr, br, tri)


_DMA_UNROLL = 8


def _dispatch_kernel(dest_ref, h_ref, xb_in_ref, xb_ref, sem):
    del xb_in_ref
    td = h_ref.shape[0]

    def row_copy(t, d):
        return pltpu.make_async_copy(h_ref.at[pl.ds(t, 1), :], xb_ref.at[pl.ds(d, 1), :], sem)

    def issue(t, carry):
        for kk in range(TOP_K):
            row_copy(t, dest_ref[0, 0, TOP_K * t + kk]).start()
        return carry

    lax.fori_loop(0, td, issue, 0, unroll=_DMA_UNROLL)

    def drain(t, carry):
        for kk in range(TOP_K):
            row_copy(0, 0).wait()
        return carry

    lax.fori_loop(0, td, drain, 0, unroll=_DMA_UNROLL)


def _dispatch_call(dest, h2, n_rows, td):
    n, d = h2.shape
    td = min(td, n)
    dest3 = dest.reshape(n // td, 1, TOP_K * td)
    xb0 = jnp.zeros((n_rows, d), h2.dtype)
    return pl.pallas_call(
        _dispatch_kernel,
        grid=(n // td,),
        in_specs=[pl.BlockSpec((1, 1, TOP_K * td), lambda i: (i, 0, 0), memory_space=pltpu.SMEM),
                  pl.BlockSpec((td, d), lambda i: (i, 0)),
                  pl.BlockSpec(memory_space=pl.ANY)],
        out_specs=pl.BlockSpec(memory_space=pl.ANY),
        out_shape=jax.ShapeDtypeStruct((n_rows, d), h2.dtype),
        scratch_shapes=[pltpu.SemaphoreType.DMA(())],
        input_output_aliases={2: 0},
        compiler_params=pltpu.CompilerParams(
            dimension_semantics=("arbitrary",), has_side_effects=True),
        name="dispatch",
    )(dest3, h2, xb0)


def _expert_kernel(be_ref, nu_ref, x_ref, wg_ref, wu_ref, wd_ref, y_ref):
    del be_ref

    @pl.when(pl.program_id(0) < nu_ref[0])
    def _():
        x = x_ref[...].astype(BF16)
        a = _dot(x, wg_ref[0].astype(BF16))
        u = _dot(x, wu_ref[0].astype(BF16))
        hmid = (_silu(a) * u).astype(BF16)
        y_ref[...] = _dot(hmid, wd_ref[0].astype(BF16))

    @pl.when(pl.program_id(0) >= nu_ref[0])
    def _():
        y_ref[...] = jnp.zeros_like(y_ref)


def _expert_call(block_e, n_used, xb, wg, wu, wd, tmb):
    n_rows, d = xb.shape
    de = wg.shape[-1]
    n_blocks = n_rows // tmb

    def xmap(i, be, nu):
        return (jnp.minimum(i, jnp.maximum(nu[0] - 1, 0)), 0)

    wmap = lambda i, be, nu: (be[i], 0, 0)
    return pl.pallas_call(
        _expert_kernel,
        grid_spec=pltpu.PrefetchScalarGridSpec(
            num_scalar_prefetch=2,
            grid=(n_blocks,),
            in_specs=[pl.BlockSpec((tmb, d), xmap),
                      pl.BlockSpec((1, d, de), wmap),
                      pl.BlockSpec((1, d, de), wmap),
                      pl.BlockSpec((1, de, d), wmap)],
            out_specs=pl.BlockSpec((tmb, d), lambda i, be, nu: (i, 0)),
        ),
        out_shape=jax.ShapeDtypeStruct((n_rows, d), F32),
        compiler_params=pltpu.CompilerParams(
            dimension_semantics=("arbitrary",), vmem_limit_bytes=VMEM_LIMIT),
        name="experts",
    )(block_e, n_used, xb, wg, wu, wd)


def _combine_kernel(dest_ref, x1_ref, meta_ref, yb_ref, o_ref, buf_ref, sem):
    td = x1_ref.shape[0]

    def row_copy(t, kk, d):
        return pltpu.make_async_copy(yb_ref.at[pl.ds(d, 1), :], buf_ref.at[kk, pl.ds(t, 1), :], sem)

    def issue(t, carry):
        for kk in range(TOP_K):
            row_copy(t, kk, dest_ref[0, 0, TOP_K * t + kk]).start()
        return carry

    lax.fori_loop(0, td, issue, 0, unroll=_DMA_UNROLL)

    def drain(t, carry):
        for kk in range(TOP_K):
            row_copy(0, kk, 0).wait()
        return carry

    lax.fori_loop(0, td, drain, 0, unroll=_DMA_UNROLL)
    meta = meta_ref[...]
    o_ref[...] = x1_ref[...] + meta[:, 4:5] * buf_ref[0] + meta[:, 5:6] * buf_ref[1]


def _combine_call(dest, x1, meta, yb, td):
    n, d = x1.shape
    td = min(td, n)
    dest3 = dest.reshape(n // td, 1, TOP_K * td)
    return pl.pallas_call(
        _combine_kernel,
        grid=(n // td,),
        in_specs=[pl.BlockSpec((1, 1, TOP_K * td), lambda i: (i, 0, 0), memory_space=pltpu.SMEM),
                  pl.BlockSpec((td, d), lambda i: (i, 0)),
                  pl.BlockSpec((td, LANES), lambda i: (i, 0)),
                  pl.BlockSpec(memory_space=pl.ANY)],
        out_specs=pl.BlockSpec((td, d), lambda i: (i, 0)),
        out_shape=jax.ShapeDtypeStruct((n, d), F32),
        scratch_shapes=[pltpu.VMEM((TOP_K, td, d), F32), pltpu.SemaphoreType.DMA(())],
        compiler_params=pltpu.CompilerParams(
            dimension_semantics=("arbitrary",), vmem_limit_bytes=VMEM_LIMIT),
        name="combine",
    )(dest3, x1, meta, yb)


TM_PROJ = 512
TQ_FOX = 256
TC_GDN = 256
TC_GLA = 256
TM_OUT = 512
TD_MOE = 256
TMB_EXPERT = 256


def _place(width, parts):
    cols, at = [], 0
    for pos, blk in parts:
        if pos > at:
            cols.append(jnp.zeros((blk.shape[0], pos - at), blk.dtype))
        cols.append(blk)
        at = pos + blk.shape[1]
    if width > at:
        cols.append(jnp.zeros((parts[0][1].shape[0], width - at), parts[0][1].dtype))
    return jnp.concatenate(cols, axis=1)


def _pad8(v):
    return jnp.zeros((8,), F32).at[:v.shape[0]].set(v.astype(F32)).reshape(8, 1)


def _token_mixer(x, attn_norm, w_in_all, layer, fox_q_norm, fox_k_norm, fox_f_bias,
                 gdn_conv, gdn_a_log, gdn_dt_bias, gla_w_a2, gla_b_a):
    b, s, d = x.shape
    wa2 = jnp.pad(gla_w_a2, ((_SM_A1, LANES - _SM_A1 - GLA_RANK), (0, 0)))
    segm = _seg_matrix(FOX_DIM)
    qg = (jnp.tile(fox_q_norm, H_FOX) * (HEAD_DIM ** -0.5)).reshape(1, FOX_DIM)
    kg = jnp.tile(fox_k_norm, H_FOX).reshape(1, FOX_DIM)
    outs = _proj_call(x, attn_norm.reshape(1, d), w_in_all, layer, segm, qg, kg,
                      _pad8(fox_f_bias), _pad8(gdn_a_log), _pad8(gdn_dt_bias),
                      wa2, gla_b_a.reshape(1, GLA_K_DIM), TM_PROJ)
    fq, fka, fv, gqkv, ggate, lq, lk, lv, lr, la, g, beta = outs
    o_fox = _fox_call(fq, fka, fv, TQ_FOX)
    o_gdn = _gdn_call(gqkv, gdn_conv.astype(F32), g, beta, segm, TC_GDN)
    o_gla = _gla_call(lq, lk, lv, la, TC_GLA)
    return o_fox, o_gdn, ggate, o_gla, lr


def _layer(x, p, layer, w_in_all, experts):
    b, s, d = x.shape
    n = b * s
    o_fox, o_gdn, ggate, o_gla, lr = _token_mixer(
        x, p['attn_norm'], w_in_all, layer, p['fox_q_norm'], p['fox_k_norm'], p['fox_f_bias'],
        p['gdn_conv'], p['gdn_a_log'], p['gdn_dt_bias'], p['gla_w_a2'], p['gla_b_a'])
    wr = _place(LANES, [(_R_GROUP, p['w_router_group']), (_R_EXPERT, p['w_router_expert'])])
    br = _place(LANES, [(_R_GROUP, p['b_router_group'].reshape(1, -1)),
                        (_R_EXPERT, p['b_router_expert'].reshape(1, -1))])
    flat = lambda a: a.reshape(n, a.shape[-1])
    x1, h2, meta, cnt = _out_call(
        flat(x), flat(o_fox), flat(o_gdn), flat(ggate), flat(o_gla), flat(lr),
        p['w_out'], _seg_matrix(FOX_DIM),
        jnp.tile(p['fox_o_norm'], H_FOX).reshape(1, FOX_DIM),
        jnp.tile(p['gdn_o_norm'], H_GDN).reshape(1, GDN_DIM),
        jnp.tile(p['gla_o_norm'], H_GLA).reshape(1, GLA_V_DIM),
        p['ffn_norm'].reshape(1, d), wr, br, TM_OUT)

    tmb = TMB_EXPERT
    counts = cnt[0, _R_EXPERT:_R_EXPERT + N_EXPERTS].astype(jnp.int32)
    padded = (counts + tmb - 1) // tmb * tmb
    pends = jnp.cumsum(padded)
    pstarts = pends - padded
    eid = meta[:, 0:TOP_K].astype(jnp.int32)
    rank = meta[:, TOP_K:2 * TOP_K].astype(jnp.int32)
    expert_ids = jnp.arange(N_EXPERTS, dtype=jnp.int32)
    start_of = jnp.sum(jnp.where(eid[..., None] == expert_ids, pstarts, 0), axis=-1)
    dest = (start_of + rank).reshape(-1)
    n_blocks = -(-(n * TOP_K) // tmb) + N_EXPERTS
    block_start = jnp.arange(n_blocks, dtype=jnp.int32) * tmb
    block_e = jnp.minimum(jnp.sum(pends[None, :] <= block_start[:, None], axis=1),
                          N_EXPERTS - 1).astype(jnp.int32)
    n_used = (pends[-1:] // tmb).astype(jnp.int32)

    xb = _dispatch_call(dest, h2, n_blocks * tmb, TD_MOE)
    yb = _expert_call(block_e + layer * N_EXPERTS, n_used, xb, *experts, tmb)
    x2 = _combine_call(dest, x1, meta, yb, TD_MOE)
    return x2.reshape(b, s, d)


_PARAM_NAMES = ['attn_norm', 'w_in', 'fox_q_norm', 'fox_k_norm', 'fox_f_bias', 'fox_o_norm',
                'gdn_conv', 'gdn_a_log', 'gdn_dt_bias', 'gdn_o_norm',
                'gla_w_a2', 'gla_b_a', 'gla_o_norm', 'w_out',
                'ffn_norm', 'w_router_group', 'b_router_group', 'w_router_expert',
                'b_router_expert', 'w_expert_gate', 'w_expert_up', 'w_expert_down']


def kernel(x, attn_norm, w_in, fox_q_norm, fox_k_norm, fox_f_bias, fox_o_norm, gdn_conv, gdn_a_log, gdn_dt_bias, gdn_o_norm, gla_w_a2, gla_b_a, gla_o_norm, w_out, ffn_norm, w_router_group, b_router_group, w_router_expert, b_router_expert, w_expert_gate, w_expert_up, w_expert_down):
    params = dict(zip(_PARAM_NAMES, (
        attn_norm, w_in, fox_q_norm, fox_k_norm, fox_f_bias, fox_o_norm, gdn_conv, gdn_a_log,
        gdn_dt_bias, gdn_o_norm, gla_w_a2, gla_b_a, gla_o_norm, w_out, ffn_norm,
        w_router_group, b_router_group, w_router_expert, b_router_expert,
        w_expert_gate, w_expert_up, w_expert_down)))
    experts = tuple(params.pop(name).reshape((-1,) + params_shape[2:])
                    for name, params_shape in (('w_expert_gate', w_expert_gate.shape),
                                               ('w_expert_up', w_expert_up.shape),
                                               ('w_expert_down', w_expert_down.shape)))
    del params['w_in']
    for layer in range(attn_norm.shape[0]):
        x = _layer(x, {name: val[layer] for name, val in params.items()}, layer, w_in, experts)
    return x
```

```python
import functools

import jax
import jax.numpy as jnp
from jax import lax
from jax.experimental import pallas as pl
from jax.experimental.pallas import tpu as pltpu

F32 = jnp.float32
BF16 = jnp.bfloat16

HEAD_DIM = 64
H_FOX = 6
H_GDN = 6
H_GLA = 4
FOX_DIM = H_FOX * HEAD_DIM
GDN_DIM = H_GDN * HEAD_DIM
GLA_DK = 32
GLA_K_DIM = H_GLA * GLA_DK
GLA_V_DIM = H_GLA * HEAD_DIM
GLA_RANK = 16
GLA_TAU = 16.0
CONV_K = 4
CHUNK = 64
N_GROUPS = 4
EXPERTS_PER_GROUP = 8
N_EXPERTS = N_GROUPS * EXPERTS_PER_GROUP
TOP_K = 2
RMS_EPS = 1e-6

_LOG2E = 1.4426950408889634
_C_PIECES = 3

LANES = 128
SUBLANES = 8
VMEM_LIMIT = 56 * 1024 * 1024

_SIZES = [FOX_DIM, FOX_DIM, FOX_DIM, H_FOX, 3 * GDN_DIM, GDN_DIM, H_GDN, H_GDN,
          GLA_K_DIM, GLA_K_DIM, GLA_V_DIM, GLA_V_DIM, GLA_RANK]
_OFFS = [sum(_SIZES[:i]) for i in range(len(_SIZES) + 1)]
_WIDE = [0, 1, 2, 4, 5, 8, 9, 10, 11]
_WIDE_OFF = [0]
for _g in _WIDE:
    _WIDE_OFF.append(_WIDE_OFF[-1] + _SIZES[_g])
_SM_F, _SM_A, _SM_B, _SM_A1 = 0, 8, 16, 32
_R_GROUP, _R_EXPERT = 0, 32


def _dot(a, b):
    return jnp.dot(a, b, preferred_element_type=F32)


def _dot_nt(a, b):
    return lax.dot_general(a, b, (((1,), (1,)), ((), ())), preferred_element_type=F32)


def _dot_tn(a, b):
    return lax.dot_general(a, b, (((0,), (0,)), ((), ())), preferred_element_type=F32)


def _dot_f32(a, b):
    return jnp.dot(a, b, preferred_element_type=F32, precision=lax.Precision.HIGHEST)


def _seg_sum(sq, segm):
    hi = sq.astype(BF16)
    lo = (sq - hi.astype(F32)).astype(BF16)
    return _dot(hi, segm) + _dot(lo, segm)


def _sigmoid(x):
    return 1.0 / (1.0 + jnp.exp(-x))


def _softplus(x):
    return jnp.maximum(x, 0.0) + jnp.log1p(jnp.exp(-jnp.abs(x)))


def _log_sigmoid(x):
    return -_softplus(-x)


def _lane_cumsum(x, seg):
    lane = lax.broadcasted_iota(jnp.int32, x.shape, 1)
    pos = lane & (seg - 1)
    s = 1
    while s < seg:
        x = x + jnp.where(pos >= s, pltpu.roll(x, s, 1), 0.0)
        s *= 2
    return x


def _pack_halves(y):
    w = y.shape[1] // 2
    lo = lax.bitcast_convert_type(y[:, :w].astype(BF16).astype(F32), jnp.uint32) >> 16
    hi = lax.bitcast_convert_type(y[:, w:].astype(BF16).astype(F32), jnp.uint32)
    return (hi & jnp.uint32(0xFFFF0000)) | lo


def _unpack_halves(p):
    lo = lax.bitcast_convert_type(p << 16, F32)
    hi = lax.bitcast_convert_type(p & jnp.uint32(0xFFFF0000), F32)
    return lo, hi


def _seg_matrix(n):
    i = jnp.arange(n) // HEAD_DIM
    return (i[:, None] == i[None, :]).astype(BF16)


_W_ROWS = 128


def _regroup_w_in(win_ref, wbf_ref):
    def body(r, carry):
        rows = pl.ds(pl.multiple_of(r * _W_ROWS, _W_ROWS), _W_ROWS)
        for i, g in enumerate(_WIDE):
            wbf_ref[rows, _WIDE_OFF[i]:_WIDE_OFF[i + 1]] = (
                win_ref[0, rows, _OFFS[g]:_OFFS[g + 1]].astype(BF16))
        small = _WIDE_OFF[-1]
        wbf_ref[rows, small:small + LANES] = jnp.zeros((_W_ROWS, LANES), BF16)
        for pos, g in ((_SM_F, 3), (_SM_A, 6), (_SM_B, 7), (_SM_A1, 12)):
            wbf_ref[rows, small + pos:small + pos + _SIZES[g]] = (
                win_ref[0, rows, _OFFS[g]:_OFFS[g + 1]].astype(BF16))
        return carry

    lax.fori_loop(0, win_ref.shape[1] // _W_ROWS, body, 0)


def _proj_kernel(x_ref, gain_ref, win_ref, segm_ref, qg_ref, kg_ref,
                 fb_ref, alog_ref, dtb_ref, wa2_ref, ba_ref, selk_ref, selc_ref,
                 fq_o, ka_o, fv_o, gqkv_o, ggate_o, lq_o, lk_o, lv_o, lr_o, la_o,
                 g_o, beta_o, carry_ref, wbf_ref):
    tm = x_ref.shape[1]

    @pl.when((pl.program_id(0) == 0) & (pl.program_id(1) == 0))
    def _():
        _regroup_w_in(win_ref, wbf_ref)

    x = x_ref[0]
    ms = jnp.mean(x * x, axis=-1, keepdims=True)
    hb = (x * lax.rsqrt(ms + RMS_EPS) * gain_ref[...]).astype(BF16)

    def wide(i):
        return _dot(hb, wbf_ref[:, _WIDE_OFF[i]:_WIDE_OFF[i + 1]])

    segm = segm_ref[...]
    q = wide(0)
    q = q * lax.rsqrt(_seg_sum(q * q, segm) * (1.0 / HEAD_DIM) + RMS_EPS) * qg_ref[...]
    fq_o[0] = q.astype(BF16).T
    k = wide(1)
    k = k * lax.rsqrt(_seg_sum(k * k, segm) * (1.0 / HEAD_DIM) + RMS_EPS) * kg_ref[...]
    kb16 = k.astype(BF16)
    fv_o[0] = wide(2).astype(BF16).T
    gqkv_o[0] = wide(3).astype(BF16)
    ggate_o[0] = wide(4).astype(BF16)
    lq_o[0] = wide(5).astype(BF16)
    lk_o[0] = wide(6).astype(BF16)
    lv_o[0] = wide(7).astype(BF16)
    lr_o[0] = wide(8).astype(BF16)

    sm = _dot(hb, wbf_ref[:, _WIDE_OFF[-1]:_WIDE_OFF[-1] + LANES])
    la_logit = _dot_f32(sm, wa2_ref[...]) + ba_ref[...]
    la_o[0] = _log_sigmoid(la_logit) * (1.0 / GLA_TAU)

    smt = sm.T
    log_f = _log_sigmoid(smt[_SM_F:_SM_F + 8] + fb_ref[...])

    @pl.when(pl.program_id(1) == 0)
    def _():
        carry_ref[...] = jnp.zeros_like(carry_ref)

    cum = _lane_cumsum(log_f, tm) + carry_ref[:, 0:1]
    carry_ref[...] = jnp.broadcast_to(cum[:, tm - 1:tm], carry_ref.shape)
    pieces = []
    rem = cum * _LOG2E
    for _ in range(_C_PIECES):
        piece = rem.astype(BF16).astype(F32)
        pieces.append(piece)
        rem = rem - piece
    pieces.append(jnp.zeros_like(cum))
    pt = jnp.concatenate(pieces, axis=0).T.astype(BF16)
    for h in range(H_FOX):
        pair = kb16[:, (h // 2) * LANES:(h // 2 + 1) * LANES]
        ka_o[0, h] = (_dot(pair, selk_ref[h % 2]) + _dot(pt, selc_ref[h])).astype(BF16)
    g_o[0] = -jnp.exp(alog_ref[...]) * _softplus(smt[_SM_A:_SM_A + 8] + dtb_ref[...])
    beta_o[0] = _sigmoid(smt[_SM_B:_SM_B + 8])


def _proj_call(x, gain, w_in_all, layer, segm, qg, kg, fb, alog, dtb, wa2, ba, tm):
    b, s, d = x.shape
    tm = min(tm, s)
    const = lambda shape: pl.BlockSpec(shape, lambda i, j: (0,) * len(shape),
                                       pipeline_mode=pl.Buffered(1))
    tok = lambda w: pl.BlockSpec((1, tm, w), lambda i, j: (i, j, 0))
    row = pl.BlockSpec((1, 8, tm), lambda i, j: (i, 0, j))
    widths = [_SIZES[g] for g in _WIDE]
    out_shape = ([jax.ShapeDtypeStruct((b, s, w), BF16) for w in widths]
                 + [jax.ShapeDtypeStruct((b, s, LANES), F32)]
                 + [jax.ShapeDtypeStruct((b, 8, s), F32)] * 2)
    out_specs = [tok(w) for w in widths] + [tok(LANES)] + [row] * 2
    for i in (0, 2):
        out_shape[i] = jax.ShapeDtypeStruct((b, widths[i], s), BF16)
        out_specs[i] = pl.BlockSpec((1, widths[i], tm), lambda i, j: (i, 0, j))
    out_shape[1] = jax.ShapeDtypeStruct((b, H_FOX, s, LANES), BF16)
    out_specs[1] = pl.BlockSpec((1, H_FOX, tm, LANES), lambda i, j: (i, 0, j, 0))
    lane = jnp.arange(LANES)
    selk = jnp.stack([(lane[:, None] == hh * HEAD_DIM + lane[None, :])
                      & (lane[None, :] < HEAD_DIM) for hh in range(2)]).astype(BF16)
    src = jnp.arange(4 * SUBLANES)
    piece, head = src // SUBLANES, src % SUBLANES
    selc = jnp.stack([(head[:, None] == h) & (piece[:, None] < _C_PIECES)
                      & (lane[None, :] == HEAD_DIM + piece[:, None])
                      for h in range(H_FOX)]).astype(BF16)
    return pl.pallas_call(
        _proj_kernel,
        grid=(b, s // tm),
        in_specs=[tok(d), const((1, d)),
                  pl.BlockSpec((1,) + w_in_all.shape[1:], lambda i, j: (layer, 0, 0),
                               pipeline_mode=pl.Buffered(1)),
                  const(segm.shape),
                  const(qg.shape), const(kg.shape), const((8, 1)), const((8, 1)), const((8, 1)),
                  const(wa2.shape), const(ba.shape), const(selk.shape), const(selc.shape)],
        out_specs=out_specs,
        out_shape=out_shape,
        scratch_shapes=[pltpu.VMEM((8, LANES), F32),
                        pltpu.VMEM((d, _WIDE_OFF[-1] + LANES), BF16)],
        compiler_params=pltpu.CompilerParams(
            dimension_semantics=("arbitrary", "arbitrary"), vmem_limit_bytes=VMEM_LIMIT),
        name="proj",
    )(x, gain, w_in_all, segm, qg, kg, fb, alog, dtb, wa2, ba, selk, selc)


_ACC_ROWS = HEAD_DIM + 16


def _fox_kernel(qt_ref, k_ref, vt_ref, aug_ref, ones_ref, o_ref, m_ref, acc_ref):
    tq = qt_ref.shape[2]
    qi = pl.program_id(1)
    heads = range(H_FOX)
    aug = aug_ref[...]
    qts = [jnp.concatenate([qt_ref[0, h * HEAD_DIM:(h + 1) * HEAD_DIM, :], aug], axis=0)
           for h in heads]
    m_ref[...] = jnp.full_like(m_ref, -jnp.inf)
    acc_ref[...] = jnp.zeros_like(acc_ref)
    krow = lax.broadcasted_iota(jnp.int32, (tq, tq), 0)
    qcol = lax.broadcasted_iota(jnp.int32, (tq, tq), 1)
    ones = ones_ref[...]

    def step(blocks, masked):
        k0s = [pl.multiple_of(j * tq, tq) for j in blocks]
        ss = [[_dot(k_ref[0, h, pl.ds(k0, tq), :], qts[h]) for k0 in k0s]
              for h in heads]
        if masked:
            ss = [[jnp.where(krow <= qcol, s, -jnp.inf) for s in sh] for sh in ss]
        m_prev = [m_ref[h] for h in heads]
        m_new = []
        for h in heads:
            m = m_prev[h]
            for s in ss[h]:
                m = jnp.maximum(m, jnp.max(s, axis=0, keepdims=True))
            m_new.append(m)
        ps = [[jnp.exp2(s - m_new[h]).astype(BF16) for s in ss[h]] for h in heads]
        for h in heads:
            acc = jnp.exp2(m_prev[h] - m_new[h]) * acc_ref[h]
            for k0, p in zip(k0s, ps[h]):
                vta = jnp.concatenate(
                    [vt_ref[0, h * HEAD_DIM:(h + 1) * HEAD_DIM, pl.ds(k0, tq)], ones], axis=0)
                acc = acc + _dot(vta, p)
            acc_ref[h] = acc
            m_ref[h] = m_new[h]

    step([qi], True)

    def body(j2, carry):
        step([2 * j2, 2 * j2 + 1], False)
        return carry

    lax.fori_loop(0, qi // 2, body, 0)

    @pl.when(qi % 2 == 1)
    def _():
        step([qi - 1], False)
    outs = []
    for h in heads:
        acc = acc_ref[h]
        outs.append(acc[0:HEAD_DIM] / acc[HEAD_DIM:HEAD_DIM + 1])
    o_ref[0] = jnp.concatenate(outs, axis=0).T.astype(o_ref.dtype)


def _fox_call(qt, ka, vt, tq):
    b, _, s = qt.shape
    tq = min(tq, s)
    aug = jnp.broadcast_to(jnp.where(jnp.arange(HEAD_DIM)[:, None] < _C_PIECES, -1.0, 0.0),
                           (HEAD_DIM, tq)).astype(BF16)
    ones = jnp.broadcast_to(jnp.where(jnp.arange(16)[:, None] == 0, 1.0, 0.0), (16, tq)).astype(BF16)
    return pl.pallas_call(
        _fox_kernel,
        grid=(b, s // tq),
        in_specs=[pl.BlockSpec((1, FOX_DIM, tq), lambda i, j: (i, 0, j)),
                  pl.BlockSpec((1, H_FOX, s, LANES), lambda i, j: (i, 0, 0, 0)),
                  pl.BlockSpec((1, FOX_DIM, s), lambda i, j: (i, 0, 0)),
                  pl.BlockSpec(aug.shape, lambda i, j: (0, 0)),
                  pl.BlockSpec(ones.shape, lambda i, j: (0, 0))],
        out_specs=pl.BlockSpec((1, tq, FOX_DIM), lambda i, j: (i, j, 0)),
        out_shape=jax.ShapeDtypeStruct((b, s, FOX_DIM), BF16),
        scratch_shapes=[pltpu.VMEM((H_FOX, 1, tq), F32),
                        pltpu.VMEM((H_FOX, _ACC_ROWS, tq), F32)],
        compiler_params=pltpu.CompilerParams(
            dimension_semantics=("arbitrary", "arbitrary"),
            vmem_limit_bytes=VMEM_LIMIT),
        name="fox",
    )(qt, ka, vt, aug, ones)


def _bmm(a, b):
    return jnp.einsum('gmk,gkn->gmn', a.astype(BF16), b.astype(BF16),
                      preferred_element_type=F32)


def _unit_lower_inverse(a_strict, eye):
    n = -a_strict
    t = eye + n
    size = 1
    while 2 * size < CHUNK:
        n = _bmm(n, n)
        t = t + _bmm(t, n)
        size *= 2
    return t


def _gdn_kernel(qkv_ref, convw_ref, g_ref, beta_ref, segm_ref, o_ref, state_ref, xext_ref):
    tc = qkv_ref.shape[1]
    pad = SUBLANES

    @pl.when(pl.program_id(1) == 0)
    def _():
        state_ref[...] = jnp.zeros_like(state_ref)
        xext_ref[0:pad, :] = jnp.zeros((pad, xext_ref.shape[1]), F32)

    x = qkv_ref[0].astype(F32)
    xext_ref[pad:pad + tc, :] = x
    y = convw_ref[0:1, :] * xext_ref[pad - 3:pad - 3 + tc, :]
    for i in range(1, CONV_K):
        y = y + convw_ref[i:i + 1, :] * xext_ref[pad - 3 + i:pad - 3 + i + tc, :]
    xext_ref[0:pad, :] = x[tc - pad:tc, :]
    y = y * _sigmoid(y)

    segm = segm_ref[...]
    q = y[:, 0:GDN_DIM]
    k = y[:, GDN_DIM:2 * GDN_DIM]
    v = y[:, 2 * GDN_DIM:3 * GDN_DIM]
    q = q * lax.rsqrt(_seg_sum(q * q, segm) + RMS_EPS) * (HEAD_DIM ** -0.5)
    k = k * lax.rsqrt(_seg_sum(k * k, segm) + RMS_EPS)

    kt = k.T

    dec_row = _lane_cumsum(g_ref[0], CHUNK)
    dec_col = dec_row.T
    beta_col = beta_ref[0].T

    nc = tc // CHUNK
    index = [(c, h) for c in range(nc) for h in range(H_GDN)]

    def split(a):
        return jnp.stack([a[c * CHUNK:(c + 1) * CHUNK, h * HEAD_DIM:(h + 1) * HEAD_DIM]
                          for c, h in index])

    def split_col(a):
        return jnp.stack([a[c * CHUNK:(c + 1) * CHUNK, h:h + 1] for c, h in index])

    q3, k3, v3 = split(q), split(k), split(v)
    kt3 = jnp.stack([kt[h * HEAD_DIM:(h + 1) * HEAD_DIM, c * CHUNK:(c + 1) * CHUNK]
                     for c, h in index])
    dcol = split_col(dec_col)
    bcol = split_col(beta_col)
    drow = jnp.stack([dec_row[h:h + 1, c * CHUNK:(c + 1) * CHUNK] for c, h in index])
    dlast = dcol[:, CHUNK - 1:CHUNK, :]

    ri = lax.broadcasted_iota(jnp.int32, (1, CHUNK, CHUNK), 1)
    ci = lax.broadcasted_iota(jnp.int32, (1, CHUNK, CHUNK), 2)
    causal = ci <= ri
    eye = (ci == ri).astype(F32)
    gamma = jnp.where(causal, jnp.exp(jnp.where(causal, dcol - drow, 0.0)), 0.0)
    edec = jnp.exp(dcol)
    kb = k3 * bcol
    a = jnp.where(ci < ri, _bmm(kb, kt3) * gamma, 0.0)
    t = _unit_lower_inverse(a, eye)
    u3 = _bmm(t, v3 * bcol)
    w3 = _bmm(t, kb * edec)
    intra = _bmm(q3, kt3) * gamma
    qd = q3 * edec
    kdt = kt3 * jnp.exp(dlast - drow)
    elast = jnp.exp(dlast)

    s = state_ref[...]
    for c in range(nc):
        sl = slice(c * H_GDN, (c + 1) * H_GDN)
        v_new = u3[sl] - _bmm(w3[sl], s)
        o = _bmm(qd[sl], s) + _bmm(intra[sl], v_new)
        s = s * elast[sl] + _bmm(kdt[sl], v_new)
        o_ref[0, c * CHUNK:(c + 1) * CHUNK, :] = jnp.concatenate(
            [o[h] for h in range(H_GDN)], axis=1).astype(o_ref.dtype)
    state_ref[...] = s


def _gdn_call(qkv, convw, g, beta, segm, tc):
    b, s, w = qkv.shape
    tc = min(tc, s)
    return pl.pallas_call(
        _gdn_kernel,
        grid=(b, s // tc),
        in_specs=[pl.BlockSpec((1, tc, w), lambda i, j: (i, j, 0)),
                  pl.BlockSpec(convw.shape, lambda i, j: (0, 0)),
                  pl.BlockSpec((1, 8, tc), lambda i, j: (i, 0, j)),
                  pl.BlockSpec((1, 8, tc), lambda i, j: (i, 0, j)),
                  pl.BlockSpec(segm.shape, lambda i, j: (0, 0))],
        out_specs=pl.BlockSpec((1, tc, GDN_DIM), lambda i, j: (i, j, 0)),
        out_shape=jax.ShapeDtypeStruct((b, s, GDN_DIM), BF16),
        scratch_shapes=[pltpu.VMEM((H_GDN, HEAD_DIM, HEAD_DIM), F32),
                        pltpu.VMEM((tc + SUBLANES, w), F32)],
        compiler_params=pltpu.CompilerParams(
            dimension_semantics=("arbitrary", "arbitrary"), vmem_limit_bytes=VMEM_LIMIT),
        name="gdn",
    )(qkv, convw, g, beta, segm)


def _gla_kernel(q_ref, k_ref, v_ref, la_ref, tri_ref, mh_ref, mt_ref, o_ref,
                st_ref, kf_ref, bq_ref, vf_ref):
    tc = q_ref.shape[1]

    @pl.when(pl.program_id(1) == 0)
    def _():
        st_ref[...] = jnp.zeros_like(st_ref)

    row = lax.broadcasted_iota(jnp.int32, (CHUNK, GLA_K_DIM), 0)
    mh = mh_ref[...]
    for c in range(tc // CHUNK):
        r0, r1 = c * CHUNK, (c + 1) * CHUNK
        bq = _dot_f32(tri_ref[...], la_ref[0, r0:r1, :])
        q = q_ref[0, r0:r1, :].astype(F32) * (GLA_DK ** -0.5)
        k = k_ref[0, r0:r1, :].astype(F32)
        v = v_ref[0, r0:r1, :].astype(F32)
        kf_ref[...] = k
        bq_ref[...] = bq
        vf_ref[...] = v
        st = st_ref[...]
        o_inter = _dot_nt((q * jnp.exp(bq)).astype(BF16), st.astype(BF16))

        group_out = []
        for g0 in range(0, CHUNK, SUBLANES):
            n = CHUNK - g0
            qg, bg, rg = q[g0:, :], bq[g0:, :], row[g0:, :]
            es = []
            for j in range(g0, g0 + SUBLANES):
                kj = kf_ref[j:j + 1, :]
                bj = bq_ref[j:j + 1, :]
                es.append(jnp.where(rg >= j, jnp.exp(bg - bj), 0.0) * (qg * kj))
            p = _dot(jnp.concatenate(es, axis=0).astype(BF16), mh)
            acc = p[0:n] * vf_ref[g0:g0 + 1, :]
            for jj in range(1, SUBLANES):
                acc = acc + p[jj * n:(jj + 1) * n] * vf_ref[g0 + jj:g0 + jj + 1, :]
            group_out.append(acc)
        pieces = []
        for r in range(0, CHUNK, SUBLANES):
            piece = o_inter[r:r + SUBLANES]
            for gi, g0 in enumerate(range(0, r + SUBLANES, SUBLANES)):
                piece = piece + group_out[gi][r - g0:r - g0 + SUBLANES]
            pieces.append(piece)
        o_ref[0, r0:r1, :] = jnp.concatenate(pieces, axis=0).astype(o_ref.dtype)
        blast = bq[CHUNK - 1:CHUNK, :]
        kd = k * jnp.exp(blast - bq)
        upd = _dot_tn(v.astype(BF16), kd.astype(BF16))
        st_ref[...] = (st * jnp.exp(blast) + upd) * mt_ref[...]


def _gla_call(q, k, v, la, tc):
    b, s, _ = q.shape
    tc = min(tc, s)
    tri = (jnp.arange(CHUNK)[:, None] >= jnp.arange(CHUNK)[None, :]).astype(F32)
    hk = jnp.arange(GLA_K_DIM) // GLA_DK
    hv = jnp.arange(GLA_V_DIM) // HEAD_DIM
    mh = (hk[:, None] == hv[None, :]).astype(BF16)
    mt = (hv[:, None] == hk[None, :]).astype(F32)
    tok = lambda w: pl.BlockSpec((1, tc, w), lambda i, j: (i, j, 0))
    const = lambda a: pl.BlockSpec(a.shape, lambda i, j: (0, 0))
    return pl.pallas_call(
        _gla_kernel,
        grid=(b, s // tc),
        in_specs=[tok(GLA_K_DIM), tok(GLA_K_DIM), tok(GLA_V_DIM), tok(GLA_K_DIM),
                  const(tri), const(mh), const(mt)],
        out_specs=tok(GLA_V_DIM),
        out_shape=jax.ShapeDtypeStruct((b, s, GLA_V_DIM), BF16),
        scratch_shapes=[pltpu.VMEM((GLA_V_DIM, GLA_K_DIM), F32),
                        pltpu.VMEM((CHUNK, GLA_K_DIM), F32),
                        pltpu.VMEM((CHUNK, GLA_K_DIM), F32),
                        pltpu.VMEM((CHUNK, GLA_V_DIM), F32)],
        compiler_params=pltpu.CompilerParams(
            dimension_semantics=("arbitrary", "arbitrary"), vmem_limit_bytes=VMEM_LIMIT),
        name="gla",
    )(q, k, v, la, tri, mh, mt)


def _head_norm(o_ref, segm, gain):
    o = o_ref[...].astype(F32)
    return o * lax.rsqrt(_seg_sum(o * o, segm) * (1.0 / HEAD_DIM) + RMS_EPS) * gain


def _silu(x):
    return x * _sigmoid(x)


def _out_kernel(x_ref, ofox_ref, ogdn_ref, ggate_ref, ogla_ref, lr_ref, wout_ref, segm_ref,
                gf_ref, gg_ref, gl_ref, fgain_ref, wr_ref, br_ref, tri_ref,
                x1_o, h2_o, meta_o, cnt_o, carry_ref):
    tm = x_ref.shape[0]
    segm = segm_ref[...]
    a = _head_norm(ofox_ref, segm, gf_ref[...])
    bb = _head_norm(ogdn_ref, segm, gg_ref[...]) * _silu(ggate_ref[...].astype(F32))
    cc = (_head_norm(ogla_ref, segm[:GLA_V_DIM, :GLA_V_DIM], gl_ref[...])
          * _silu(lr_ref[...].astype(F32)))
    y = (x_ref[...]
         + _dot(a.astype(BF16), wout_ref[0:FOX_DIM, :].astype(BF16))
         + _dot(bb.astype(BF16), wout_ref[FOX_DIM:FOX_DIM + GDN_DIM, :].astype(BF16))
         + _dot(cc.astype(BF16), wout_ref[FOX_DIM + GDN_DIM:, :].astype(BF16)))
    x1_o[...] = y
    ms = jnp.mean(y * y, axis=-1, keepdims=True)
    h2 = y * lax.rsqrt(ms + RMS_EPS) * fgain_ref[...]
    h2_o[...] = _pack_halves(h2)

    wr = wr_ref[...]
    wr_hi = wr.astype(BF16)
    wr_lo = (wr - wr_hi.astype(F32)).astype(BF16)
    h2_hi = h2.astype(BF16)
    h2_lo = (h2 - h2_hi.astype(F32)).astype(BF16)
    logits = (_dot(h2_hi, wr_hi) + (_dot(h2_hi, wr_lo) + _dot(h2_lo, wr_hi))
              + br_ref[...])
    lane = lax.broadcasted_iota(jnp.int32, logits.shape, 1).astype(F32)
    big = float(4 * LANES)
    ninf = -jnp.inf
    gl = jnp.where(lane < _R_GROUP + N_GROUPS, logits, ninf)
    gmax = jnp.max(gl, axis=1, keepdims=True)
    group_p = 1.0 / jnp.sum(jnp.exp(gl - gmax), axis=1, keepdims=True)
    gidx = jnp.min(jnp.where(gl == gmax, lane, big), axis=1, keepdims=True)
    elane = lane - _R_EXPERT
    group_of_lane = jnp.floor(elane * (1.0 / EXPERTS_PER_GROUP))
    in_group = (elane >= 0) & (elane < N_EXPERTS) & (group_of_lane == gidx)
    el = jnp.where(in_group, logits, ninf)
    m1 = jnp.max(el, axis=1, keepdims=True)
    i1 = jnp.min(jnp.where(el == m1, lane, big), axis=1, keepdims=True)
    el2 = jnp.where(lane == i1, ninf, el)
    m2 = jnp.max(el2, axis=1, keepdims=True)
    i2 = jnp.min(jnp.where(el2 == m2, lane, big), axis=1, keepdims=True)
    t = jnp.exp(m2 - m1)
    g1 = group_p / (1.0 + t)
    g2 = group_p * t / (1.0 + t)

    @pl.when(pl.program_id(0) == 0)
    def _():
        carry_ref[...] = jnp.zeros_like(carry_ref)

    sel = jnp.where((lane == i1) | (lane == i2), 1.0, 0.0)
    carry = carry_ref[0:1, :]
    rank = _dot(tri_ref[...], sel.astype(BF16)) + carry
    rank1 = jnp.sum(jnp.where(lane == i1, rank, 0.0), axis=1, keepdims=True)
    rank2 = jnp.sum(jnp.where(lane == i2, rank, 0.0), axis=1, keepdims=True)
    new_carry = carry + jnp.sum(sel, axis=0, keepdims=True)
    carry_ref[...] = jnp.broadcast_to(new_carry, carry_ref.shape)
    cnt_o[...] = jnp.broadcast_to(new_carry, cnt_o.shape)
    cols = [i1 - _R_EXPERT, i2 - _R_EXPERT, rank1, rank2, g1, g2]
    meta = jnp.zeros(logits.shape, F32)
    for idx, col in enumerate(cols):
        meta = jnp.where(lane == idx, col, meta)
    meta_o[...] = meta


def _out_call(x, ofox, ogdn, ggate, ogla, lr, wout, segm, gf, gg, gl, fgain, wr, br, tm):
    n, d = x.shape
    tm = min(tm, n)
    tri = (jnp.arange(tm)[:, None] > jnp.arange(tm)[None, :]).astype(BF16)
    tok = lambda w: pl.BlockSpec((tm, w), lambda i: (i, 0))
    const = lambda a: pl.BlockSpec(a.shape, lambda i: (0,) * a.ndim,
                                   pipeline_mode=pl.Buffered(1))
    return pl.pallas_call(
        _out_kernel,
        grid=(n // tm,),
        in_specs=[tok(d), tok(FOX_DIM), tok(GDN_DIM), tok(GDN_DIM), tok(GLA_V_DIM), tok(GLA_V_DIM),
                  const(wout), const(segm), const(gf), const(gg), const(gl), const(fgain),
                  const(wr), const(br), const(tri)],
        out_specs=[tok(d), tok(d // 2), tok(LANES), pl.BlockSpec((8, LANES), lambda i: (0, 0))],
        out_shape=[jax.ShapeDtypeStruct((n, d), F32), jax.ShapeDtypeStruct((n, d // 2), jnp.uint32),
                   jax.ShapeDtypeStruct((n, LANES), F32), jax.ShapeDtypeStruct((8, LANES), F32)],
        scratch_shapes=[pltpu.VMEM((8, LANES), F32)],
        compiler_params=pltpu.CompilerParams(
            dimension_semantics=("arbitrary",), vmem_limit_bytes=VMEM_LIMIT),
        name="out_router",
    )(x, ofox, ogdn, ggate, ogla, lr, wout, segm, gf, gg, gl, fgain, wr, br, tri)


_DMA_UNROLL = 8


def _dispatch_kernel(dest_ref, h_ref, xb_in_ref, xb_ref, sem):
    del xb_in_ref
    td = h_ref.shape[0]

    def row_copy(t, d):
        return pltpu.make_async_copy(h_ref.at[pl.ds(t, 1), :], xb_ref.at[pl.ds(d, 1), :], sem)

    def issue(t, carry):
        for kk in range(TOP_K):
            row_copy(t, dest_ref[0, 0, TOP_K * t + kk]).start()
        return carry

    lax.fori_loop(0, td, issue, 0, unroll=_DMA_UNROLL)

    def drain(t, carry):
        for kk in range(TOP_K):
            row_copy(0, 0).wait()
        return carry

    lax.fori_loop(0, td, drain, 0, unroll=_DMA_UNROLL)


def _dispatch_call(dest, h2, n_rows, td):
    n, d = h2.shape
    td = min(td, n)
    dest3 = dest.reshape(n // td, 1, TOP_K * td)
    xb0 = jnp.zeros((n_rows, d), h2.dtype)
    return pl.pallas_call(
        _dispatch_kernel,
        grid=(n // td,),
        in_specs=[pl.BlockSpec((1, 1, TOP_K * td), lambda i: (i, 0, 0), memory_space=pltpu.SMEM),
                  pl.BlockSpec((td, d), lambda i: (i, 0)),
                  pl.BlockSpec(memory_space=pl.ANY)],
        out_specs=pl.BlockSpec(memory_space=pl.ANY),
        out_shape=jax.ShapeDtypeStruct((n_rows, d), h2.dtype),
        scratch_shapes=[pltpu.SemaphoreType.DMA(())],
        input_output_aliases={2: 0},
        compiler_params=pltpu.CompilerParams(
            dimension_semantics=("arbitrary",), has_side_effects=True),
        name="dispatch",
    )(dest3, h2, xb0)


def _expert_kernel(be_ref, nu_ref, x_ref, wg_ref, wu_ref, wd_ref, y_ref, wgb_ref, wub_ref, wdb_ref):
    i = pl.program_id(0)
    used = i < nu_ref[0]

    @pl.when(used & ((i == 0) | (be_ref[i] != be_ref[jnp.maximum(i - 1, 0)])))
    def _():
        wgb_ref[...] = wg_ref[0].astype(BF16)
        wub_ref[...] = wu_ref[0].astype(BF16)
        wdb_ref[...] = wd_ref[0].astype(BF16)

    @pl.when(used)
    def _():
        half = wgb_ref.shape[0] // 2
        lo, hi = _unpack_halves(x_ref[...])
        lo, hi = lo.astype(BF16), hi.astype(BF16)
        a = _dot(lo, wgb_ref[0:half, :]) + _dot(hi, wgb_ref[half:, :])
        u = _dot(lo, wub_ref[0:half, :]) + _dot(hi, wub_ref[half:, :])
        hmid = (_silu(a) * u).astype(BF16)
        y_ref[...] = _pack_halves(_dot(hmid, wdb_ref[...]))

    @pl.when(jnp.logical_not(used))
    def _():
        y_ref[...] = jnp.zeros_like(y_ref)


def _expert_call(block_e, n_used, xb, wg, wu, wd, tmb):
    n_rows, dh = xb.shape
    d = 2 * dh
    de = wg.shape[-1]
    n_blocks = n_rows // tmb

    def xmap(i, be, nu):
        return (jnp.minimum(i, jnp.maximum(nu[0] - 1, 0)), 0)

    wmap = lambda i, be, nu: (be[i], 0, 0)
    return pl.pallas_call(
        _expert_kernel,
        grid_spec=pltpu.PrefetchScalarGridSpec(
            num_scalar_prefetch=2,
            grid=(n_blocks,),
            in_specs=[pl.BlockSpec((tmb, dh), xmap),
                      pl.BlockSpec((1, d, de), wmap),
                      pl.BlockSpec((1, d, de), wmap),
                      pl.BlockSpec((1, de, d), wmap)],
            out_specs=pl.BlockSpec((tmb, dh), lambda i, be, nu: (i, 0)),
            scratch_shapes=[pltpu.VMEM((d, de), BF16), pltpu.VMEM((d, de), BF16),
                            pltpu.VMEM((de, d), BF16)],
        ),
        out_shape=jax.ShapeDtypeStruct((n_rows, dh), jnp.uint32),
        compiler_params=pltpu.CompilerParams(
            dimension_semantics=("arbitrary",), vmem_limit_bytes=VMEM_LIMIT),
        name="experts",
    )(block_e, n_used, xb, wg, wu, wd)


def _combine_kernel(dest_ref, x1_ref, meta_ref, yb_ref, o_ref, buf_ref, sem):
    td = x1_ref.shape[0]

    def row_copy(t, kk, d):
        return pltpu.make_async_copy(yb_ref.at[pl.ds(d, 1), :], buf_ref.at[kk, pl.ds(t, 1), :], sem)

    def issue(t, carry):
        for kk in range(TOP_K):
            row_copy(t, kk, dest_ref[0, 0, TOP_K * t + kk]).start()
        return carry

    lax.fori_loop(0, td, issue, 0, unroll=_DMA_UNROLL)

    def drain(t, carry):
        for kk in range(TOP_K):
            row_copy(0, kk, 0).wait()
        return carry

    lax.fori_loop(0, td, drain, 0, unroll=_DMA_UNROLL)
    meta = meta_ref[...]
    g1, g2 = meta[:, 4:5], meta[:, 5:6]
    half = buf_ref.shape[2]
    lo1, hi1 = _unpack_halves(buf_ref[0])
    lo2, hi2 = _unpack_halves(buf_ref[1])
    o_ref[:, 0:half] = x1_ref[:, 0:half] + g1 * lo1 + g2 * lo2
    o_ref[:, half:] = x1_ref[:, half:] + g1 * hi1 + g2 * hi2


def _combine_call(dest, x1, meta, yb, td):
    n, d = x1.shape
    td = min(td, n)
    dest3 = dest.reshape(n // td, 1, TOP_K * td)
    return pl.pallas_call(
        _combine_kernel,
        grid=(n // td,),
        in_specs=[pl.BlockSpec((1, 1, TOP_K * td), lambda i: (i, 0, 0), memory_space=pltpu.SMEM),
                  pl.BlockSpec((td, d), lambda i: (i, 0)),
                  pl.BlockSpec((td, LANES), lambda i: (i, 0)),
                  pl.BlockSpec(memory_space=pl.ANY)],
        out_specs=pl.BlockSpec((td, d), lambda i: (i, 0)),
        out_shape=jax.ShapeDtypeStruct((n, d), F32),
        scratch_shapes=[pltpu.VMEM((TOP_K,) + (td, yb.shape[1]), yb.dtype),
                        pltpu.SemaphoreType.DMA(())],
        compiler_params=pltpu.CompilerParams(
            dimension_semantics=("arbitrary",), vmem_limit_bytes=VMEM_LIMIT),
        name="combine",
    )(dest3, x1, meta, yb)


TM_PROJ = 512
TQ_FOX = 256
TC_GDN = 256
TC_GLA = 256
TM_OUT = 512
TD_MOE = 256
TMB_EXPERT = 256


def _place(width, parts):
    cols, at = [], 0
    for pos, blk in parts:
        if pos > at:
            cols.append(jnp.zeros((blk.shape[0], pos - at), blk.dtype))
        cols.append(blk)
        at = pos + blk.shape[1]
    if width > at:
        cols.append(jnp.zeros((parts[0][1].shape[0], width - at), parts[0][1].dtype))
    return jnp.concatenate(cols, axis=1)


def _pad8(v):
    return jnp.zeros((8,), F32).at[:v.shape[0]].set(v.astype(F32)).reshape(8, 1)


def _token_mixer(x, attn_norm, w_in_all, layer, fox_q_norm, fox_k_norm, fox_f_bias,
                 gdn_conv, gdn_a_log, gdn_dt_bias, gla_w_a2, gla_b_a):
    b, s, d = x.shape
    wa2 = jnp.pad(gla_w_a2, ((_SM_A1, LANES - _SM_A1 - GLA_RANK), (0, 0)))
    segm = _seg_matrix(FOX_DIM)
    qg = (jnp.tile(fox_q_norm, H_FOX) * (HEAD_DIM ** -0.5 * _LOG2E)).reshape(1, FOX_DIM)
    kg = jnp.tile(fox_k_norm, H_FOX).reshape(1, FOX_DIM)
    outs = _proj_call(x, attn_norm.reshape(1, d), w_in_all, layer, segm, qg, kg,
                      _pad8(fox_f_bias), _pad8(gdn_a_log), _pad8(gdn_dt_bias),
                      wa2, gla_b_a.reshape(1, GLA_K_DIM), TM_PROJ)
    fq, fka, fv, gqkv, ggate, lq, lk, lv, lr, la, g, beta = outs
    o_fox = _fox_call(fq, fka, fv, TQ_FOX)
    o_gdn = _gdn_call(gqkv, gdn_conv.astype(F32), g, beta, segm, TC_GDN)
    o_gla = _gla_call(lq, lk, lv, la, TC_GLA)
    return o_fox, o_gdn, ggate, o_gla, lr


def _layer(x, p, layer, w_in_all, experts):
    b, s, d = x.shape
    n = b * s
    o_fox, o_gdn, ggate, o_gla, lr = _token_mixer(
        x, p['attn_norm'], w_in_all, layer, p['fox_q_norm'], p['fox_k_norm'], p['fox_f_bias'],
        p['gdn_conv'], p['gdn_a_log'], p['gdn_dt_bias'], p['gla_w_a2'], p['gla_b_a'])
    wr = _place(LANES, [(_R_GROUP, p['w_router_group']), (_R_EXPERT, p['w_router_expert'])])
    br = _place(LANES, [(_R_GROUP, p['b_router_group'].reshape(1, -1)),
                        (_R_EXPERT, p['b_router_expert'].reshape(1, -1))])
    flat = lambda a: a.reshape(n, a.shape[-1])
    x1, h2, meta, cnt = _out_call(
        flat(x), flat(o_fox), flat(o_gdn), flat(ggate), flat(o_gla), flat(lr),
        p['w_out'], _seg_matrix(FOX_DIM),
        jnp.tile(p['fox_o_norm'], H_FOX).reshape(1, FOX_DIM),
        jnp.tile(p['gdn_o_norm'], H_GDN).reshape(1, GDN_DIM),
        jnp.tile(p['gla_o_norm'], H_GLA).reshape(1, GLA_V_DIM),
        p['ffn_norm'].reshape(1, d), wr, br, TM_OUT)

    tmb = TMB_EXPERT
    counts = cnt[0, _R_EXPERT:_R_EXPERT + N_EXPERTS].astype(jnp.int32)
    padded = (counts + tmb - 1) // tmb * tmb
    pends = jnp.cumsum(padded)
    pstarts = pends - padded
    eid = meta[:, 0:TOP_K].astype(jnp.int32)
    rank = meta[:, TOP_K:2 * TOP_K].astype(jnp.int32)
    expert_ids = jnp.arange(N_EXPERTS, dtype=jnp.int32)
    start_of = jnp.sum(jnp.where(eid[..., None] == expert_ids, pstarts, 0), axis=-1)
    dest = (start_of + rank).reshape(-1)
    n_blocks = -(-(n * TOP_K) // tmb) + N_EXPERTS
    block_start = jnp.arange(n_blocks, dtype=jnp.int32) * tmb
    block_e = jnp.minimum(jnp.sum(pends[None, :] <= block_start[:, None], axis=1),
                          N_EXPERTS - 1).astype(jnp.int32)
    n_used = (pends[-1:] // tmb).astype(jnp.int32)

    xb = _dispatch_call(dest, h2, n_blocks * tmb, TD_MOE)
    yb = _expert_call(block_e + layer * N_EXPERTS, n_used, xb, *experts, tmb)
    x2 = _combine_call(dest, x1, meta, yb, TD_MOE)
    return x2.reshape(b, s, d)


_PARAM_NAMES = ['attn_norm', 'w_in', 'fox_q_norm', 'fox_k_norm', 'fox_f_bias', 'fox_o_norm',
                'gdn_conv', 'gdn_a_log', 'gdn_dt_bias', 'gdn_o_norm',
                'gla_w_a2', 'gla_b_a', 'gla_o_norm', 'w_out',
                'ffn_norm', 'w_router_group', 'b_router_group', 'w_router_expert',
                'b_router_expert', 'w_expert_gate', 'w_expert_up', 'w_expert_down']


def kernel(x, attn_norm, w_in, fox_q_norm, fox_k_norm, fox_f_bias, fox_o_norm, gdn_conv, gdn_a_log, gdn_dt_bias, gdn_o_norm, gla_w_a2, gla_b_a, gla_o_norm, w_out, ffn_norm, w_router_group, b_router_group, w_router_expert, b_router_expert, w_expert_gate, w_expert_up, w_expert_down):
    params = dict(zip(_PARAM_NAMES, (
        attn_norm, w_in, fox_q_norm, fox_k_norm, fox_f_bias, fox_o_norm, gdn_conv, gdn_a_log,
        gdn_dt_bias, gdn_o_norm, gla_w_a2, gla_b_a, gla_o_norm, w_out, ffn_norm,
        w_router_group, b_router_group, w_router_expert, b_router_expert,
        w_expert_gate, w_expert_up, w_expert_down)))
    experts = tuple(params.pop(name).reshape((-1,) + params_shape[2:])
                    for name, params_shape in (('w_expert_gate', w_expert_gate.shape),
                                               ('w_expert_up', w_expert_up.shape),
                                               ('w_expert_down', w_expert_down.shape)))
    del params['w_in']
    for layer in range(attn_norm.shape[0]):
        x = _layer(x, {name: val[layer] for name, val in params.items()}, layer, w_in, experts)
    return x
```

```python
import functools

import jax
import jax.numpy as jnp
from jax import lax
from jax.experimental import pallas as pl
from jax.experimental.pallas import tpu as pltpu

F32 = jnp.float32
BF16 = jnp.bfloat16

HEAD_DIM = 64
H_FOX = 6
H_GDN = 6
H_GLA = 4
FOX_DIM = H_FOX * HEAD_DIM
GDN_DIM = H_GDN * HEAD_DIM
GLA_DK = 32
GLA_K_DIM = H_GLA * GLA_DK
GLA_V_DIM = H_GLA * HEAD_DIM
GLA_RANK = 16
GLA_TAU = 16.0
CONV_K = 4
CHUNK = 64
N_GROUPS = 4
EXPERTS_PER_GROUP = 8
N_EXPERTS = N_GROUPS * EXPERTS_PER_GROUP
TOP_K = 2
RMS_EPS = 1e-6

_LOG2E = 1.4426950408889634
_C_PIECES = 3

LANES = 128
SUBLANES = 8
VMEM_LIMIT = 56 * 1024 * 1024

_SIZES = [FOX_DIM, FOX_DIM, FOX_DIM, H_FOX, 3 * GDN_DIM, GDN_DIM, H_GDN, H_GDN,
          GLA_K_DIM, GLA_K_DIM, GLA_V_DIM, GLA_V_DIM, GLA_RANK]
_OFFS = [sum(_SIZES[:i]) for i in range(len(_SIZES) + 1)]
_WIDE = [0, 1, 2, 4, 5, 8, 9, 10, 11]
_WIDE_OFF = [0]
for _g in _WIDE:
    _WIDE_OFF.append(_WIDE_OFF[-1] + _SIZES[_g])
_SM_F, _SM_A, _SM_B, _SM_A1 = 0, 8, 16, 32
_R_GROUP, _R_EXPERT = 0, 32


def _dot(a, b):
    return jnp.dot(a, b, preferred_element_type=F32)


def _dot_nt(a, b):
    return lax.dot_general(a, b, (((1,), (1,)), ((), ())), preferred_element_type=F32)


def _dot_tn(a, b):
    return lax.dot_general(a, b, (((0,), (0,)), ((), ())), preferred_element_type=F32)


def _split(a):
    hi = a.astype(BF16)
    return hi, (a - hi.astype(F32)).astype(BF16)


def _dot_split(a, b):
    a_hi, a_lo = _split(a)
    b_hi, b_lo = _split(b)
    return _dot(a_hi, b_hi) + (_dot(a_hi, b_lo) + _dot(a_lo, b_hi))


def _dot_exact_lhs(a01, b):
    a16 = a01.astype(BF16)
    b_hi = b.astype(BF16)
    rem = b - b_hi.astype(F32)
    b_mid = rem.astype(BF16)
    b_lo = (rem - b_mid.astype(F32)).astype(BF16)
    return _dot(a16, b_hi) + (_dot(a16, b_mid) + _dot(a16, b_lo))


def _seg_sum(sq, segm):
    return _dot(sq.astype(BF16), segm)


def _sigmoid(x):
    return 1.0 / (1.0 + jnp.exp(-x))


def _softplus(x):
    return jnp.maximum(x, 0.0) + jnp.log1p(jnp.exp(-jnp.abs(x)))


def _log_sigmoid(x):
    return -_softplus(-x)


def _lane_cumsum(x, seg):
    lane = lax.broadcasted_iota(jnp.int32, x.shape, 1)
    pos = lane & (seg - 1)
    s = 1
    while s < seg:
        x = x + jnp.where(pos >= s, pltpu.roll(x, s, 1), 0.0)
        s *= 2
    return x


def _seg_matrix(n):
    i = jnp.arange(n) // HEAD_DIM
    return (i[:, None] == i[None, :]).astype(BF16)


_W_ROWS = 128


def _regroup_w_in(win_ref, wbf_ref):
    def body(r, carry):
        rows = pl.ds(pl.multiple_of(r * _W_ROWS, _W_ROWS), _W_ROWS)
        for i, g in enumerate(_WIDE):
            wbf_ref[rows, _WIDE_OFF[i]:_WIDE_OFF[i + 1]] = (
                win_ref[0, rows, _OFFS[g]:_OFFS[g + 1]].astype(BF16))
        small = _WIDE_OFF[-1]
        wbf_ref[rows, small:small + LANES] = jnp.zeros((_W_ROWS, LANES), BF16)
        for pos, g in ((_SM_F, 3), (_SM_A, 6), (_SM_B, 7), (_SM_A1, 12)):
            wbf_ref[rows, small + pos:small + pos + _SIZES[g]] = (
                win_ref[0, rows, _OFFS[g]:_OFFS[g + 1]].astype(BF16))
        return carry

    lax.fori_loop(0, win_ref.shape[1] // _W_ROWS, body, 0)


def _proj_kernel(x_ref, gain_ref, win_ref, segm_ref, qg_ref, kg_ref,
                 fb_ref, alog_ref, dtb_ref, wa2_ref, ba_ref,
                 fq_o, ka_o, fv_o, gqkv_o, ggate_o, lq_o, lk_o, lv_o, lr_o, la_o,
                 g_o, beta_o, carry_ref, wbf_ref):
    tm = x_ref.shape[1]

    @pl.when((pl.program_id(0) == 0) & (pl.program_id(1) == 0))
    def _():
        _regroup_w_in(win_ref, wbf_ref)

    x = x_ref[0]
    ms = jnp.mean(x * x, axis=-1, keepdims=True)
    hb = (x * lax.rsqrt(ms + RMS_EPS) * gain_ref[...]).astype(BF16)

    def wide(i):
        return _dot(hb, wbf_ref[:, _WIDE_OFF[i]:_WIDE_OFF[i + 1]])

    segm = segm_ref[...]
    q = wide(0)
    q = q * lax.rsqrt(_seg_sum(q * q, segm) * (1.0 / HEAD_DIM) + RMS_EPS) * qg_ref[...]
    fq_o[0] = q.astype(BF16).T
    k = wide(1)
    k = k * lax.rsqrt(_seg_sum(k * k, segm) * (1.0 / HEAD_DIM) + RMS_EPS) * kg_ref[...]
    fv_o[0] = wide(2).astype(BF16).T
    gqkv_o[0] = wide(3).astype(BF16)
    ggate_o[0] = wide(4).astype(BF16)
    lq_o[0] = wide(5).astype(BF16)
    lk_o[0] = wide(6).astype(BF16)
    lv_o[0] = wide(7).astype(BF16)
    lr_o[0] = wide(8).astype(BF16)

    sm = _dot(hb, wbf_ref[:, _WIDE_OFF[-1]:_WIDE_OFF[-1] + LANES])
    la_logit = _dot_split(sm, wa2_ref[...]) + ba_ref[...]
    la_o[0] = _log_sigmoid(la_logit) * (1.0 / GLA_TAU)

    smt = sm.T
    log_f = _log_sigmoid(smt[_SM_F:_SM_F + 8] + fb_ref[...])

    @pl.when(pl.program_id(1) == 0)
    def _():
        carry_ref[...] = jnp.zeros_like(carry_ref)

    cum = _lane_cumsum(log_f, tm) + carry_ref[:, 0:1]
    carry_ref[...] = jnp.broadcast_to(cum[:, tm - 1:tm], carry_ref.shape)
    pieces = []
    rem = cum * _LOG2E
    for _ in range(_C_PIECES):
        piece = rem.astype(BF16).astype(F32)
        pieces.append(piece)
        rem = rem - piece
    zero_row = jnp.zeros((1, tm), F32)
    rows = []
    for h in range(H_FOX):
        rows += [piece[h:h + 1] for piece in pieces] + [zero_row]
    rows.append(jnp.zeros((LANES - 4 * H_FOX, tm), F32))
    ptw = jnp.concatenate(rows, axis=0).T
    lane = lax.broadcasted_iota(jnp.int32, (tm, LANES), 1)
    for h in range(H_FOX):
        kp = k[:, (h // 2) * LANES:(h // 2 + 1) * LANES]
        if h % 2:
            kp = pltpu.roll(kp, HEAD_DIM, 1)
        cp = pltpu.roll(ptw, HEAD_DIM - 4 * h, 1)
        ka = jnp.where(lane < HEAD_DIM, kp, jnp.where(lane < HEAD_DIM + 4, cp, 0.0))
        ka_o[0, h] = ka.astype(BF16)
    g_o[0] = -jnp.exp(alog_ref[...]) * _softplus(smt[_SM_A:_SM_A + 8] + dtb_ref[...])
    beta_o[0] = _sigmoid(smt[_SM_B:_SM_B + 8])


def _proj_call(x, gain, w_in_all, layer, segm, qg, kg, fb, alog, dtb, wa2, ba, tm):
    b, s, d = x.shape
    tm = min(tm, s)
    const = lambda shape: pl.BlockSpec(shape, lambda i, j: (0,) * len(shape),
                                       pipeline_mode=pl.Buffered(1))
    tok = lambda w: pl.BlockSpec((1, tm, w), lambda i, j: (i, j, 0))
    row = pl.BlockSpec((1, 8, tm), lambda i, j: (i, 0, j))
    widths = [_SIZES[g] for g in _WIDE]
    out_shape = ([jax.ShapeDtypeStruct((b, s, w), BF16) for w in widths]
                 + [jax.ShapeDtypeStruct((b, s, LANES), F32)]
                 + [jax.ShapeDtypeStruct((b, 8, s), F32)] * 2)
    out_specs = [tok(w) for w in widths] + [tok(LANES)] + [row] * 2
    for i in (0, 2):
        out_shape[i] = jax.ShapeDtypeStruct((b, widths[i], s), BF16)
        out_specs[i] = pl.BlockSpec((1, widths[i], tm), lambda i, j: (i, 0, j))
    out_shape[1] = jax.ShapeDtypeStruct((b, H_FOX, s, LANES), BF16)
    out_specs[1] = pl.BlockSpec((1, H_FOX, tm, LANES), lambda i, j: (i, 0, j, 0))
    return pl.pallas_call(
        _proj_kernel,
        grid=(b, s // tm),
        in_specs=[tok(d), const((1, d)),
                  pl.BlockSpec((1,) + w_in_all.shape[1:], lambda i, j: (layer, 0, 0),
                               pipeline_mode=pl.Buffered(1)),
                  const(segm.shape),
                  const(qg.shape), const(kg.shape), const((8, 1)), const((8, 1)), const((8, 1)),
                  const(wa2.shape), const(ba.shape)],
        out_specs=out_specs,
        out_shape=out_shape,
        scratch_shapes=[pltpu.VMEM((8, LANES), F32),
                        pltpu.VMEM((d, _WIDE_OFF[-1] + LANES), BF16)],
        compiler_params=pltpu.CompilerParams(
            dimension_semantics=("arbitrary", "arbitrary"), vmem_limit_bytes=VMEM_LIMIT),
        name="proj",
    )(x, gain, w_in_all, segm, qg, kg, fb, alog, dtb, wa2, ba)


_ACC_ROWS = HEAD_DIM + 16


def _fox_kernel(qt_ref, k_ref, vt_ref, aug_ref, ones_ref, o_ref, m_ref, acc_ref):
    tq = qt_ref.shape[2]
    qi = pl.program_id(1)
    heads = range(H_FOX)
    aug = aug_ref[...]
    qts = [jnp.concatenate([qt_ref[0, h * HEAD_DIM:(h + 1) * HEAD_DIM, :], aug], axis=0)
           for h in heads]
    m_ref[...] = jnp.full_like(m_ref, -jnp.inf)
    acc_ref[...] = jnp.zeros_like(acc_ref)
    krow = lax.broadcasted_iota(jnp.int32, (tq, tq), 0)
    qcol = lax.broadcasted_iota(jnp.int32, (tq, tq), 1)
    ones = ones_ref[...]

    def step(blocks, masked):
        k0s = [pl.multiple_of(j * tq, tq) for j in blocks]
        ss = [[_dot(k_ref[0, h, pl.ds(k0, tq), :], qts[h]) for k0 in k0s]
              for h in heads]
        if masked:
            ss = [[jnp.where(krow <= qcol, s, -jnp.inf) for s in sh] for sh in ss]
        m_prev = [m_ref[h] for h in heads]
        m_new = []
        for h in heads:
            m = m_prev[h]
            for s in ss[h]:
                m = jnp.maximum(m, jnp.max(s, axis=0, keepdims=True))
            m_new.append(m)
        ps = [[jnp.exp2(s - m_new[h]).astype(BF16) for s in ss[h]] for h in heads]
        for h in heads:
            acc = jnp.exp2(m_prev[h] - m_new[h]) * acc_ref[h]
            for k0, p in zip(k0s, ps[h]):
                vta = jnp.concatenate(
                    [vt_ref[0, h * HEAD_DIM:(h + 1) * HEAD_DIM, pl.ds(k0, tq)], ones], axis=0)
                acc = acc + _dot(vta, p)
            acc_ref[h] = acc
            m_ref[h] = m_new[h]

    step([qi], True)

    def body(j2, carry):
        step([2 * j2, 2 * j2 + 1], False)
        return carry

    lax.fori_loop(0, qi // 2, body, 0)

    @pl.when(qi % 2 == 1)
    def _():
        step([qi - 1], False)
    outs = []
    for h in heads:
        acc = acc_ref[h]
        outs.append(acc[0:HEAD_DIM] / acc[HEAD_DIM:HEAD_DIM + 1])
    o_ref[0] = jnp.concatenate(outs, axis=0).T.astype(o_ref.dtype)


def _fox_call(qt, ka, vt, tq):
    b, _, s = qt.shape
    tq = min(tq, s)
    aug = jnp.broadcast_to(jnp.where(jnp.arange(HEAD_DIM)[:, None] < _C_PIECES, -1.0, 0.0),
                           (HEAD_DIM, tq)).astype(BF16)
    ones = jnp.broadcast_to(jnp.where(jnp.arange(16)[:, None] == 0, 1.0, 0.0), (16, tq)).astype(BF16)
    return pl.pallas_call(
        _fox_kernel,
        grid=(b, s // tq),
        in_specs=[pl.BlockSpec((1, FOX_DIM, tq), lambda i, j: (i, 0, j)),
                  pl.BlockSpec((1, H_FOX, s, LANES), lambda i, j: (i, 0, 0, 0)),
                  pl.BlockSpec((1, FOX_DIM, s), lambda i, j: (i, 0, 0)),
                  pl.BlockSpec(aug.shape, lambda i, j: (0, 0)),
                  pl.BlockSpec(ones.shape, lambda i, j: (0, 0))],
        out_specs=pl.BlockSpec((1, tq, FOX_DIM), lambda i, j: (i, j, 0)),
        out_shape=jax.ShapeDtypeStruct((b, s, FOX_DIM), BF16),
        scratch_shapes=[pltpu.VMEM((H_FOX, 1, tq), F32),
                        pltpu.VMEM((H_FOX, _ACC_ROWS, tq), F32)],
        compiler_params=pltpu.CompilerParams(
            dimension_semantics=("arbitrary", "arbitrary"),
            vmem_limit_bytes=VMEM_LIMIT),
        name="fox",
    )(qt, ka, vt, aug, ones)


def _bmm(a, b):
    return jnp.einsum('gmk,gkn->gmn', a.astype(BF16), b.astype(BF16),
                      preferred_element_type=F32)


def _pair_diag(x):
    xb = x.astype(BF16)
    low = lax.broadcasted_iota(jnp.int32, (1,) + xb.shape[1:], 2) < HEAD_DIM
    zero = jnp.zeros_like(xb)
    return jnp.concatenate([jnp.where(low, xb, zero), jnp.where(low, zero, xb)], axis=1)


def _unit_lower_inverse(a_strict, eye):
    n = -a_strict
    t = eye + n
    nd = _pair_diag(n)
    size = 1
    while 2 * size < CHUNK:
        n = _bmm(n, nd)
        nd = _pair_diag(n)
        t = t + _bmm(t, nd)
        size *= 2
    return t


def _gdn_kernel(qkv_ref, convw_ref, g_ref, beta_ref, segm_ref, o_ref, state_ref, xext_ref):
    tc = qkv_ref.shape[1]
    pad = SUBLANES

    @pl.when(pl.program_id(1) == 0)
    def _():
        state_ref[...] = jnp.zeros_like(state_ref)
        xext_ref[0:pad, :] = jnp.zeros((pad, xext_ref.shape[1]), F32)

    x = qkv_ref[0].astype(F32)
    xext_ref[pad:pad + tc, :] = x
    y = convw_ref[0:1, :] * xext_ref[pad - 3:pad - 3 + tc, :]
    for i in range(1, CONV_K):
        y = y + convw_ref[i:i + 1, :] * xext_ref[pad - 3 + i:pad - 3 + i + tc, :]
    xext_ref[0:pad, :] = x[tc - pad:tc, :]
    y = y * _sigmoid(y)

    segm = segm_ref[...]
    q = y[:, 0:GDN_DIM]
    k = y[:, GDN_DIM:2 * GDN_DIM]
    v = y[:, 2 * GDN_DIM:3 * GDN_DIM]
    q = q * lax.rsqrt(_seg_sum(q * q, segm) + RMS_EPS) * (HEAD_DIM ** -0.5)
    k = k * lax.rsqrt(_seg_sum(k * k, segm) + RMS_EPS)

    kt = k.T

    dec_row = _lane_cumsum(g_ref[0], CHUNK)
    dec_col = dec_row.T
    beta_col = beta_ref[0].T

    head_of_lane = lax.broadcasted_iota(jnp.int32, (tc, GDN_DIM), 1) // HEAD_DIM
    dexp = jnp.zeros((tc, GDN_DIM), F32)
    bexp = jnp.zeros((tc, GDN_DIM), F32)
    for h in range(H_GDN):
        dexp = jnp.where(head_of_lane == h, dec_col[:, h:h + 1], dexp)
        bexp = jnp.where(head_of_lane == h, beta_col[:, h:h + 1], bexp)
    edec = jnp.exp(dexp)
    kb = k * bexp
    vb = v * bexp
    kbe = kb * edec
    qd = q * edec

    nc = tc // CHUNK
    npair = H_GDN // 2
    index = [(c, p) for c in range(nc) for p in range(npair)]

    def split(a):
        return jnp.stack([a[c * CHUNK:(c + 1) * CHUNK, p * LANES:(p + 1) * LANES]
                          for c, p in index])

    q3, kb3, vb3, kbe3, qd3, dcol = (split(a) for a in (q, kb, vb, kbe, qd, dexp))
    drow = jnp.stack([jnp.concatenate(
        [dec_row[2 * p + hh:2 * p + hh + 1, c * CHUNK:(c + 1) * CHUNK] for hh in range(2)], axis=1)
        for c, p in index])
    dlast = dcol[:, CHUNK - 1:CHUNK, :]

    low = lax.broadcasted_iota(jnp.int32, (CHUNK, LANES), 1) < HEAD_DIM
    kdt_list, kd_list = [], []
    for c, p in index:
        tile = kt[p * LANES:(p + 1) * LANES, (c // 2) * LANES:(c // 2 + 1) * LANES]
        swapped = pltpu.roll(tile, HEAD_DIM, 1)
        top = (tile if c % 2 == 0 else swapped)[0:HEAD_DIM]
        bot = (swapped if c % 2 == 0 else tile)[HEAD_DIM:]
        kdt_list.append(jnp.where(low, top, bot))
        kd_list.append(jnp.concatenate([jnp.where(low, top, 0.0), jnp.where(low, 0.0, bot)],
                                       axis=0).astype(BF16))
    kt3 = jnp.stack(kdt_list)
    ktd = jnp.stack(kd_list)

    ri = lax.broadcasted_iota(jnp.int32, (1, CHUNK, LANES), 1)
    cj = lax.broadcasted_iota(jnp.int32, (1, CHUNK, LANES), 2) & (HEAD_DIM - 1)
    causal = cj <= ri
    eye = (cj == ri).astype(F32)
    gamma = jnp.where(causal, jnp.exp(jnp.where(causal, dcol - drow, 0.0)), 0.0)
    a = jnp.where(cj < ri, _bmm(kb3, ktd) * gamma, 0.0)
    t = _unit_lower_inverse(a, eye)
    u3 = _bmm(t, _pair_diag(vb3))
    w3 = _bmm(t, _pair_diag(kbe3))
    intra = _bmm(q3, ktd) * gamma
    kdt = kt3 * jnp.exp(dlast - drow)
    elast = jnp.exp(dlast)

    s = state_ref[...]
    for c in range(nc):
        sl = slice(c * npair, (c + 1) * npair)
        sd = _pair_diag(s)
        v_new = u3[sl] - _bmm(w3[sl], sd)
        vd = _pair_diag(v_new)
        o = _bmm(qd3[sl], sd) + _bmm(intra[sl], vd)
        s = s * elast[sl] + _bmm(kdt[sl], vd)
        o_ref[0, c * CHUNK:(c + 1) * CHUNK, :] = jnp.concatenate(
            [o[p] for p in range(npair)], axis=1).astype(o_ref.dtype)
    state_ref[...] = s


def _gdn_call(qkv, convw, g, beta, segm, tc):
    b, s, w = qkv.shape
    tc = min(tc, s)
    return pl.pallas_call(
        _gdn_kernel,
        grid=(b, s // tc),
        in_specs=[pl.BlockSpec((1, tc, w), lambda i, j: (i, j, 0)),
                  pl.BlockSpec(convw.shape, lambda i, j: (0, 0)),
                  pl.BlockSpec((1, 8, tc), lambda i, j: (i, 0, j)),
                  pl.BlockSpec((1, 8, tc), lambda i, j: (i, 0, j)),
                  pl.BlockSpec(segm.shape, lambda i, j: (0, 0))],
        out_specs=pl.BlockSpec((1, tc, GDN_DIM), lambda i, j: (i, j, 0)),
        out_shape=jax.ShapeDtypeStruct((b, s, GDN_DIM), BF16),
        scratch_shapes=[pltpu.VMEM((H_GDN // 2, HEAD_DIM, LANES), F32),
                        pltpu.VMEM((tc + SUBLANES, w), F32)],
        compiler_params=pltpu.CompilerParams(
            dimension_semantics=("arbitrary", "arbitrary"), vmem_limit_bytes=VMEM_LIMIT),
        name="gdn",
    )(qkv, convw, g, beta, segm)


def _gla_kernel(q_ref, k_ref, v_ref, la_ref, tri_ref, mh_ref, mt_ref, o_ref,
                st_ref, kf_ref, bq_ref, vf_ref):
    tc = q_ref.shape[1]

    @pl.when(pl.program_id(1) == 0)
    def _():
        st_ref[...] = jnp.zeros_like(st_ref)

    row = lax.broadcasted_iota(jnp.int32, (CHUNK, GLA_K_DIM), 0)
    mh = mh_ref[...]
    for c in range(tc // CHUNK):
        r0, r1 = c * CHUNK, (c + 1) * CHUNK
        bq = _dot_exact_lhs(tri_ref[...], la_ref[0, r0:r1, :])
        q = q_ref[0, r0:r1, :].astype(F32) * (GLA_DK ** -0.5)
        k = k_ref[0, r0:r1, :].astype(F32)
        v = v_ref[0, r0:r1, :].astype(F32)
        kf_ref[...] = k
        bq_ref[...] = bq
        vf_ref[...] = v
        st = st_ref[...]
        o_inter = _dot_nt((q * jnp.exp(bq)).astype(BF16), st.astype(BF16))

        group_out = []
        for g0 in range(0, CHUNK, SUBLANES):
            n = CHUNK - g0
            qg, bg, rg = q[g0:, :], bq[g0:, :], row[g0:, :]
            es = []
            for j in range(g0, g0 + SUBLANES):
                kj = kf_ref[j:j + 1, :]
                bj = bq_ref[j:j + 1, :]
                es.append(jnp.where(rg >= j, jnp.exp(bg - bj), 0.0) * (qg * kj))
            p = _dot(jnp.concatenate(es, axis=0).astype(BF16), mh)
            acc = p[0:n] * vf_ref[g0:g0 + 1, :]
            for jj in range(1, SUBLANES):
                acc = acc + p[jj * n:(jj + 1) * n] * vf_ref[g0 + jj:g0 + jj + 1, :]
            group_out.append(acc)
        pieces = []
        for r in range(0, CHUNK, SUBLANES):
            piece = o_inter[r:r + SUBLANES]
            for gi, g0 in enumerate(range(0, r + SUBLANES, SUBLANES)):
                piece = piece + group_out[gi][r - g0:r - g0 + SUBLANES]
            pieces.append(piece)
        o_ref[0, r0:r1, :] = jnp.concatenate(pieces, axis=0).astype(o_ref.dtype)
        blast = bq[CHUNK - 1:CHUNK, :]
        kd = k * jnp.exp(blast - bq)
        upd = _dot_tn(v.astype(BF16), kd.astype(BF16))
        st_ref[...] = (st * jnp.exp(blast) + upd) * mt_ref[...]


def _gla_call(q, k, v, la, tc):
    b, s, _ = q.shape
    tc = min(tc, s)
    tri = (jnp.arange(CHUNK)[:, None] >= jnp.arange(CHUNK)[None, :]).astype(F32)
    hk = jnp.arange(GLA_K_DIM) // GLA_DK
    hv = jnp.arange(GLA_V_DIM) // HEAD_DIM
    mh = (hk[:, None] == hv[None, :]).astype(BF16)
    mt = (hv[:, None] == hk[None, :]).astype(F32)
    tok = lambda w: pl.BlockSpec((1, tc, w), lambda i, j: (i, j, 0))
    const = lambda a: pl.BlockSpec(a.shape, lambda i, j: (0, 0))
    return pl.pallas_call(
        _gla_kernel,
        grid=(b, s // tc),
        in_specs=[tok(GLA_K_DIM), tok(GLA_K_DIM), tok(GLA_V_DIM), tok(GLA_K_DIM),
                  const(tri), const(mh), const(mt)],
        out_specs=tok(GLA_V_DIM),
        out_shape=jax.ShapeDtypeStruct((b, s, GLA_V_DIM), BF16),
        scratch_shapes=[pltpu.VMEM((GLA_V_DIM, GLA_K_DIM), F32),
                        pltpu.VMEM((CHUNK, GLA_K_DIM), F32),
                        pltpu.VMEM((CHUNK, GLA_K_DIM), F32),
                        pltpu.VMEM((CHUNK, GLA_V_DIM), F32)],
        compiler_params=pltpu.CompilerParams(
            dimension_semantics=("arbitrary", "arbitrary"), vmem_limit_bytes=VMEM_LIMIT),
        name="gla",
    )(q, k, v, la, tri, mh, mt)


def _head_norm(o_ref, segm, gain):
    o = o_ref[...].astype(F32)
    return o * lax.rsqrt(_seg_sum(o * o, segm) * (1.0 / HEAD_DIM) + RMS_EPS) * gain


def _silu(x):
    return x * _sigmoid(x)


def _out_kernel(x_ref, ofox_ref, ogdn_ref, ggate_ref, ogla_ref, lr_ref, wout_ref, segm_ref,
                gf_ref, gg_ref, gl_ref, fgain_ref, wr_ref, br_ref, tri_ref,
                x1_o, h2_o, meta_o, cnt_o, carry_ref):
    tm = x_ref.shape[0]
    segm = segm_ref[...]
    a = _head_norm(ofox_ref, segm, gf_ref[...])
    bb = _head_norm(ogdn_ref, segm, gg_ref[...]) * _silu(ggate_ref[...].astype(F32))
    cc = (_head_norm(ogla_ref, segm[:GLA_V_DIM, :GLA_V_DIM], gl_ref[...])
          * _silu(lr_ref[...].astype(F32)))
    y = (x_ref[...]
         + _dot(a.astype(BF16), wout_ref[0:FOX_DIM, :].astype(BF16))
         + _dot(bb.astype(BF16), wout_ref[FOX_DIM:FOX_DIM + GDN_DIM, :].astype(BF16))
         + _dot(cc.astype(BF16), wout_ref[FOX_DIM + GDN_DIM:, :].astype(BF16)))
    x1_o[...] = y
    ms = jnp.mean(y * y, axis=-1, keepdims=True)
    h2 = y * lax.rsqrt(ms + RMS_EPS) * fgain_ref[...]
    h2_o[...] = h2

    wr_hi, wr_lo = _split(wr_ref[...])
    h2_hi, h2_lo = _split(h2)
    hi_terms = _dot(h2_hi, jnp.concatenate([wr_hi, wr_lo], axis=1))
    logits = (hi_terms[:, :LANES] + (hi_terms[:, LANES:] + _dot(h2_lo, wr_hi))
              + br_ref[...])
    lane = lax.broadcasted_iota(jnp.int32, logits.shape, 1).astype(F32)
    big = float(4 * LANES)
    ninf = -jnp.inf
    gl = jnp.where(lane < _R_GROUP + N_GROUPS, logits, ninf)
    gmax = jnp.max(gl, axis=1, keepdims=True)
    group_p = 1.0 / jnp.sum(jnp.exp(gl - gmax), axis=1, keepdims=True)
    gidx = jnp.min(jnp.where(gl == gmax, lane, big), axis=1, keepdims=True)
    elane = lane - _R_EXPERT
    group_of_lane = jnp.floor(elane * (1.0 / EXPERTS_PER_GROUP))
    in_group = (elane >= 0) & (elane < N_EXPERTS) & (group_of_lane == gidx)
    el = jnp.where(in_group, logits, ninf)
    m1 = jnp.max(el, axis=1, keepdims=True)
    i1 = jnp.min(jnp.where(el == m1, lane, big), axis=1, keepdims=True)
    el2 = jnp.where(lane == i1, ninf, el)
    m2 = jnp.max(el2, axis=1, keepdims=True)
    i2 = jnp.min(jnp.where(el2 == m2, lane, big), axis=1, keepdims=True)
    t = jnp.exp(m2 - m1)
    g1 = group_p / (1.0 + t)
    g2 = group_p * t / (1.0 + t)

    @pl.when(pl.program_id(0) == 0)
    def _():
        carry_ref[...] = jnp.zeros_like(carry_ref)

    sel = jnp.where((lane == i1) | (lane == i2), 1.0, 0.0)
    carry = carry_ref[0:1, :]
    rank = _dot(tri_ref[...], sel.astype(BF16)) + carry
    rank1 = jnp.sum(jnp.where(lane == i1, rank, 0.0), axis=1, keepdims=True)
    rank2 = jnp.sum(jnp.where(lane == i2, rank, 0.0), axis=1, keepdims=True)
    new_carry = carry + jnp.sum(sel, axis=0, keepdims=True)
    carry_ref[...] = jnp.broadcast_to(new_carry, carry_ref.shape)
    cnt_o[...] = jnp.broadcast_to(new_carry, cnt_o.shape)
    cols = [i1 - _R_EXPERT, i2 - _R_EXPERT, rank1, rank2, g1, g2]
    meta = jnp.zeros(logits.shape, F32)
    for idx, col in enumerate(cols):
        meta = jnp.where(lane == idx, col, meta)
    meta_o[...] = meta


def _out_call(x, ofox, ogdn, ggate, ogla, lr, wout, segm, gf, gg, gl, fgain, wr, br, tm):
    n, d = x.shape
    tm = min(tm, n)
    tri = (jnp.arange(tm)[:, None] > jnp.arange(tm)[None, :]).astype(BF16)
    tok = lambda w: pl.BlockSpec((tm, w), lambda i: (i, 0))
    const = lambda a: pl.BlockSpec(a.shape, lambda i: (0,) * a.ndim,
                                   pipeline_mode=pl.Buffered(1))
    return pl.pallas_call(
        _out_kernel,
        grid=(n // tm,),
        in_specs=[tok(d), tok(FOX_DIM), tok(GDN_DIM), tok(GDN_DIM), tok(GLA_V_DIM), tok(GLA_V_DIM),
                  const(wout), const(segm), const(gf), const(gg), const(gl), const(fgain),
                  const(wr), const(br), const(tri)],
        out_specs=[tok(d), tok(d), tok(LANES), pl.BlockSpec((8, LANES), lambda i: (0, 0))],
        out_shape=[jax.ShapeDtypeStruct((n, d), F32), jax.ShapeDtypeStruct((n, d), F32),
                   jax.ShapeDtypeStruct((n, LANES), F32), jax.ShapeDtypeStruct((8, LANES), F32)],
        scratch_shapes=[pltpu.VMEM((8, LANES), F32)],
        compiler_params=pltpu.CompilerParams(
            dimension_semantics=("arbitrary",), vmem_limit_bytes=VMEM_LIMIT),
        name="out_router",
    )(x, ofox, ogdn, ggate, ogla, lr, wout, segm, gf, gg, gl, fgain, wr, br, tri)


_DMA_UNROLL = 8


def _dispatch_kernel(dest_ref, h_ref, xb_in_ref, xb_ref, sem):
    del xb_in_ref
    td = h_ref.shape[0]

    def row_copy(t, d):
        return pltpu.make_async_copy(h_ref.at[pl.ds(t, 1), :], xb_ref.at[pl.ds(d, 1), :], sem)

    def issue(t, carry):
        for kk in range(TOP_K):
            row_copy(t, dest_ref[0, 0, TOP_K * t + kk]).start()
        return carry

    lax.fori_loop(0, td, issue, 0, unroll=_DMA_UNROLL)

    def drain(t, carry):
        for kk in range(TOP_K):
            row_copy(0, 0).wait()
        return carry

    lax.fori_loop(0, td, drain, 0, unroll=_DMA_UNROLL)


def _dispatch_call(dest, h2, n_rows, td):
    n, d = h2.shape
    td = min(td, n)
    dest3 = dest.reshape(n // td, 1, TOP_K * td)
    xb0 = jnp.zeros((n_rows, d), h2.dtype)
    return pl.pallas_call(
        _dispatch_kernel,
        grid=(n // td,),
        in_specs=[pl.BlockSpec((1, 1, TOP_K * td), lambda i: (i, 0, 0), memory_space=pltpu.SMEM),
                  pl.BlockSpec((td, d), lambda i: (i, 0)),
                  pl.BlockSpec(memory_space=pl.ANY)],
        out_specs=pl.BlockSpec(memory_space=pl.ANY),
        out_shape=jax.ShapeDtypeStruct((n_rows, d), h2.dtype),
        scratch_shapes=[pltpu.SemaphoreType.DMA(())],
        input_output_aliases={2: 0},
        compiler_params=pltpu.CompilerParams(
            dimension_semantics=("arbitrary",), has_side_effects=True),
        name="dispatch",
    )(dest3, h2, xb0)


def _expert_kernel(be_ref, nu_ref, x_ref, wg_ref, wu_ref, wd_ref, y_ref, wgb_ref, wub_ref, wdb_ref):
    i = pl.program_id(0)
    used = i < nu_ref[0]

    @pl.when(used & ((i == 0) | (be_ref[i] != be_ref[jnp.maximum(i - 1, 0)])))
    def _():
        wgb_ref[...] = wg_ref[0].astype(BF16)
        wub_ref[...] = wu_ref[0].astype(BF16)
        wdb_ref[...] = wd_ref[0].astype(BF16)

    @pl.when(used)
    def _():
        x = x_ref[...].astype(BF16)
        a = _dot(x, wgb_ref[...])
        u = _dot(x, wub_ref[...])
        hmid = (_silu(a) * u).astype(BF16)
        y_ref[...] = _dot(hmid, wdb_ref[...])

    @pl.when(jnp.logical_not(used))
    def _():
        y_ref[...] = jnp.zeros_like(y_ref)


def _expert_call(block_e, n_used, xb, wg, wu, wd, tmb):
    n_rows, d = xb.shape
    de = wg.shape[-1]
    n_blocks = n_rows // tmb

    def xmap(i, be, nu):
        return (jnp.minimum(i, jnp.maximum(nu[0] - 1, 0)), 0)

    wmap = lambda i, be, nu: (be[i], 0, 0)
    return pl.pallas_call(
        _expert_kernel,
        grid_spec=pltpu.PrefetchScalarGridSpec(
            num_scalar_prefetch=2,
            grid=(n_blocks,),
            in_specs=[pl.BlockSpec((tmb, d), xmap),
                      pl.BlockSpec((1, d, de), wmap),
                      pl.BlockSpec((1, d, de), wmap),
                      pl.BlockSpec((1, de, d), wmap)],
            out_specs=pl.BlockSpec((tmb, d), lambda i, be, nu: (i, 0)),
            scratch_shapes=[pltpu.VMEM((d, de), BF16), pltpu.VMEM((d, de), BF16),
                            pltpu.VMEM((de, d), BF16)],
        ),
        out_shape=jax.ShapeDtypeStruct((n_rows, d), F32),
        compiler_params=pltpu.CompilerParams(
            dimension_semantics=("arbitrary",), vmem_limit_bytes=VMEM_LIMIT),
        name="experts",
    )(block_e, n_used, xb, wg, wu, wd)


def _combine_kernel(dest_ref, x1_ref, meta_ref, yb_ref, o_ref, buf_ref, sem):
    td = x1_ref.shape[0]

    def row_copy(t, kk, d):
        return pltpu.make_async_copy(yb_ref.at[pl.ds(d, 1), :], buf_ref.at[kk, pl.ds(t, 1), :], sem)

    def issue(t, carry):
        for kk in range(TOP_K):
            row_copy(t, kk, dest_ref[0, 0, TOP_K * t + kk]).start()
        return carry

    lax.fori_loop(0, td, issue, 0, unroll=_DMA_UNROLL)

    def drain(t, carry):
        for kk in range(TOP_K):
            row_copy(0, kk, 0).wait()
        return carry

    lax.fori_loop(0, td, drain, 0, unroll=_DMA_UNROLL)
    meta = meta_ref[...]
    o_ref[...] = x1_ref[...] + meta[:, 4:5] * buf_ref[0] + meta[:, 5:6] * buf_ref[1]


def _combine_call(dest, x1, meta, yb, td):
    n, d = x1.shape
    td = min(td, n)
    dest3 = dest.reshape(n // td, 1, TOP_K * td)
    return pl.pallas_call(
        _combine_kernel,
        grid=(n // td,),
        in_specs=[pl.BlockSpec((1, 1, TOP_K * td), lambda i: (i, 0, 0), memory_space=pltpu.SMEM),
                  pl.BlockSpec((td, d), lambda i: (i, 0)),
                  pl.BlockSpec((td, LANES), lambda i: (i, 0)),
                  pl.BlockSpec(memory_space=pl.ANY)],
        out_specs=pl.BlockSpec((td, d), lambda i: (i, 0)),
        out_shape=jax.ShapeDtypeStruct((n, d), F32),
        scratch_shapes=[pltpu.VMEM((TOP_K,) + (td, yb.shape[1]), yb.dtype),
                        pltpu.SemaphoreType.DMA(())],
        compiler_params=pltpu.CompilerParams(
            dimension_semantics=("arbitrary",), vmem_limit_bytes=VMEM_LIMIT),
        name="combine",
    )(dest3, x1, meta, yb)


TM_PROJ = 512
TQ_FOX = 256
TC_GDN = 256
TC_GLA = 256
TM_OUT = 512
TD_MOE = 256
TMB_EXPERT = 256


def _place(width, parts):
    cols, at = [], 0
    for pos, blk in parts:
        if pos > at:
            cols.append(jnp.zeros((blk.shape[0], pos - at), blk.dtype))
        cols.append(blk)
        at = pos + blk.shape[1]
    if width > at:
        cols.append(jnp.zeros((parts[0][1].shape[0], width - at), parts[0][1].dtype))
    return jnp.concatenate(cols, axis=1)


def _pad8(v):
    return jnp.zeros((8,), F32).at[:v.shape[0]].set(v.astype(F32)).reshape(8, 1)


def _token_mixer(x, attn_norm, w_in_all, layer, fox_q_norm, fox_k_norm, fox_f_bias,
                 gdn_conv, gdn_a_log, gdn_dt_bias, gla_w_a2, gla_b_a):
    b, s, d = x.shape
    wa2 = jnp.pad(gla_w_a2, ((_SM_A1, LANES - _SM_A1 - GLA_RANK), (0, 0)))
    segm = _seg_matrix(FOX_DIM)
    qg = (jnp.tile(fox_q_norm, H_FOX) * (HEAD_DIM ** -0.5 * _LOG2E)).reshape(1, FOX_DIM)
    kg = jnp.tile(fox_k_norm, H_FOX).reshape(1, FOX_DIM)
    outs = _proj_call(x, attn_norm.reshape(1, d), w_in_all, layer, segm, qg, kg,
                      _pad8(fox_f_bias), _pad8(gdn_a_log), _pad8(gdn_dt_bias),
                      wa2, gla_b_a.reshape(1, GLA_K_DIM), TM_PROJ)
    fq, fka, fv, gqkv, ggate, lq, lk, lv, lr, la, g, beta = outs
    o_fox = _fox_call(fq, fka, fv, TQ_FOX)
    o_gdn = _gdn_call(gqkv, gdn_conv.astype(F32), g, beta, segm, TC_GDN)
    o_gla = _gla_call(lq, lk, lv, la, TC_GLA)
    return o_fox, o_gdn, ggate, o_gla, lr


def _layer(x, p, layer, w_in_all, experts):
    b, s, d = x.shape
    n = b * s
    o_fox, o_gdn, ggate, o_gla, lr = _token_mixer(
        x, p['attn_norm'], w_in_all, layer, p['fox_q_norm'], p['fox_k_norm'], p['fox_f_bias'],
        p['gdn_conv'], p['gdn_a_log'], p['gdn_dt_bias'], p['gla_w_a2'], p['gla_b_a'])
    wr = _place(LANES, [(_R_GROUP, p['w_router_group']), (_R_EXPERT, p['w_router_expert'])])
    br = _place(LANES, [(_R_GROUP, p['b_router_group'].reshape(1, -1)),
                        (_R_EXPERT, p['b_router_expert'].reshape(1, -1))])
    flat = lambda a: a.reshape(n, a.shape[-1])
    x1, h2, meta, cnt = _out_call(
        flat(x), flat(o_fox), flat(o_gdn), flat(ggate), flat(o_gla), flat(lr),
        p['w_out'], _seg_matrix(FOX_DIM),
        jnp.tile(p['fox_o_norm'], H_FOX).reshape(1, FOX_DIM),
        jnp.tile(p['gdn_o_norm'], H_GDN).reshape(1, GDN_DIM),
        jnp.tile(p['gla_o_norm'], H_GLA).reshape(1, GLA_V_DIM),
        p['ffn_norm'].reshape(1, d), wr, br, TM_OUT)

    tmb = TMB_EXPERT
    counts = cnt[0, _R_EXPERT:_R_EXPERT + N_EXPERTS].astype(jnp.int32)
    padded = (counts + tmb - 1) // tmb * tmb
    pends = jnp.cumsum(padded)
    pstarts = pends - padded
    eid = meta[:, 0:TOP_K].astype(jnp.int32)
    rank = meta[:, TOP_K:2 * TOP_K].astype(jnp.int32)
    expert_ids = jnp.arange(N_EXPERTS, dtype=jnp.int32)
    start_of = jnp.sum(jnp.where(eid[..., None] == expert_ids, pstarts, 0), axis=-1)
    dest = (start_of + rank).reshape(-1)
    n_blocks = -(-(n * TOP_K) // tmb) + N_EXPERTS
    block_start = jnp.arange(n_blocks, dtype=jnp.int32) * tmb
    block_e = jnp.minimum(jnp.sum(pends[None, :] <= block_start[:, None], axis=1),
                          N_EXPERTS - 1).astype(jnp.int32)
    n_used = (pends[-1:] // tmb).astype(jnp.int32)

    xb = _dispatch_call(dest, h2, n_blocks * tmb, TD_MOE)
    yb = _expert_call(block_e + layer * N_EXPERTS, n_used, xb, *experts, tmb)
    x2 = _combine_call(dest, x1, meta, yb, TD_MOE)
    return x2.reshape(b, s, d)


_PARAM_NAMES = ['attn_norm', 'w_in', 'fox_q_norm', 'fox_k_norm', 'fox_f_bias', 'fox_o_norm',
                'gdn_conv', 'gdn_a_log', 'gdn_dt_bias', 'gdn_o_norm',
                'gla_w_a2', 'gla_b_a', 'gla_o_norm', 'w_out',
                'ffn_norm', 'w_router_group', 'b_router_group', 'w_router_expert',
                'b_router_expert', 'w_expert_gate', 'w_expert_up', 'w_expert_down']


def kernel(x, attn_norm, w_in, fox_q_norm, fox_k_norm, fox_f_bias, fox_o_norm, gdn_conv, gdn_a_log, gdn_dt_bias, gdn_o_norm, gla_w_a2, gla_b_a, gla_o_norm, w_out, ffn_norm, w_router_group, b_router_group, w_router_expert, b_router_expert, w_expert_gate, w_expert_up, w_expert_down):
    params = dict(zip(_PARAM_NAMES, (
        attn_norm, w_in, fox_q_norm, fox_k_norm, fox_f_bias, fox_o_norm, gdn_conv, gdn_a_log,
        gdn_dt_bias, gdn_o_norm, gla_w_a2, gla_b_a, gla_o_norm, w_out, ffn_norm,
        w_router_group, b_router_group, w_router_expert, b_router_expert,
        w_expert_gate, w_expert_up, w_expert_down)))
    experts = tuple(params.pop(name).reshape((-1,) + params_shape[2:])
                    for name, params_shape in (('w_expert_gate', w_expert_gate.shape),
                                               ('w_expert_up', w_expert_up.shape),
                                               ('w_expert_down', w_expert_down.shape)))
    del params['w_in']
    for layer in range(attn_norm.shape[0]):
        x = _layer(x, {name: val[layer] for name, val in params.items()}, layer, w_in, experts)
    return x
```

```python
import functools

import jax
import jax.numpy as jnp
from jax import lax
from jax.experimental import pallas as pl
from jax.experimental.pallas import tpu as pltpu

F32 = jnp.float32
BF16 = jnp.bfloat16

HEAD_DIM = 64
H_FOX = 6
H_GDN = 6
H_GLA = 4
FOX_DIM = H_FOX * HEAD_DIM
GDN_DIM = H_GDN * HEAD_DIM
GLA_DK = 32
GLA_K_DIM = H_GLA * GLA_DK
GLA_V_DIM = H_GLA * HEAD_DIM
GLA_RANK = 16
GLA_TAU = 16.0
CONV_K = 4
CHUNK = 64
N_GROUPS = 4
EXPERTS_PER_GROUP = 8
N_EXPERTS = N_GROUPS * EXPERTS_PER_GROUP
TOP_K = 2
RMS_EPS = 1e-6

_LOG2E = 1.4426950408889634
_C_PIECES = 3

LANES = 128
SUBLANES = 8
VMEM_LIMIT = 56 * 1024 * 1024

_SIZES = [FOX_DIM, FOX_DIM, FOX_DIM, H_FOX, 3 * GDN_DIM, GDN_DIM, H_GDN, H_GDN,
          GLA_K_DIM, GLA_K_DIM, GLA_V_DIM, GLA_V_DIM, GLA_RANK]
_OFFS = [sum(_SIZES[:i]) for i in range(len(_SIZES) + 1)]
_WIDE = [0, 1, 2, 4, 5, 8, 9, 10, 11]
_WIDE_OFF = [0]
for _g in _WIDE:
    _WIDE_OFF.append(_WIDE_OFF[-1] + _SIZES[_g])
_SM_F, _SM_A, _SM_B, _SM_A1 = 0, 8, 16, 32
_R_GROUP, _R_EXPERT = 0, 32


def _dot(a, b):
    return jnp.dot(a, b, preferred_element_type=F32)


def _dot_nt(a, b):
    return lax.dot_general(a, b, (((1,), (1,)), ((), ())), preferred_element_type=F32)


def _dot_tn(a, b):
    return lax.dot_general(a, b, (((0,), (0,)), ((), ())), preferred_element_type=F32)


def _split(a):
    hi = a.astype(BF16)
    return hi, (a - hi.astype(F32)).astype(BF16)


def _dot_split(a, b):
    a_hi, a_lo = _split(a)
    b_hi, b_lo = _split(b)
    return _dot(a_hi, b_hi) + (_dot(a_hi, b_lo) + _dot(a_lo, b_hi))


def _dot_exact_lhs(a01, b):
    a16 = a01.astype(BF16)
    b_hi = b.astype(BF16)
    rem = b - b_hi.astype(F32)
    b_mid = rem.astype(BF16)
    b_lo = (rem - b_mid.astype(F32)).astype(BF16)
    return _dot(a16, b_hi) + (_dot(a16, b_mid) + _dot(a16, b_lo))


def _seg_sum(sq, segm):
    return _dot(sq.astype(BF16), segm)


def _sigmoid(x):
    return 1.0 / (1.0 + jnp.exp(-x))


def _softplus(x):
    return jnp.maximum(x, 0.0) + jnp.log1p(jnp.exp(-jnp.abs(x)))


def _log_sigmoid(x):
    return -_softplus(-x)


def _lane_cumsum(x, seg):
    lane = lax.broadcasted_iota(jnp.int32, x.shape, 1)
    pos = lane & (seg - 1)
    s = 1
    while s < seg:
        x = x + jnp.where(pos >= s, pltpu.roll(x, s, 1), 0.0)
        s *= 2
    return x


def _seg_matrix(n):
    i = jnp.arange(n) // HEAD_DIM
    return (i[:, None] == i[None, :]).astype(BF16)


_W_ROWS = 128


def _regroup_w_in(win_ref, wbf_ref):
    def body(r, carry):
        rows = pl.ds(pl.multiple_of(r * _W_ROWS, _W_ROWS), _W_ROWS)
        for i, g in enumerate(_WIDE):
            wbf_ref[rows, _WIDE_OFF[i]:_WIDE_OFF[i + 1]] = (
                win_ref[0, rows, _OFFS[g]:_OFFS[g + 1]].astype(BF16))
        small = _WIDE_OFF[-1]
        wbf_ref[rows, small:small + LANES] = jnp.zeros((_W_ROWS, LANES), BF16)
        for pos, g in ((_SM_F, 3), (_SM_A, 6), (_SM_B, 7), (_SM_A1, 12)):
            wbf_ref[rows, small + pos:small + pos + _SIZES[g]] = (
                win_ref[0, rows, _OFFS[g]:_OFFS[g + 1]].astype(BF16))
        return carry

    lax.fori_loop(0, win_ref.shape[1] // _W_ROWS, body, 0)


def _proj_kernel(x_ref, gain_ref, win_ref, segm_ref, qg_ref, kg_ref,
                 fb_ref, alog_ref, dtb_ref, wa2_ref, ba_ref,
                 fq_o, ka_o, fv_o, gqkv_o, ggate_o, lq_o, lk_o, lv_o, lr_o, la_o,
                 g_o, beta_o, carry_ref, wbf_ref):
    tm = x_ref.shape[1]

    @pl.when((pl.program_id(0) == 0) & (pl.program_id(1) == 0))
    def _():
        _regroup_w_in(win_ref, wbf_ref)

    x = x_ref[0]
    ms = jnp.mean(x * x, axis=-1, keepdims=True)
    hb = (x * lax.rsqrt(ms + RMS_EPS) * gain_ref[...]).astype(BF16)

    def wide(i):
        return _dot(hb, wbf_ref[:, _WIDE_OFF[i]:_WIDE_OFF[i + 1]])

    segm = segm_ref[...]
    q = wide(0)
    q = q * lax.rsqrt(_seg_sum(q * q, segm) * (1.0 / HEAD_DIM) + RMS_EPS) * qg_ref[...]
    fq_o[0] = q.astype(BF16).T
    k = wide(1)
    k = k * lax.rsqrt(_seg_sum(k * k, segm) * (1.0 / HEAD_DIM) + RMS_EPS) * kg_ref[...]
    fv_o[0] = wide(2).astype(BF16).T
    gqkv_o[0] = wide(3).astype(BF16)
    ggate_o[0] = wide(4).astype(BF16)
    lq_o[0] = wide(5).astype(BF16)
    lk_o[0] = wide(6).astype(BF16)
    lv_o[0] = wide(7).astype(BF16)
    lr_o[0] = wide(8).astype(BF16)

    sm = _dot(hb, wbf_ref[:, _WIDE_OFF[-1]:_WIDE_OFF[-1] + LANES])
    la_logit = _dot_split(sm, wa2_ref[...]) + ba_ref[...]
    la_o[0] = _log_sigmoid(la_logit) * (1.0 / GLA_TAU)

    smt = sm.T
    log_f = _log_sigmoid(smt[_SM_F:_SM_F + 8] + fb_ref[...])

    @pl.when(pl.program_id(1) == 0)
    def _():
        carry_ref[...] = jnp.zeros_like(carry_ref)

    cum = _lane_cumsum(log_f, tm) + carry_ref[:, 0:1]
    carry_ref[...] = jnp.broadcast_to(cum[:, tm - 1:tm], carry_ref.shape)
    pieces = []
    rem = cum * _LOG2E
    for _ in range(_C_PIECES):
        piece = rem.astype(BF16).astype(F32)
        pieces.append(piece)
        rem = rem - piece
    zero_row = jnp.zeros((1, tm), F32)
    rows = []
    for h in range(H_FOX):
        rows += [piece[h:h + 1] for piece in pieces] + [zero_row]
    rows.append(jnp.zeros((LANES - 4 * H_FOX, tm), F32))
    ptw = jnp.concatenate(rows, axis=0).T
    lane = lax.broadcasted_iota(jnp.int32, (tm, LANES), 1)
    for h in range(H_FOX):
        kp = k[:, (h // 2) * LANES:(h // 2 + 1) * LANES]
        if h % 2:
            kp = pltpu.roll(kp, HEAD_DIM, 1)
        cp = pltpu.roll(ptw, HEAD_DIM - 4 * h, 1)
        ka = jnp.where(lane < HEAD_DIM, kp, jnp.where(lane < HEAD_DIM + 4, cp, 0.0))
        ka_o[0, h] = ka.astype(BF16)
    g_o[0] = -jnp.exp(alog_ref[...]) * _softplus(smt[_SM_A:_SM_A + 8] + dtb_ref[...])
    beta_o[0] = _sigmoid(smt[_SM_B:_SM_B + 8])


def _proj_call(x, gain, w_in_all, layer, segm, qg, kg, fb, alog, dtb, wa2, ba, tm):
    b, s, d = x.shape
    tm = min(tm, s)
    const = lambda shape: pl.BlockSpec(shape, lambda i, j: (0,) * len(shape),
                                       pipeline_mode=pl.Buffered(1))
    tok = lambda w: pl.BlockSpec((1, tm, w), lambda i, j: (i, j, 0))
    row = pl.BlockSpec((1, 8, tm), lambda i, j: (i, 0, j))
    widths = [_SIZES[g] for g in _WIDE]
    out_shape = ([jax.ShapeDtypeStruct((b, s, w), BF16) for w in widths]
                 + [jax.ShapeDtypeStruct((b, s, LANES), F32)]
                 + [jax.ShapeDtypeStruct((b, 8, s), F32)] * 2)
    out_specs = [tok(w) for w in widths] + [tok(LANES)] + [row] * 2
    for i in (0, 2):
        out_shape[i] = jax.ShapeDtypeStruct((b, widths[i], s), BF16)
        out_specs[i] = pl.BlockSpec((1, widths[i], tm), lambda i, j: (i, 0, j))
    out_shape[1] = jax.ShapeDtypeStruct((b, H_FOX, s, LANES), BF16)
    out_specs[1] = pl.BlockSpec((1, H_FOX, tm, LANES), lambda i, j: (i, 0, j, 0))
    return pl.pallas_call(
        _proj_kernel,
        grid=(b, s // tm),
        in_specs=[tok(d), const((1, d)),
                  pl.BlockSpec((1,) + w_in_all.shape[1:], lambda i, j: (layer, 0, 0),
                               pipeline_mode=pl.Buffered(1)),
                  const(segm.shape),
                  const(qg.shape), const(kg.shape), const((8, 1)), const((8, 1)), const((8, 1)),
                  const(wa2.shape), const(ba.shape)],
        out_specs=out_specs,
        out_shape=out_shape,
        scratch_shapes=[pltpu.VMEM((8, LANES), F32),
                        pltpu.VMEM((d, _WIDE_OFF[-1] + LANES), BF16)],
        compiler_params=pltpu.CompilerParams(
            dimension_semantics=("arbitrary", "arbitrary"), vmem_limit_bytes=VMEM_LIMIT),
        name="proj",
    )(x, gain, w_in_all, segm, qg, kg, fb, alog, dtb, wa2, ba)


_ACC_ROWS = HEAD_DIM + 16


def _fox_kernel(qt_ref, k_ref, vt_ref, aug_ref, ones_ref, o_ref, m_ref, acc_ref):
    tq = qt_ref.shape[2]
    qi = pl.program_id(1)
    heads = range(H_FOX)
    aug = aug_ref[...]
    qts = [jnp.concatenate([qt_ref[0, h * HEAD_DIM:(h + 1) * HEAD_DIM, :], aug], axis=0)
           for h in heads]
    m_ref[...] = jnp.full_like(m_ref, -jnp.inf)
    acc_ref[...] = jnp.zeros_like(acc_ref)
    krow = lax.broadcasted_iota(jnp.int32, (tq, tq), 0)
    qcol = lax.broadcasted_iota(jnp.int32, (tq, tq), 1)
    ones = ones_ref[...]

    def step(blocks, masked):
        k0s = [pl.multiple_of(j * tq, tq) for j in blocks]
        ss = [[_dot(k_ref[0, h, pl.ds(k0, tq), :], qts[h]) for k0 in k0s]
              for h in heads]
        if masked:
            ss = [[jnp.where(krow <= qcol, s, -jnp.inf) for s in sh] for sh in ss]
        m_prev = [m_ref[h] for h in heads]
        m_new = []
        for h in heads:
            m = m_prev[h]
            for s in ss[h]:
                m = jnp.maximum(m, jnp.max(s, axis=0, keepdims=True))
            m_new.append(m)
        ps = [[jnp.exp2(s - m_new[h]).astype(BF16) for s in ss[h]] for h in heads]
        for h in heads:
            acc = jnp.exp2(m_prev[h] - m_new[h]) * acc_ref[h]
            for k0, p in zip(k0s, ps[h]):
                vta = jnp.concatenate(
                    [vt_ref[0, h * HEAD_DIM:(h + 1) * HEAD_DIM, pl.ds(k0, tq)], ones], axis=0)
                acc = acc + _dot(vta, p)
            acc_ref[h] = acc
            m_ref[h] = m_new[h]

    step([qi], True)

    def body(j2, carry):
        step([2 * j2, 2 * j2 + 1], False)
        return carry

    lax.fori_loop(0, qi // 2, body, 0)

    @pl.when(qi % 2 == 1)
    def _():
        step([qi - 1], False)
    outs = []
    for h in heads:
        acc = acc_ref[h]
        outs.append(acc[0:HEAD_DIM] / acc[HEAD_DIM:HEAD_DIM + 1])
    o_ref[0] = jnp.concatenate(outs, axis=0).T.astype(o_ref.dtype)


def _fox_call(qt, ka, vt, tq):
    b, _, s = qt.shape
    tq = min(tq, s)
    aug = jnp.broadcast_to(jnp.where(jnp.arange(HEAD_DIM)[:, None] < _C_PIECES, -1.0, 0.0),
                           (HEAD_DIM, tq)).astype(BF16)
    ones = jnp.broadcast_to(jnp.where(jnp.arange(16)[:, None] == 0, 1.0, 0.0), (16, tq)).astype(BF16)
    return pl.pallas_call(
        _fox_kernel,
        grid=(b, s // tq),
        in_specs=[pl.BlockSpec((1, FOX_DIM, tq), lambda i, j: (i, 0, j)),
                  pl.BlockSpec((1, H_FOX, s, LANES), lambda i, j: (i, 0, 0, 0)),
                  pl.BlockSpec((1, FOX_DIM, s), lambda i, j: (i, 0, 0)),
                  pl.BlockSpec(aug.shape, lambda i, j: (0, 0)),
                  pl.BlockSpec(ones.shape, lambda i, j: (0, 0))],
        out_specs=pl.BlockSpec((1, tq, FOX_DIM), lambda i, j: (i, j, 0)),
        out_shape=jax.ShapeDtypeStruct((b, s, FOX_DIM), BF16),
        scratch_shapes=[pltpu.VMEM((H_FOX, 1, tq), F32),
                        pltpu.VMEM((H_FOX, _ACC_ROWS, tq), F32)],
        compiler_params=pltpu.CompilerParams(
            dimension_semantics=("arbitrary", "arbitrary"),
            vmem_limit_bytes=VMEM_LIMIT),
        name="fox",
    )(qt, ka, vt, aug, ones)


def _bmm(a, b):
    return jnp.einsum('gmk,gkn->gmn', a.astype(BF16), b.astype(BF16),
                      preferred_element_type=F32)


def _pair_diag(x):
    xb = x.astype(BF16)
    low = lax.broadcasted_iota(jnp.int32, (1,) + xb.shape[1:], 2) < HEAD_DIM
    zero = jnp.zeros_like(xb)
    return jnp.concatenate([jnp.where(low, xb, zero), jnp.where(low, zero, xb)], axis=1)


def _unit_lower_inverse(a_strict, eye):
    n = -a_strict
    t = eye + n
    nd = _pair_diag(n)
    size = 1
    while 2 * size < CHUNK:
        n = _bmm(n, nd)
        nd = _pair_diag(n)
        t = t + _bmm(t, nd)
        size *= 2
    return t


def _gdn_kernel(qkv_ref, convw_ref, g_ref, beta_ref, segm_ref, o_ref, state_ref, xext_ref):
    tc = qkv_ref.shape[1]
    pad = SUBLANES

    @pl.when(pl.program_id(1) == 0)
    def _():
        state_ref[...] = jnp.zeros_like(state_ref)
        xext_ref[0:pad, :] = jnp.zeros((pad, xext_ref.shape[1]), F32)

    x = qkv_ref[0].astype(F32)
    xext_ref[pad:pad + tc, :] = x
    y = convw_ref[0:1, :] * xext_ref[pad - 3:pad - 3 + tc, :]
    for i in range(1, CONV_K):
        y = y + convw_ref[i:i + 1, :] * xext_ref[pad - 3 + i:pad - 3 + i + tc, :]
    xext_ref[0:pad, :] = x[tc - pad:tc, :]
    y = y * _sigmoid(y)

    segm = segm_ref[...]
    q = y[:, 0:GDN_DIM]
    k = y[:, GDN_DIM:2 * GDN_DIM]
    v = y[:, 2 * GDN_DIM:3 * GDN_DIM]
    q = q * lax.rsqrt(_seg_sum(q * q, segm) + RMS_EPS) * (HEAD_DIM ** -0.5)
    k = k * lax.rsqrt(_seg_sum(k * k, segm) + RMS_EPS)

    kt = k.T

    dec_row = _lane_cumsum(g_ref[0], CHUNK)
    dec_col = dec_row.T
    beta_col = beta_ref[0].T

    head_of_lane = lax.broadcasted_iota(jnp.int32, (tc, GDN_DIM), 1) // HEAD_DIM
    dexp = jnp.zeros((tc, GDN_DIM), F32)
    bexp = jnp.zeros((tc, GDN_DIM), F32)
    for h in range(H_GDN):
        dexp = jnp.where(head_of_lane == h, dec_col[:, h:h + 1], dexp)
        bexp = jnp.where(head_of_lane == h, beta_col[:, h:h + 1], bexp)
    edec = jnp.exp(dexp)
    kb = k * bexp
    vb = v * bexp
    kbe = kb * edec
    qd = q * edec

    nc = tc // CHUNK
    npair = H_GDN // 2
    index = [(c, p) for c in range(nc) for p in range(npair)]

    def split(a):
        return jnp.stack([a[c * CHUNK:(c + 1) * CHUNK, p * LANES:(p + 1) * LANES]
                          for c, p in index])

    q3, kb3, vb3, kbe3, qd3, dcol = (split(a) for a in (q, kb, vb, kbe, qd, dexp))
    drow = jnp.stack([jnp.concatenate(
        [dec_row[2 * p + hh:2 * p + hh + 1, c * CHUNK:(c + 1) * CHUNK] for hh in range(2)], axis=1)
        for c, p in index])
    dlast = dcol[:, CHUNK - 1:CHUNK, :]

    low = lax.broadcasted_iota(jnp.int32, (CHUNK, LANES), 1) < HEAD_DIM
    kdt_list, kd_list = [], []
    for c, p in index:
        tile = kt[p * LANES:(p + 1) * LANES, (c // 2) * LANES:(c // 2 + 1) * LANES]
        swapped = pltpu.roll(tile, HEAD_DIM, 1)
        top = (tile if c % 2 == 0 else swapped)[0:HEAD_DIM]
        bot = (swapped if c % 2 == 0 else tile)[HEAD_DIM:]
        kdt_list.append(jnp.where(low, top, bot))
        kd_list.append(jnp.concatenate([jnp.where(low, top, 0.0), jnp.where(low, 0.0, bot)],
                                       axis=0).astype(BF16))
    kt3 = jnp.stack(kdt_list)
    ktd = jnp.stack(kd_list)

    ri = lax.broadcasted_iota(jnp.int32, (1, CHUNK, LANES), 1)
    cj = lax.broadcasted_iota(jnp.int32, (1, CHUNK, LANES), 2) & (HEAD_DIM - 1)
    causal = cj <= ri
    eye = (cj == ri).astype(F32)
    gamma = jnp.where(causal, jnp.exp(jnp.where(causal, dcol - drow, 0.0)), 0.0)
    a = jnp.where(cj < ri, _bmm(kb3, ktd) * gamma, 0.0)
    t = _unit_lower_inverse(a, eye)
    u3 = _bmm(t, _pair_diag(vb3))
    w3 = _bmm(t, _pair_diag(kbe3))
    intra = _bmm(q3, ktd) * gamma
    kdt = kt3 * jnp.exp(dlast - drow)
    elast = jnp.exp(dlast)

    s = state_ref[...]
    for c in range(nc):
        sl = slice(c * npair, (c + 1) * npair)
        sd = _pair_diag(s)
        v_new = u3[sl] - _bmm(w3[sl], sd)
        vd = _pair_diag(v_new)
        o = _bmm(qd3[sl], sd) + _bmm(intra[sl], vd)
        s = s * elast[sl] + _bmm(kdt[sl], vd)
        o_ref[0, c * CHUNK:(c + 1) * CHUNK, :] = jnp.concatenate(
            [o[p] for p in range(npair)], axis=1).astype(o_ref.dtype)
    state_ref[...] = s


def _gdn_call(qkv, convw, g, beta, segm, tc):
    b, s, w = qkv.shape
    tc = min(tc, s)
    return pl.pallas_call(
        _gdn_kernel,
        grid=(b, s // tc),
        in_specs=[pl.BlockSpec((1, tc, w), lambda i, j: (i, j, 0)),
                  pl.BlockSpec(convw.shape, lambda i, j: (0, 0)),
                  pl.BlockSpec((1, 8, tc), lambda i, j: (i, 0, j)),
                  pl.BlockSpec((1, 8, tc), lambda i, j: (i, 0, j)),
                  pl.BlockSpec(segm.shape, lambda i, j: (0, 0))],
        out_specs=pl.BlockSpec((1, tc, GDN_DIM), lambda i, j: (i, j, 0)),
        out_shape=jax.ShapeDtypeStruct((b, s, GDN_DIM), BF16),
        scratch_shapes=[pltpu.VMEM((H_GDN // 2, HEAD_DIM, LANES), F32),
                        pltpu.VMEM((tc + SUBLANES, w), F32)],
        compiler_params=pltpu.CompilerParams(
            dimension_semantics=("arbitrary", "arbitrary"), vmem_limit_bytes=VMEM_LIMIT),
        name="gdn",
    )(qkv, convw, g, beta, segm)


def _gla_kernel(q_ref, k_ref, v_ref, la_ref, tri_ref, mh_ref, mt_ref, o_ref,
                st_ref, kf_ref, bq_ref, vf_ref):
    tc = q_ref.shape[1]

    @pl.when(pl.program_id(1) == 0)
    def _():
        st_ref[...] = jnp.zeros_like(st_ref)

    row = lax.broadcasted_iota(jnp.int32, (CHUNK, GLA_K_DIM), 0)
    mh = mh_ref[...]
    for c in range(tc // CHUNK):
        r0, r1 = c * CHUNK, (c + 1) * CHUNK
        bq = _dot_exact_lhs(tri_ref[...], la_ref[0, r0:r1, :])
        q = q_ref[0, r0:r1, :].astype(F32) * (GLA_DK ** -0.5)
        k = k_ref[0, r0:r1, :].astype(F32)
        v = v_ref[0, r0:r1, :].astype(F32)
        kf_ref[...] = k
        bq_ref[...] = bq
        vf_ref[...] = v
        st = st_ref[...]
        o_inter = _dot_nt((q * jnp.exp(bq)).astype(BF16), st.astype(BF16))

        group_out = []
        for g0 in range(0, CHUNK, SUBLANES):
            n = CHUNK - g0
            qg, bg, rg = q[g0:, :], bq[g0:, :], row[g0:, :]
            es = []
            for j in range(g0, g0 + SUBLANES):
                kj = kf_ref[j:j + 1, :]
                bj = bq_ref[j:j + 1, :]
                es.append(jnp.where(rg >= j, jnp.exp(bg - bj), 0.0) * (qg * kj))
            p = _dot(jnp.concatenate(es, axis=0).astype(BF16), mh)
            acc = p[0:n] * vf_ref[g0:g0 + 1, :]
            for jj in range(1, SUBLANES):
                acc = acc + p[jj * n:(jj + 1) * n] * vf_ref[g0 + jj:g0 + jj + 1, :]
            group_out.append(acc)
        pieces = []
        for r in range(0, CHUNK, SUBLANES):
            piece = o_inter[r:r + SUBLANES]
            for gi, g0 in enumerate(range(0, r + SUBLANES, SUBLANES)):
                piece = piece + group_out[gi][r - g0:r - g0 + SUBLANES]
            pieces.append(piece)
        o_ref[0, r0:r1, :] = jnp.concatenate(pieces, axis=0).astype(o_ref.dtype)
        blast = bq[CHUNK - 1:CHUNK, :]
        kd = k * jnp.exp(blast - bq)
        upd = _dot_tn(v.astype(BF16), kd.astype(BF16))
        st_ref[...] = (st * jnp.exp(blast) + upd) * mt_ref[...]


def _gla_call(q, k, v, la, tc):
    b, s, _ = q.shape
    tc = min(tc, s)
    tri = (jnp.arange(CHUNK)[:, None] >= jnp.arange(CHUNK)[None, :]).astype(F32)
    hk = jnp.arange(GLA_K_DIM) // GLA_DK
    hv = jnp.arange(GLA_V_DIM) // HEAD_DIM
    mh = (hk[:, None] == hv[None, :]).astype(BF16)
    mt = (hv[:, None] == hk[None, :]).astype(F32)
    tok = lambda w: pl.BlockSpec((1, tc, w), lambda i, j: (i, j, 0))
    const = lambda a: pl.BlockSpec(a.shape, lambda i, j: (0, 0))
    return pl.pallas_call(
        _gla_kernel,
        grid=(b, s // tc),
        in_specs=[tok(GLA_K_DIM), tok(GLA_K_DIM), tok(GLA_V_DIM), tok(GLA_K_DIM),
                  const(tri), const(mh), const(mt)],
        out_specs=tok(GLA_V_DIM),
        out_shape=jax.ShapeDtypeStruct((b, s, GLA_V_DIM), BF16),
        scratch_shapes=[pltpu.VMEM((GLA_V_DIM, GLA_K_DIM), F32),
                        pltpu.VMEM((CHUNK, GLA_K_DIM), F32),
                        pltpu.VMEM((CHUNK, GLA_K_DIM), F32),
                        pltpu.VMEM((CHUNK, GLA_V_DIM), F32)],
        compiler_params=pltpu.CompilerParams(
            dimension_semantics=("arbitrary", "arbitrary"), vmem_limit_bytes=VMEM_LIMIT),
        name="gla",
    )(q, k, v, la, tri, mh, mt)


def _head_norm(o_ref, segm, gain):
    o = o_ref[...].astype(F32)
    return o * lax.rsqrt(_seg_sum(o * o, segm) * (1.0 / HEAD_DIM) + RMS_EPS) * gain


def _silu(x):
    return x * _sigmoid(x)


def _out_kernel(x_ref, ofox_ref, ogdn_ref, ggate_ref, ogla_ref, lr_ref, wout_ref, segm_ref,
                gf_ref, gg_ref, gl_ref, fgain_ref, wr_ref, br_ref, tri_ref,
                x1_o, h2_o, meta_o, cnt_o, carry_ref):
    tm = x_ref.shape[0]
    segm = segm_ref[...]
    a = _head_norm(ofox_ref, segm, gf_ref[...])
    bb = _head_norm(ogdn_ref, segm, gg_ref[...]) * _silu(ggate_ref[...].astype(F32))
    cc = (_head_norm(ogla_ref, segm[:GLA_V_DIM, :GLA_V_DIM], gl_ref[...])
          * _silu(lr_ref[...].astype(F32)))
    y = (x_ref[...]
         + _dot(a.astype(BF16), wout_ref[0:FOX_DIM, :].astype(BF16))
         + _dot(bb.astype(BF16), wout_ref[FOX_DIM:FOX_DIM + GDN_DIM, :].astype(BF16))
         + _dot(cc.astype(BF16), wout_ref[FOX_DIM + GDN_DIM:, :].astype(BF16)))
    x1_o[...] = y
    ms = jnp.mean(y * y, axis=-1, keepdims=True)
    h2 = y * lax.rsqrt(ms + RMS_EPS) * fgain_ref[...]
    _to_token_tiles(h2_o, h2)

    wr_hi, wr_lo = _split(wr_ref[...])
    h2_hi, h2_lo = _split(h2)
    hi_terms = _dot(h2_hi, jnp.concatenate([wr_hi, wr_lo], axis=1))
    logits = (hi_terms[:, :LANES] + (hi_terms[:, LANES:] + _dot(h2_lo, wr_hi))
              + br_ref[...])
    lane = lax.broadcasted_iota(jnp.int32, logits.shape, 1).astype(F32)
    big = float(4 * LANES)
    ninf = -jnp.inf
    gl = jnp.where(lane < _R_GROUP + N_GROUPS, logits, ninf)
    gmax = jnp.max(gl, axis=1, keepdims=True)
    group_p = 1.0 / jnp.sum(jnp.exp(gl - gmax), axis=1, keepdims=True)
    gidx = jnp.min(jnp.where(gl == gmax, lane, big), axis=1, keepdims=True)
    elane = lane - _R_EXPERT
    group_of_lane = jnp.floor(elane * (1.0 / EXPERTS_PER_GROUP))
    in_group = (elane >= 0) & (elane < N_EXPERTS) & (group_of_lane == gidx)
    el = jnp.where(in_group, logits, ninf)
    m1 = jnp.max(el, axis=1, keepdims=True)
    i1 = jnp.min(jnp.where(el == m1, lane, big), axis=1, keepdims=True)
    el2 = jnp.where(lane == i1, ninf, el)
    m2 = jnp.max(el2, axis=1, keepdims=True)
    i2 = jnp.min(jnp.where(el2 == m2, lane, big), axis=1, keepdims=True)
    t = jnp.exp(m2 - m1)
    g1 = group_p / (1.0 + t)
    g2 = group_p * t / (1.0 + t)

    @pl.when(pl.program_id(0) == 0)
    def _():
        carry_ref[...] = jnp.zeros_like(carry_ref)

    sel = jnp.where((lane == i1) | (lane == i2), 1.0, 0.0)
    carry = carry_ref[0:1, :]
    rank = _dot(tri_ref[...], sel.astype(BF16)) + carry
    rank1 = jnp.sum(jnp.where(lane == i1, rank, 0.0), axis=1, keepdims=True)
    rank2 = jnp.sum(jnp.where(lane == i2, rank, 0.0), axis=1, keepdims=True)
    new_carry = carry + jnp.sum(sel, axis=0, keepdims=True)
    carry_ref[...] = jnp.broadcast_to(new_carry, carry_ref.shape)
    cnt_o[...] = jnp.broadcast_to(new_carry, cnt_o.shape)
    cols = [i1 - _R_EXPERT, i2 - _R_EXPERT, rank1, rank2, g1, g2]
    meta = jnp.zeros(logits.shape, F32)
    for idx, col in enumerate(cols):
        meta = jnp.where(lane == idx, col, meta)
    meta_o[...] = meta


def _out_call(x, ofox, ogdn, ggate, ogla, lr, wout, segm, gf, gg, gl, fgain, wr, br, tm):
    n, d = x.shape
    tm = min(tm, n)
    tri = (jnp.arange(tm)[:, None] > jnp.arange(tm)[None, :]).astype(BF16)
    tok = lambda w: pl.BlockSpec((tm, w), lambda i: (i, 0))
    const = lambda a: pl.BlockSpec(a.shape, lambda i: (0,) * a.ndim,
                                   pipeline_mode=pl.Buffered(1))
    return pl.pallas_call(
        _out_kernel,
        grid=(n // tm,),
        in_specs=[tok(d), tok(FOX_DIM), tok(GDN_DIM), tok(GDN_DIM), tok(GLA_V_DIM), tok(GLA_V_DIM),
                  const(wout), const(segm), const(gf), const(gg), const(gl), const(fgain),
                  const(wr), const(br), const(tri)],
        out_specs=[tok(d), pl.BlockSpec((tm * SUBLANES, LANES), lambda i: (i, 0)), tok(LANES),
                   pl.BlockSpec((8, LANES), lambda i: (0, 0))],
        out_shape=[jax.ShapeDtypeStruct((n, d), F32),
                   jax.ShapeDtypeStruct((n * SUBLANES, LANES), F32),
                   jax.ShapeDtypeStruct((n, LANES), F32), jax.ShapeDtypeStruct((8, LANES), F32)],
        scratch_shapes=[pltpu.VMEM((8, LANES), F32)],
        compiler_params=pltpu.CompilerParams(
            dimension_semantics=("arbitrary",), vmem_limit_bytes=VMEM_LIMIT),
        name="out_router",
    )(x, ofox, ogdn, ggate, ogla, lr, wout, segm, gf, gg, gl, fgain, wr, br, tri)


_DMA_UNROLL = 8


def _tile_rows(r):
    return pl.ds(pl.multiple_of(r * SUBLANES, SUBLANES), SUBLANES)


def _to_token_tiles(ref, x):
    t = x.shape[0]
    for s in range(SUBLANES):
        ref[pl.ds(s, t, stride=SUBLANES), :] = x[:, s * LANES:(s + 1) * LANES]


def _from_token_tiles(ref, t):
    return [ref[pl.ds(s, t, stride=SUBLANES), :] for s in range(SUBLANES)]


def _dispatch_kernel(dest_ref, h_ref, xb_in_ref, xb_ref, sem):
    del xb_in_ref
    td = h_ref.shape[0] // SUBLANES

    def row_copy(t, d):
        return pltpu.make_async_copy(h_ref.at[_tile_rows(t), :], xb_ref.at[_tile_rows(d), :], sem)

    def issue(t, carry):
        for kk in range(TOP_K):
            row_copy(t, dest_ref[0, 0, TOP_K * t + kk]).start()
        return carry

    lax.fori_loop(0, td, issue, 0, unroll=_DMA_UNROLL)

    def drain(t, carry):
        for kk in range(TOP_K):
            row_copy(0, 0).wait()
        return carry

    lax.fori_loop(0, td, drain, 0, unroll=_DMA_UNROLL)


def _dispatch_call(dest, h2, n_rows, td):
    n = h2.shape[0] // SUBLANES
    td = min(td, n)
    dest3 = dest.reshape(n // td, 1, TOP_K * td)
    xb0 = jnp.zeros((n_rows * SUBLANES, LANES), h2.dtype)
    return pl.pallas_call(
        _dispatch_kernel,
        grid=(n // td,),
        in_specs=[pl.BlockSpec((1, 1, TOP_K * td), lambda i: (i, 0, 0), memory_space=pltpu.SMEM),
                  pl.BlockSpec((td * SUBLANES, LANES), lambda i: (i, 0)),
                  pl.BlockSpec(memory_space=pl.ANY)],
        out_specs=pl.BlockSpec(memory_space=pl.ANY),
        out_shape=jax.ShapeDtypeStruct((n_rows * SUBLANES, LANES), h2.dtype),
        scratch_shapes=[pltpu.SemaphoreType.DMA(())],
        input_output_aliases={2: 0},
        compiler_params=pltpu.CompilerParams(
            dimension_semantics=("arbitrary",), has_side_effects=True),
        name="dispatch",
    )(dest3, h2, xb0)


def _expert_kernel(be_ref, nu_ref, x_ref, wg_ref, wu_ref, wd_ref, y_ref, wgb_ref, wub_ref, wdb_ref):
    i = pl.program_id(0)
    used = i < nu_ref[0]

    @pl.when(used & ((i == 0) | (be_ref[i] != be_ref[jnp.maximum(i - 1, 0)])))
    def _():
        wgb_ref[...] = wg_ref[0].astype(BF16)
        wub_ref[...] = wu_ref[0].astype(BF16)
        wdb_ref[...] = wd_ref[0].astype(BF16)

    @pl.when(used)
    def _():
        tmb = x_ref.shape[0] // SUBLANES
        x = jnp.concatenate([blk.astype(BF16) for blk in _from_token_tiles(x_ref, tmb)], axis=1)
        a = _dot(x, wgb_ref[...])
        u = _dot(x, wub_ref[...])
        hmid = (_silu(a) * u).astype(BF16)
        _to_token_tiles(y_ref, _dot(hmid, wdb_ref[...]))

    @pl.when(jnp.logical_not(used))
    def _():
        y_ref[...] = jnp.zeros_like(y_ref)


def _expert_call(block_e, n_used, xb, wg, wu, wd, tmb):
    n_rows = xb.shape[0] // SUBLANES
    d = SUBLANES * LANES
    de = wg.shape[-1]
    n_blocks = n_rows // tmb

    def xmap(i, be, nu):
        return (jnp.minimum(i, jnp.maximum(nu[0] - 1, 0)), 0)

    wmap = lambda i, be, nu: (be[i], 0, 0)
    return pl.pallas_call(
        _expert_kernel,
        grid_spec=pltpu.PrefetchScalarGridSpec(
            num_scalar_prefetch=2,
            grid=(n_blocks,),
            in_specs=[pl.BlockSpec((tmb * SUBLANES, LANES), xmap),
                      pl.BlockSpec((1, d, de), wmap),
                      pl.BlockSpec((1, d, de), wmap),
                      pl.BlockSpec((1, de, d), wmap)],
            out_specs=pl.BlockSpec((tmb * SUBLANES, LANES), lambda i, be, nu: (i, 0)),
            scratch_shapes=[pltpu.VMEM((d, de), BF16), pltpu.VMEM((d, de), BF16),
                            pltpu.VMEM((de, d), BF16)],
        ),
        out_shape=jax.ShapeDtypeStruct((n_rows * SUBLANES, LANES), F32),
        compiler_params=pltpu.CompilerParams(
            dimension_semantics=("arbitrary",), vmem_limit_bytes=VMEM_LIMIT),
        name="experts",
    )(block_e, n_used, xb, wg, wu, wd)


def _combine_kernel(dest_ref, x1_ref, meta_ref, yb_ref, o_ref, buf_ref, sem):
    td = x1_ref.shape[0]

    def row_copy(t, kk, d):
        return pltpu.make_async_copy(yb_ref.at[_tile_rows(d), :], buf_ref.at[kk, _tile_rows(t), :],
                                     sem)

    def issue(t, carry):
        for kk in range(TOP_K):
            row_copy(t, kk, dest_ref[0, 0, TOP_K * t + kk]).start()
        return carry

    lax.fori_loop(0, td, issue, 0, unroll=_DMA_UNROLL)

    def drain(t, carry):
        for kk in range(TOP_K):
            row_copy(0, kk, 0).wait()
        return carry

    lax.fori_loop(0, td, drain, 0, unroll=_DMA_UNROLL)
    meta = meta_ref[...]
    g1, g2 = meta[:, 4:5], meta[:, 5:6]
    y1 = _from_token_tiles(buf_ref.at[0], td)
    y2 = _from_token_tiles(buf_ref.at[1], td)
    for s in range(SUBLANES):
        lanes = slice(s * LANES, (s + 1) * LANES)
        o_ref[:, lanes] = x1_ref[:, lanes] + g1 * y1[s] + g2 * y2[s]


def _combine_call(dest, x1, meta, yb, td):
    n, d = x1.shape
    td = min(td, n)
    dest3 = dest.reshape(n // td, 1, TOP_K * td)
    return pl.pallas_call(
        _combine_kernel,
        grid=(n // td,),
        in_specs=[pl.BlockSpec((1, 1, TOP_K * td), lambda i: (i, 0, 0), memory_space=pltpu.SMEM),
                  pl.BlockSpec((td, d), lambda i: (i, 0)),
                  pl.BlockSpec((td, LANES), lambda i: (i, 0)),
                  pl.BlockSpec(memory_space=pl.ANY)],
        out_specs=pl.BlockSpec((td, d), lambda i: (i, 0)),
        out_shape=jax.ShapeDtypeStruct((n, d), F32),
        scratch_shapes=[pltpu.VMEM((TOP_K, td * SUBLANES, LANES), yb.dtype),
                        pltpu.SemaphoreType.DMA(())],
        compiler_params=pltpu.CompilerParams(
            dimension_semantics=("arbitrary",), vmem_limit_bytes=VMEM_LIMIT),
        name="combine",
    )(dest3, x1, meta, yb)


TM_PROJ = 512
TQ_FOX = 256
TC_GDN = 256
TC_GLA = 256
TM_OUT = 512
TD_MOE = 256
TMB_EXPERT = 256


def _place(width, parts):
    cols, at = [], 0
    for pos, blk in parts:
        if pos > at:
            cols.append(jnp.zeros((blk.shape[0], pos - at), blk.dtype))
        cols.append(blk)
        at = pos + blk.shape[1]
    if width > at:
        cols.append(jnp.zeros((parts[0][1].shape[0], width - at), parts[0][1].dtype))
    return jnp.concatenate(cols, axis=1)


def _pad8(v):
    return jnp.zeros((8,), F32).at[:v.shape[0]].set(v.astype(F32)).reshape(8, 1)


def _token_mixer(x, attn_norm, w_in_all, layer, fox_q_norm, fox_k_norm, fox_f_bias,
                 gdn_conv, gdn_a_log, gdn_dt_bias, gla_w_a2, gla_b_a):
    b, s, d = x.shape
    wa2 = jnp.pad(gla_w_a2, ((_SM_A1, LANES - _SM_A1 - GLA_RANK), (0, 0)))
    segm = _seg_matrix(FOX_DIM)
    qg = (jnp.tile(fox_q_norm, H_FOX) * (HEAD_DIM ** -0.5 * _LOG2E)).reshape(1, FOX_DIM)
    kg = jnp.tile(fox_k_norm, H_FOX).reshape(1, FOX_DIM)
    outs = _proj_call(x, attn_norm.reshape(1, d), w_in_all, layer, segm, qg, kg,
                      _pad8(fox_f_bias), _pad8(gdn_a_log), _pad8(gdn_dt_bias),
                      wa2, gla_b_a.reshape(1, GLA_K_DIM), TM_PROJ)
    fq, fka, fv, gqkv, ggate, lq, lk, lv, lr, la, g, beta = outs
    o_fox = _fox_call(fq, fka, fv, TQ_FOX)
    o_gdn = _gdn_call(gqkv, gdn_conv.astype(F32), g, beta, segm, TC_GDN)
    o_gla = _gla_call(lq, lk, lv, la, TC_GLA)
    return o_fox, o_gdn, ggate, o_gla, lr


def _layer(x, p, layer, w_in_all, experts):
    b, s, d = x.shape
    n = b * s
    o_fox, o_gdn, ggate, o_gla, lr = _token_mixer(
        x, p['attn_norm'], w_in_all, layer, p['fox_q_norm'], p['fox_k_norm'], p['fox_f_bias'],
        p['gdn_conv'], p['gdn_a_log'], p['gdn_dt_bias'], p['gla_w_a2'], p['gla_b_a'])
    wr = _place(LANES, [(_R_GROUP, p['w_router_group']), (_R_EXPERT, p['w_router_expert'])])
    br = _place(LANES, [(_R_GROUP, p['b_router_group'].reshape(1, -1)),
                        (_R_EXPERT, p['b_router_expert'].reshape(1, -1))])
    flat = lambda a: a.reshape(n, a.shape[-1])
    x1, h2, meta, cnt = _out_call(
        flat(x), flat(o_fox), flat(o_gdn), flat(ggate), flat(o_gla), flat(lr),
        p['w_out'], _seg_matrix(FOX_DIM),
        jnp.tile(p['fox_o_norm'], H_FOX).reshape(1, FOX_DIM),
        jnp.tile(p['gdn_o_norm'], H_GDN).reshape(1, GDN_DIM),
        jnp.tile(p['gla_o_norm'], H_GLA).reshape(1, GLA_V_DIM),
        p['ffn_norm'].reshape(1, d), wr, br, TM_OUT)

    tmb = TMB_EXPERT
    counts = cnt[0, _R_EXPERT:_R_EXPERT + N_EXPERTS].astype(jnp.int32)
    padded = (counts + tmb - 1) // tmb * tmb
    pends = jnp.cumsum(padded)
    pstarts = pends - padded
    eid = meta[:, 0:TOP_K].astype(jnp.int32)
    rank = meta[:, TOP_K:2 * TOP_K].astype(jnp.int32)
    expert_ids = jnp.arange(N_EXPERTS, dtype=jnp.int32)
    start_of = jnp.sum(jnp.where(eid[..., None] == expert_ids, pstarts, 0), axis=-1)
    dest = (start_of + rank).reshape(-1)
    n_blocks = -(-(n * TOP_K) // tmb) + N_EXPERTS
    block_start = jnp.arange(n_blocks, dtype=jnp.int32) * tmb
    block_e = jnp.minimum(jnp.sum(pends[None, :] <= block_start[:, None], axis=1),
                          N_EXPERTS - 1).astype(jnp.int32)
    n_used = (pends[-1:] // tmb).astype(jnp.int32)

    xb = _dispatch_call(dest, h2, n_blocks * tmb, TD_MOE)
    yb = _expert_call(block_e + layer * N_EXPERTS, n_used, xb, *experts, tmb)
    x2 = _combine_call(dest, x1, meta, yb, TD_MOE)
    return x2.reshape(b, s, d)


_PARAM_NAMES = ['attn_norm', 'w_in', 'fox_q_norm', 'fox_k_norm', 'fox_f_bias', 'fox_o_norm',
                'gdn_conv', 'gdn_a_log', 'gdn_dt_bias', 'gdn_o_norm',
                'gla_w_a2', 'gla_b_a', 'gla_o_norm', 'w_out',
                'ffn_norm', 'w_router_group', 'b_router_group', 'w_router_expert',
                'b_router_expert', 'w_expert_gate', 'w_expert_up', 'w_expert_down']


def kernel(x, attn_norm, w_in, fox_q_norm, fox_k_norm, fox_f_bias, fox_o_norm, gdn_conv, gdn_a_log, gdn_dt_bias, gdn_o_norm, gla_w_a2, gla_b_a, gla_o_norm, w_out, ffn_norm, w_router_group, b_router_group, w_router_expert, b_router_expert, w_expert_gate, w_expert_up, w_expert_down):
    params = dict(zip(_PARAM_NAMES, (
        attn_norm, w_in, fox_q_norm, fox_k_norm, fox_f_bias, fox_o_norm, gdn_conv, gdn_a_log,
        gdn_dt_bias, gdn_o_norm, gla_w_a2, gla_b_a, gla_o_norm, w_out, ffn_norm,
        w_router_group, b_router_group, w_router_expert, b_router_expert,
        w_expert_gate, w_expert_up, w_expert_down)))
    experts = tuple(params.pop(name).reshape((-1,) + params_shape[2:])
                    for name, params_shape in (('w_expert_gate', w_expert_gate.shape),
                                               ('w_expert_up', w_expert_up.shape),
                                               ('w_expert_down', w_expert_down.shape)))
    del params['w_in']
    for layer in range(attn_norm.shape[0]):
        x = _layer(x, {name: val[layer] for name, val in params.items()}, layer, w_in, experts)
    return x
```

```python
import functools

import jax
import jax.numpy as jnp
from jax import lax
from jax.experimental import pallas as pl
from jax.experimental.pallas import tpu as pltpu

F32 = jnp.float32
BF16 = jnp.bfloat16

HEAD_DIM = 64
H_FOX = 6
H_GDN = 6
H_GLA = 4
FOX_DIM = H_FOX * HEAD_DIM
GDN_DIM = H_GDN * HEAD_DIM
GLA_DK = 32
GLA_K_DIM = H_GLA * GLA_DK
GLA_V_DIM = H_GLA * HEAD_DIM
GLA_RANK = 16
GLA_TAU = 16.0
CONV_K = 4
CHUNK = 64
N_GROUPS = 4
EXPERTS_PER_GROUP = 8
N_EXPERTS = N_GROUPS * EXPERTS_PER_GROUP
TOP_K = 2
RMS_EPS = 1e-6

_LOG2E = 1.4426950408889634
_C_PIECES = 3

LANES = 128
SUBLANES = 8
VMEM_LIMIT = 56 * 1024 * 1024

_SIZES = [FOX_DIM, FOX_DIM, FOX_DIM, H_FOX, 3 * GDN_DIM, GDN_DIM, H_GDN, H_GDN,
          GLA_K_DIM, GLA_K_DIM, GLA_V_DIM, GLA_V_DIM, GLA_RANK]
_OFFS = [sum(_SIZES[:i]) for i in range(len(_SIZES) + 1)]
_WIDE = [0, 1, 2, 4, 5, 8, 9, 10, 11]
_WIDE_OFF = [0]
for _g in _WIDE:
    _WIDE_OFF.append(_WIDE_OFF[-1] + _SIZES[_g])
_SM_F, _SM_A, _SM_B, _SM_A1 = 0, 8, 16, 32
_R_GROUP, _R_EXPERT = 0, 32


def _dot(a, b):
    return jnp.dot(a, b, preferred_element_type=F32)


def _dot_nt(a, b):
    return lax.dot_general(a, b, (((1,), (1,)), ((), ())), preferred_element_type=F32)


def _dot_tn(a, b):
    return lax.dot_general(a, b, (((0,), (0,)), ((), ())), preferred_element_type=F32)


def _split(a):
    hi = a.astype(BF16)
    return hi, (a - hi.astype(F32)).astype(BF16)


def _dot_split(a, b):
    a_hi, a_lo = _split(a)
    b_hi, b_lo = _split(b)
    return _dot(a_hi, b_hi) + (_dot(a_hi, b_lo) + _dot(a_lo, b_hi))


def _dot_exact_lhs(a01, b):
    a16 = a01.astype(BF16)
    b_hi = b.astype(BF16)
    rem = b - b_hi.astype(F32)
    b_mid = rem.astype(BF16)
    b_lo = (rem - b_mid.astype(F32)).astype(BF16)
    return _dot(a16, b_hi) + (_dot(a16, b_mid) + _dot(a16, b_lo))


def _seg_sum(sq, segm):
    return _dot(sq.astype(BF16), segm)


def _sigmoid(x):
    return 1.0 / (1.0 + jnp.exp(-x))


def _softplus(x):
    return jnp.maximum(x, 0.0) + jnp.log1p(jnp.exp(-jnp.abs(x)))


def _log_sigmoid(x):
    return -_softplus(-x)


def _lane_cumsum(x, seg):
    lane = lax.broadcasted_iota(jnp.int32, x.shape, 1)
    pos = lane & (seg - 1)
    s = 1
    while s < seg:
        x = x + jnp.where(pos >= s, pltpu.roll(x, s, 1), 0.0)
        s *= 2
    return x


def _seg_matrix(n):
    i = jnp.arange(n) // HEAD_DIM
    return (i[:, None] == i[None, :]).astype(BF16)


_W_ROWS = 128


def _regroup_w_in(win_ref, wbf_ref):
    def body(r, carry):
        rows = pl.ds(pl.multiple_of(r * _W_ROWS, _W_ROWS), _W_ROWS)
        for i, g in enumerate(_WIDE):
            wbf_ref[rows, _WIDE_OFF[i]:_WIDE_OFF[i + 1]] = (
                win_ref[0, rows, _OFFS[g]:_OFFS[g + 1]].astype(BF16))
        small = _WIDE_OFF[-1]
        wbf_ref[rows, small:small + LANES] = jnp.zeros((_W_ROWS, LANES), BF16)
        for pos, g in ((_SM_F, 3), (_SM_A, 6), (_SM_B, 7), (_SM_A1, 12)):
            wbf_ref[rows, small + pos:small + pos + _SIZES[g]] = (
                win_ref[0, rows, _OFFS[g]:_OFFS[g + 1]].astype(BF16))
        return carry

    lax.fori_loop(0, win_ref.shape[1] // _W_ROWS, body, 0)


def _proj_kernel(x_ref, gain_ref, win_ref, segm_ref, qg_ref, kg_ref,
                 fb_ref, alog_ref, dtb_ref, wa2_ref, ba_ref,
                 fq_o, ka_o, fv_o, gqkv_o, ggate_o, lq_o, lk_o, lv_o, lr_o, la_o,
                 g_o, beta_o, carry_ref, wbf_ref):
    tm = x_ref.shape[1]

    @pl.when((pl.program_id(0) == 0) & (pl.program_id(1) == 0))
    def _():
        _regroup_w_in(win_ref, wbf_ref)

    x = x_ref[0]
    ms = jnp.mean(x * x, axis=-1, keepdims=True)
    hb = (x * lax.rsqrt(ms + RMS_EPS) * gain_ref[...]).astype(BF16)

    def wide(i):
        return _dot(hb, wbf_ref[:, _WIDE_OFF[i]:_WIDE_OFF[i + 1]])

    segm = segm_ref[...]
    q = wide(0)
    q = q * lax.rsqrt(_seg_sum(q * q, segm) * (1.0 / HEAD_DIM) + RMS_EPS) * qg_ref[...]
    fq_o[0] = q.astype(BF16).T
    k = wide(1)
    k = k * lax.rsqrt(_seg_sum(k * k, segm) * (1.0 / HEAD_DIM) + RMS_EPS) * kg_ref[...]
    fv_o[0] = wide(2).astype(BF16).T
    gqkv_o[0] = wide(3).astype(BF16)
    ggate_o[0] = wide(4).astype(BF16)
    lq_o[0] = wide(5).astype(BF16)
    lk_o[0] = wide(6).astype(BF16)
    lv_o[0] = wide(7).astype(BF16)
    lr_o[0] = wide(8).astype(BF16)

    sm = _dot(hb, wbf_ref[:, _WIDE_OFF[-1]:_WIDE_OFF[-1] + LANES])
    la_logit = _dot_split(sm, wa2_ref[...]) + ba_ref[...]
    la_o[0] = _log_sigmoid(la_logit) * (1.0 / GLA_TAU)

    smt = sm.T
    log_f = _log_sigmoid(smt[_SM_F:_SM_F + 8] + fb_ref[...])

    @pl.when(pl.program_id(1) == 0)
    def _():
        carry_ref[...] = jnp.zeros_like(carry_ref)

    cum = _lane_cumsum(log_f, tm) + carry_ref[:, 0:1]
    carry_ref[...] = jnp.broadcast_to(cum[:, tm - 1:tm], carry_ref.shape)
    pieces = []
    rem = cum * _LOG2E
    for _ in range(_C_PIECES):
        piece = rem.astype(BF16).astype(F32)
        pieces.append(piece)
        rem = rem - piece
    zero_row = jnp.zeros((1, tm), F32)
    rows = []
    for h in range(H_FOX):
        rows += [piece[h:h + 1] for piece in pieces] + [zero_row]
    rows.append(jnp.zeros((LANES - 4 * H_FOX, tm), F32))
    ptw = jnp.concatenate(rows, axis=0).T
    lane = lax.broadcasted_iota(jnp.int32, (tm, LANES), 1)
    for h in range(H_FOX):
        kp = k[:, (h // 2) * LANES:(h // 2 + 1) * LANES]
        if h % 2:
            kp = pltpu.roll(kp, HEAD_DIM, 1)
        cp = pltpu.roll(ptw, HEAD_DIM - 4 * h, 1)
        ka = jnp.where(lane < HEAD_DIM, kp, jnp.where(lane < HEAD_DIM + 4, cp, 0.0))
        ka_o[0, h] = ka.astype(BF16)
    g_o[0] = -jnp.exp(alog_ref[...]) * _softplus(smt[_SM_A:_SM_A + 8] + dtb_ref[...])
    beta_o[0] = _sigmoid(smt[_SM_B:_SM_B + 8])


def _proj_call(x, gain, w_in_all, layer, segm, qg, kg, fb, alog, dtb, wa2, ba, tm):
    b, s, d = x.shape
    tm = min(tm, s)
    const = lambda shape: pl.BlockSpec(shape, lambda i, j: (0,) * len(shape),
                                       pipeline_mode=pl.Buffered(1))
    tok = lambda w: pl.BlockSpec((1, tm, w), lambda i, j: (i, j, 0))
    row = pl.BlockSpec((1, 8, tm), lambda i, j: (i, 0, j))
    widths = [_SIZES[g] for g in _WIDE]
    out_shape = ([jax.ShapeDtypeStruct((b, s, w), BF16) for w in widths]
                 + [jax.ShapeDtypeStruct((b, s, LANES), F32)]
                 + [jax.ShapeDtypeStruct((b, 8, s), F32)] * 2)
    out_specs = [tok(w) for w in widths] + [tok(LANES)] + [row] * 2
    for i in (0, 2):
        out_shape[i] = jax.ShapeDtypeStruct((b, widths[i], s), BF16)
        out_specs[i] = pl.BlockSpec((1, widths[i], tm), lambda i, j: (i, 0, j))
    out_shape[1] = jax.ShapeDtypeStruct((b, H_FOX, s, LANES), BF16)
    out_specs[1] = pl.BlockSpec((1, H_FOX, tm, LANES), lambda i, j: (i, 0, j, 0))
    return pl.pallas_call(
        _proj_kernel,
        grid=(b, s // tm),
        in_specs=[tok(d), const((1, d)),
                  pl.BlockSpec((1,) + w_in_all.shape[1:], lambda i, j: (layer, 0, 0),
                               pipeline_mode=pl.Buffered(1)),
                  const(segm.shape),
                  const(qg.shape), const(kg.shape), const((8, 1)), const((8, 1)), const((8, 1)),
                  const(wa2.shape), const(ba.shape)],
        out_specs=out_specs,
        out_shape=out_shape,
        scratch_shapes=[pltpu.VMEM((8, LANES), F32),
                        pltpu.VMEM((d, _WIDE_OFF[-1] + LANES), BF16)],
        compiler_params=pltpu.CompilerParams(
            dimension_semantics=("arbitrary", "arbitrary"), vmem_limit_bytes=VMEM_LIMIT),
        name="proj",
    )(x, gain, w_in_all, segm, qg, kg, fb, alog, dtb, wa2, ba)


_ACC_ROWS = HEAD_DIM + 16


def _fox_kernel(qt_ref, k_ref, vt_ref, aug_ref, ones_ref, o_ref, m_ref, acc_ref):
    tq = qt_ref.shape[2]
    qi = pl.program_id(1)
    heads = range(H_FOX)
    aug = aug_ref[...]
    qts = [jnp.concatenate([qt_ref[0, h * HEAD_DIM:(h + 1) * HEAD_DIM, :], aug], axis=0)
           for h in heads]
    m_ref[...] = jnp.full_like(m_ref, -jnp.inf)
    acc_ref[...] = jnp.zeros_like(acc_ref)
    krow = lax.broadcasted_iota(jnp.int32, (tq, tq), 0)
    qcol = lax.broadcasted_iota(jnp.int32, (tq, tq), 1)
    ones = ones_ref[...]

    def step(blocks, masked):
        k0s = [pl.multiple_of(j * tq, tq) for j in blocks]
        ss = [[_dot(k_ref[0, h, pl.ds(k0, tq), :], qts[h]) for k0 in k0s]
              for h in heads]
        if masked:
            ss = [[jnp.where(krow <= qcol, s, -jnp.inf) for s in sh] for sh in ss]
        m_prev = [m_ref[h] for h in heads]
        m_new = []
        for h in heads:
            m = m_prev[h]
            for s in ss[h]:
                m = jnp.maximum(m, jnp.max(s, axis=0, keepdims=True))
            m_new.append(m)
        ps = [[jnp.exp2(s - m_new[h]).astype(BF16) for s in ss[h]] for h in heads]
        for h in heads:
            acc = jnp.exp2(m_prev[h] - m_new[h]) * acc_ref[h]
            for k0, p in zip(k0s, ps[h]):
                vta = jnp.concatenate(
                    [vt_ref[0, h * HEAD_DIM:(h + 1) * HEAD_DIM, pl.ds(k0, tq)], ones], axis=0)
                acc = acc + _dot(vta, p)
            acc_ref[h] = acc
            m_ref[h] = m_new[h]

    step([qi], True)

    def body(j2, carry):
        step([2 * j2, 2 * j2 + 1], False)
        return carry

    lax.fori_loop(0, qi // 2, body, 0)

    @pl.when(qi % 2 == 1)
    def _():
        step([qi - 1], False)
    outs = []
    for h in heads:
        acc = acc_ref[h]
        outs.append(acc[0:HEAD_DIM] / acc[HEAD_DIM:HEAD_DIM + 1])
    o_ref[0] = jnp.concatenate(outs, axis=0).T.astype(o_ref.dtype)


def _fox_call(qt, ka, vt, tq):
    b, _, s = qt.shape
    tq = min(tq, s)
    aug = jnp.broadcast_to(jnp.where(jnp.arange(HEAD_DIM)[:, None] < _C_PIECES, -1.0, 0.0),
                           (HEAD_DIM, tq)).astype(BF16)
    ones = jnp.broadcast_to(jnp.where(jnp.arange(16)[:, None] == 0, 1.0, 0.0), (16, tq)).astype(BF16)
    return pl.pallas_call(
        _fox_kernel,
        grid=(b, s // tq),
        in_specs=[pl.BlockSpec((1, FOX_DIM, tq), lambda i, j: (i, 0, j)),
                  pl.BlockSpec((1, H_FOX, s, LANES), lambda i, j: (i, 0, 0, 0)),
                  pl.BlockSpec((1, FOX_DIM, s), lambda i, j: (i, 0, 0)),
                  pl.BlockSpec(aug.shape, lambda i, j: (0, 0)),
                  pl.BlockSpec(ones.shape, lambda i, j: (0, 0))],
        out_specs=pl.BlockSpec((1, tq, FOX_DIM), lambda i, j: (i, j, 0)),
        out_shape=jax.ShapeDtypeStruct((b, s, FOX_DIM), BF16),
        scratch_shapes=[pltpu.VMEM((H_FOX, 1, tq), F32),
                        pltpu.VMEM((H_FOX, _ACC_ROWS, tq), F32)],
        compiler_params=pltpu.CompilerParams(
            dimension_semantics=("arbitrary", "arbitrary"),
            vmem_limit_bytes=VMEM_LIMIT),
        name="fox",
    )(qt, ka, vt, aug, ones)


def _bmm(a, b):
    return jnp.einsum('gmk,gkn->gmn', a.astype(BF16), b.astype(BF16),
                      preferred_element_type=F32)


def _pair_diag(x):
    xb = x.astype(BF16)
    low = lax.broadcasted_iota(jnp.int32, (1,) + xb.shape[1:], 2) < HEAD_DIM
    zero = jnp.zeros_like(xb)
    return jnp.concatenate([jnp.where(low, xb, zero), jnp.where(low, zero, xb)], axis=1)


def _unit_lower_inverse(a_strict, eye, between):
    n = -a_strict
    t = eye + n
    nd = _pair_diag(n)
    size, step = 1, 0
    while 2 * size < CHUNK:
        n = _bmm(n, nd)
        between(step)
        nd = _pair_diag(n)
        t = t + _bmm(t, nd)
        size *= 2
        step += 1
    return t


def _gdn_reset(state_ref, xext_ref):
    state_ref[...] = jnp.zeros_like(state_ref)
    xext_ref[0:SUBLANES, :] = jnp.zeros((SUBLANES, xext_ref.shape[1]), F32)


def _gdn_body(qkv_ref, convw_ref, g_ref, beta_ref, segm_ref, o_ref, state_ref, xext_ref, between):
    tc = qkv_ref.shape[1]
    pad = SUBLANES
    x = qkv_ref[0].astype(F32)
    xext_ref[pad:pad + tc, :] = x
    y = convw_ref[0:1, :] * xext_ref[pad - 3:pad - 3 + tc, :]
    for i in range(1, CONV_K):
        y = y + convw_ref[i:i + 1, :] * xext_ref[pad - 3 + i:pad - 3 + i + tc, :]
    xext_ref[0:pad, :] = x[tc - pad:tc, :]
    y = y * _sigmoid(y)

    segm = segm_ref[...]
    q = y[:, 0:GDN_DIM]
    k = y[:, GDN_DIM:2 * GDN_DIM]
    v = y[:, 2 * GDN_DIM:3 * GDN_DIM]
    q = q * lax.rsqrt(_seg_sum(q * q, segm) + RMS_EPS) * (HEAD_DIM ** -0.5)
    k = k * lax.rsqrt(_seg_sum(k * k, segm) + RMS_EPS)

    kt = k.T

    dec_row = _lane_cumsum(g_ref[0], CHUNK)
    dec_col = dec_row.T
    beta_col = beta_ref[0].T

    head_of_lane = lax.broadcasted_iota(jnp.int32, (tc, GDN_DIM), 1) // HEAD_DIM
    dexp = jnp.zeros((tc, GDN_DIM), F32)
    bexp = jnp.zeros((tc, GDN_DIM), F32)
    for h in range(H_GDN):
        dexp = jnp.where(head_of_lane == h, dec_col[:, h:h + 1], dexp)
        bexp = jnp.where(head_of_lane == h, beta_col[:, h:h + 1], bexp)
    edec = jnp.exp(dexp)
    kb = k * bexp
    vb = v * bexp
    kbe = kb * edec
    qd = q * edec

    nc = tc // CHUNK
    npair = H_GDN // 2
    index = [(c, p) for c in range(nc) for p in range(npair)]

    def split(a):
        return jnp.stack([a[c * CHUNK:(c + 1) * CHUNK, p * LANES:(p + 1) * LANES]
                          for c, p in index])

    q3, kb3, vb3, kbe3, qd3, dcol = (split(a) for a in (q, kb, vb, kbe, qd, dexp))
    drow = jnp.stack([jnp.concatenate(
        [dec_row[2 * p + hh:2 * p + hh + 1, c * CHUNK:(c + 1) * CHUNK] for hh in range(2)], axis=1)
        for c, p in index])
    dlast = dcol[:, CHUNK - 1:CHUNK, :]

    low = lax.broadcasted_iota(jnp.int32, (CHUNK, LANES), 1) < HEAD_DIM
    kdt_list, kd_list = [], []
    for c, p in index:
        tile = kt[p * LANES:(p + 1) * LANES, (c // 2) * LANES:(c // 2 + 1) * LANES]
        swapped = pltpu.roll(tile, HEAD_DIM, 1)
        top = (tile if c % 2 == 0 else swapped)[0:HEAD_DIM]
        bot = (swapped if c % 2 == 0 else tile)[HEAD_DIM:]
        kdt_list.append(jnp.where(low, top, bot))
        kd_list.append(jnp.concatenate([jnp.where(low, top, 0.0), jnp.where(low, 0.0, bot)],
                                       axis=0).astype(BF16))
    kt3 = jnp.stack(kdt_list)
    ktd = jnp.stack(kd_list)

    ri = lax.broadcasted_iota(jnp.int32, (1, CHUNK, LANES), 1)
    cj = lax.broadcasted_iota(jnp.int32, (1, CHUNK, LANES), 2) & (HEAD_DIM - 1)
    causal = cj <= ri
    eye = (cj == ri).astype(F32)
    gamma = jnp.where(causal, jnp.exp(jnp.where(causal, dcol - drow, 0.0)), 0.0)
    a = jnp.where(cj < ri, _bmm(kb3, ktd) * gamma, 0.0)
    t = _unit_lower_inverse(a, eye, between)
    u3 = _bmm(t, _pair_diag(vb3))
    w3 = _bmm(t, _pair_diag(kbe3))
    intra = _bmm(q3, ktd) * gamma
    kdt = kt3 * jnp.exp(dlast - drow)
    elast = jnp.exp(dlast)

    s = state_ref[...]
    for c in range(nc):
        sl = slice(c * npair, (c + 1) * npair)
        sd = _pair_diag(s)
        v_new = u3[sl] - _bmm(w3[sl], sd)
        vd = _pair_diag(v_new)
        o = _bmm(qd3[sl], sd) + _bmm(intra[sl], vd)
        s = s * elast[sl] + _bmm(kdt[sl], vd)
        o_ref[0, c * CHUNK:(c + 1) * CHUNK, :] = jnp.concatenate(
            [o[p] for p in range(npair)], axis=1).astype(o_ref.dtype)
    state_ref[...] = s


def _gla_chunks(chunks, q_ref, k_ref, v_ref, la_ref, tri_ref, mh_ref, mt_ref, o_ref,
                st_ref, kf_all, bq_all, vf_all):
    row = lax.broadcasted_iota(jnp.int32, (CHUNK, GLA_K_DIM), 0)
    mh = mh_ref[...]
    for c in chunks:
        kf_ref, bq_ref, vf_ref = kf_all.at[c], bq_all.at[c], vf_all.at[c]
        r0, r1 = c * CHUNK, (c + 1) * CHUNK
        bq = _dot_exact_lhs(tri_ref[...], la_ref[0, r0:r1, :])
        q = q_ref[0, r0:r1, :].astype(F32) * (GLA_DK ** -0.5)
        k = k_ref[0, r0:r1, :].astype(F32)
        v = v_ref[0, r0:r1, :].astype(F32)
        kf_ref[...] = k
        bq_ref[...] = bq
        vf_ref[...] = v
        st = st_ref[...]
        o_inter = _dot_nt((q * jnp.exp(bq)).astype(BF16), st.astype(BF16))

        group_out = []
        for g0 in range(0, CHUNK, SUBLANES):
            n = CHUNK - g0
            qg, bg, rg = q[g0:, :], bq[g0:, :], row[g0:, :]
            es = []
            for j in range(g0, g0 + SUBLANES):
                kj = kf_ref[j:j + 1, :]
                bj = bq_ref[j:j + 1, :]
                es.append(jnp.where(rg >= j, jnp.exp(bg - bj), 0.0) * (qg * kj))
            p = _dot(jnp.concatenate(es, axis=0).astype(BF16), mh)
            acc = p[0:n] * vf_ref[g0:g0 + 1, :]
            for jj in range(1, SUBLANES):
                acc = acc + p[jj * n:(jj + 1) * n] * vf_ref[g0 + jj:g0 + jj + 1, :]
            group_out.append(acc)
        pieces = []
        for r in range(0, CHUNK, SUBLANES):
            piece = o_inter[r:r + SUBLANES]
            for gi, g0 in enumerate(range(0, r + SUBLANES, SUBLANES)):
                piece = piece + group_out[gi][r - g0:r - g0 + SUBLANES]
            pieces.append(piece)
        o_ref[0, r0:r1, :] = jnp.concatenate(pieces, axis=0).astype(o_ref.dtype)
        blast = bq[CHUNK - 1:CHUNK, :]
        kd = k * jnp.exp(blast - bq)
        upd = _dot_tn(v.astype(BF16), kd.astype(BF16))
        st_ref[...] = (st * jnp.exp(blast) + upd) * mt_ref[...]


_N_GDN_IN, _N_GLA_IN = 5, 7


def _recurrent_kernel(*refs):
    gdn_in = refs[:_N_GDN_IN]
    gla_in = refs[_N_GDN_IN:_N_GDN_IN + _N_GLA_IN]
    o_gdn, o_gla = refs[_N_GDN_IN + _N_GLA_IN:_N_GDN_IN + _N_GLA_IN + 2]
    state_ref, xext_ref, st_ref, kf_ref, bq_ref, vf_ref = refs[_N_GDN_IN + _N_GLA_IN + 2:]

    @pl.when(pl.program_id(1) == 0)
    def _():
        _gdn_reset(state_ref, xext_ref)
        st_ref[...] = jnp.zeros_like(st_ref)

    n_chunks = o_gla.shape[1] // CHUNK
    done = []

    def gla_chunks(step):
        if step < n_chunks:
            _gla_chunks([step], *gla_in, o_gla, st_ref, kf_ref, bq_ref, vf_ref)
            done.append(step)

    _gdn_body(*gdn_in, o_gdn, state_ref, xext_ref, gla_chunks)
    _gla_chunks([c for c in range(n_chunks) if c not in done], *gla_in, o_gla,
                st_ref, kf_ref, bq_ref, vf_ref)


def _recurrent_call(qkv, convw, g, beta, segm, q, k, v, la, tc):
    b, s, w = qkv.shape
    tc = min(tc, s)
    tri = (jnp.arange(CHUNK)[:, None] >= jnp.arange(CHUNK)[None, :]).astype(F32)
    hk = jnp.arange(GLA_K_DIM) // GLA_DK
    hv = jnp.arange(GLA_V_DIM) // HEAD_DIM
    mh = (hk[:, None] == hv[None, :]).astype(BF16)
    mt = (hv[:, None] == hk[None, :]).astype(F32)
    tok = lambda wd: pl.BlockSpec((1, tc, wd), lambda i, j: (i, j, 0))
    row = pl.BlockSpec((1, 8, tc), lambda i, j: (i, 0, j))
    const = lambda a: pl.BlockSpec(a.shape, lambda i, j: (0, 0))
    return pl.pallas_call(
        _recurrent_kernel,
        grid=(b, s // tc),
        in_specs=[tok(w), const(convw), row, row, const(segm),
                  tok(GLA_K_DIM), tok(GLA_K_DIM), tok(GLA_V_DIM), tok(GLA_K_DIM),
                  const(tri), const(mh), const(mt)],
        out_specs=[tok(GDN_DIM), tok(GLA_V_DIM)],
        out_shape=[jax.ShapeDtypeStruct((b, s, GDN_DIM), BF16),
                   jax.ShapeDtypeStruct((b, s, GLA_V_DIM), BF16)],
        scratch_shapes=[pltpu.VMEM((H_GDN // 2, HEAD_DIM, LANES), F32),
                        pltpu.VMEM((tc + SUBLANES, w), F32),
                        pltpu.VMEM((GLA_V_DIM, GLA_K_DIM), F32),
                        pltpu.VMEM((tc // CHUNK, CHUNK, GLA_K_DIM), F32),
                        pltpu.VMEM((tc // CHUNK, CHUNK, GLA_K_DIM), F32),
                        pltpu.VMEM((tc // CHUNK, CHUNK, GLA_V_DIM), F32)],
        compiler_params=pltpu.CompilerParams(
            dimension_semantics=("arbitrary", "arbitrary"), vmem_limit_bytes=VMEM_LIMIT),
        name="gdn_gla",
    )(qkv, convw, g, beta, segm, q, k, v, la, tri, mh, mt)


def _head_norm(o_ref, segm, gain):
    o = o_ref[...].astype(F32)
    return o * lax.rsqrt(_seg_sum(o * o, segm) * (1.0 / HEAD_DIM) + RMS_EPS) * gain


def _silu(x):
    return x * _sigmoid(x)


def _out_kernel(x_ref, ofox_ref, ogdn_ref, ggate_ref, ogla_ref, lr_ref, wout_ref, segm_ref,
                gf_ref, gg_ref, gl_ref, fgain_ref, wr_ref, br_ref, tri_ref,
                x1_o, h2_o, meta_o, cnt_o, carry_ref):
    tm = x_ref.shape[0]
    segm = segm_ref[...]
    a = _head_norm(ofox_ref, segm, gf_ref[...])
    bb = _head_norm(ogdn_ref, segm, gg_ref[...]) * _silu(ggate_ref[...].astype(F32))
    cc = (_head_norm(ogla_ref, segm[:GLA_V_DIM, :GLA_V_DIM], gl_ref[...])
          * _silu(lr_ref[...].astype(F32)))
    y = (x_ref[...]
         + _dot(a.astype(BF16), wout_ref[0:FOX_DIM, :].astype(BF16))
         + _dot(bb.astype(BF16), wout_ref[FOX_DIM:FOX_DIM + GDN_DIM, :].astype(BF16))
         + _dot(cc.astype(BF16), wout_ref[FOX_DIM + GDN_DIM:, :].astype(BF16)))
    x1_o[...] = y
    ms = jnp.mean(y * y, axis=-1, keepdims=True)
    h2 = y * lax.rsqrt(ms + RMS_EPS) * fgain_ref[...]
    _to_token_tiles(h2_o, h2)

    wr_hi, wr_lo = _split(wr_ref[...])
    h2_hi, h2_lo = _split(h2)
    hi_terms = _dot(h2_hi, jnp.concatenate([wr_hi, wr_lo], axis=1))
    logits = (hi_terms[:, :LANES] + (hi_terms[:, LANES:] + _dot(h2_lo, wr_hi))
              + br_ref[...])
    lane = lax.broadcasted_iota(jnp.int32, logits.shape, 1).astype(F32)
    big = float(4 * LANES)
    ninf = -jnp.inf
    gl = jnp.where(lane < _R_GROUP + N_GROUPS, logits, ninf)
    gmax = jnp.max(gl, axis=1, keepdims=True)
    group_p = 1.0 / jnp.sum(jnp.exp(gl - gmax), axis=1, keepdims=True)
    gidx = jnp.min(jnp.where(gl == gmax, lane, big), axis=1, keepdims=True)
    elane = lane - _R_EXPERT
    group_of_lane = jnp.floor(elane * (1.0 / EXPERTS_PER_GROUP))
    in_group = (elane >= 0) & (elane < N_EXPERTS) & (group_of_lane == gidx)
    el = jnp.where(in_group, logits, ninf)
    m1 = jnp.max(el, axis=1, keepdims=True)
    i1 = jnp.min(jnp.where(el == m1, lane, big), axis=1, keepdims=True)
    el2 = jnp.where(lane == i1, ninf, el)
    m2 = jnp.max(el2, axis=1, keepdims=True)
    i2 = jnp.min(jnp.where(el2 == m2, lane, big), axis=1, keepdims=True)
    t = jnp.exp(m2 - m1)
    g1 = group_p / (1.0 + t)
    g2 = group_p * t / (1.0 + t)

    @pl.when(pl.program_id(0) == 0)
    def _():
        carry_ref[...] = jnp.zeros_like(carry_ref)

    sel = jnp.where((lane == i1) | (lane == i2), 1.0, 0.0)
    carry = carry_ref[0:1, :]
    rank = _dot(tri_ref[...], sel.astype(BF16)) + carry
    rank1 = jnp.sum(jnp.where(lane == i1, rank, 0.0), axis=1, keepdims=True)
    rank2 = jnp.sum(jnp.where(lane == i2, rank, 0.0), axis=1, keepdims=True)
    new_carry = carry + jnp.sum(sel, axis=0, keepdims=True)
    carry_ref[...] = jnp.broadcast_to(new_carry, carry_ref.shape)
    cnt_o[...] = jnp.broadcast_to(new_carry, cnt_o.shape)
    cols = [i1 - _R_EXPERT, i2 - _R_EXPERT, rank1, rank2, g1, g2]
    meta = jnp.zeros(logits.shape, F32)
    for idx, col in enumerate(cols):
        meta = jnp.where(lane == idx, col, meta)
    meta_o[...] = meta


def _out_call(x, ofox, ogdn, ggate, ogla, lr, wout, segm, gf, gg, gl, fgain, wr, br, tm):
    n, d = x.shape
    tm = min(tm, n)
    tri = (jnp.arange(tm)[:, None] > jnp.arange(tm)[None, :]).astype(BF16)
    tok = lambda w: pl.BlockSpec((tm, w), lambda i: (i, 0))
    const = lambda a: pl.BlockSpec(a.shape, lambda i: (0,) * a.ndim,
                                   pipeline_mode=pl.Buffered(1))
    return pl.pallas_call(
        _out_kernel,
        grid=(n // tm,),
        in_specs=[tok(d), tok(FOX_DIM), tok(GDN_DIM), tok(GDN_DIM), tok(GLA_V_DIM), tok(GLA_V_DIM),
                  const(wout), const(segm), const(gf), const(gg), const(gl), const(fgain),
                  const(wr), const(br), const(tri)],
        out_specs=[tok(d), pl.BlockSpec((tm * SUBLANES, LANES), lambda i: (i, 0)), tok(LANES),
                   pl.BlockSpec((8, LANES), lambda i: (0, 0))],
        out_shape=[jax.ShapeDtypeStruct((n, d), F32),
                   jax.ShapeDtypeStruct((n * SUBLANES, LANES), F32),
                   jax.ShapeDtypeStruct((n, LANES), F32), jax.ShapeDtypeStruct((8, LANES), F32)],
        scratch_shapes=[pltpu.VMEM((8, LANES), F32)],
        compiler_params=pltpu.CompilerParams(
            dimension_semantics=("arbitrary",), vmem_limit_bytes=VMEM_LIMIT),
        name="out_router",
    )(x, ofox, ogdn, ggate, ogla, lr, wout, segm, gf, gg, gl, fgain, wr, br, tri)


_DMA_UNROLL = 8


def _tile_rows(r):
    return pl.ds(pl.multiple_of(r * SUBLANES, SUBLANES), SUBLANES)


def _to_token_tiles(ref, x):
    t = x.shape[0]
    for s in range(SUBLANES):
        ref[pl.ds(s, t, stride=SUBLANES), :] = x[:, s * LANES:(s + 1) * LANES]


def _from_token_tiles(ref, t):
    return [ref[pl.ds(s, t, stride=SUBLANES), :] for s in range(SUBLANES)]


def _dispatch_kernel(dest_ref, h_ref, xb_in_ref, xb_ref, sem):
    del xb_in_ref
    td = h_ref.shape[0] // SUBLANES

    def row_copy(t, d):
        return pltpu.make_async_copy(h_ref.at[_tile_rows(t), :], xb_ref.at[_tile_rows(d), :], sem)

    def issue(t, carry):
        for kk in range(TOP_K):
            row_copy(t, dest_ref[0, 0, TOP_K * t + kk]).start(priority=kk % 2)
        return carry

    lax.fori_loop(0, td, issue, 0, unroll=_DMA_UNROLL)

    def drain(t, carry):
        for kk in range(TOP_K):
            row_copy(0, 0).wait()
        return carry

    lax.fori_loop(0, td, drain, 0, unroll=_DMA_UNROLL)


def _dispatch_call(dest, h2, n_rows, td):
    n = h2.shape[0] // SUBLANES
    td = min(td, n)
    dest3 = dest.reshape(n // td, 1, TOP_K * td)
    xb0 = jnp.zeros((n_rows * SUBLANES, LANES), h2.dtype)
    return pl.pallas_call(
        _dispatch_kernel,
        grid=(n // td,),
        in_specs=[pl.BlockSpec((1, 1, TOP_K * td), lambda i: (i, 0, 0), memory_space=pltpu.SMEM),
                  pl.BlockSpec((td * SUBLANES, LANES), lambda i: (i, 0)),
                  pl.BlockSpec(memory_space=pl.ANY)],
        out_specs=pl.BlockSpec(memory_space=pl.ANY),
        out_shape=jax.ShapeDtypeStruct((n_rows * SUBLANES, LANES), h2.dtype),
        scratch_shapes=[pltpu.SemaphoreType.DMA(())],
        input_output_aliases={2: 0},
        compiler_params=pltpu.CompilerParams(
            dimension_semantics=("arbitrary",), has_side_effects=True),
        name="dispatch",
    )(dest3, h2, xb0)


def _expert_kernel(be_ref, nu_ref, x_ref, wg_ref, wu_ref, wd_ref, y_ref, wgb_ref, wub_ref, wdb_ref):
    i = pl.program_id(0)
    used = i < nu_ref[0]

    @pl.when(used & ((i == 0) | (be_ref[i] != be_ref[jnp.maximum(i - 1, 0)])))
    def _():
        wgb_ref[...] = wg_ref[0].astype(BF16)
        wub_ref[...] = wu_ref[0].astype(BF16)
        wdb_ref[...] = wd_ref[0].astype(BF16)

    @pl.when(used)
    def _():
        tmb = x_ref.shape[0] // SUBLANES
        x = jnp.concatenate([blk.astype(BF16) for blk in _from_token_tiles(x_ref, tmb)], axis=1)
        a = _dot(x, wgb_ref[...])
        u = _dot(x, wub_ref[...])
        hmid = (_silu(a) * u).astype(BF16)
        _to_token_tiles(y_ref, _dot(hmid, wdb_ref[...]))

    @pl.when(jnp.logical_not(used))
    def _():
        y_ref[...] = jnp.zeros_like(y_ref)


def _expert_call(block_e, n_used, xb, wg, wu, wd, tmb):
    n_rows = xb.shape[0] // SUBLANES
    d = SUBLANES * LANES
    de = wg.shape[-1]
    n_blocks = n_rows // tmb

    def xmap(i, be, nu):
        return (jnp.minimum(i, jnp.maximum(nu[0] - 1, 0)), 0)

    wmap = lambda i, be, nu: (be[i], 0, 0)
    return pl.pallas_call(
        _expert_kernel,
        grid_spec=pltpu.PrefetchScalarGridSpec(
            num_scalar_prefetch=2,
            grid=(n_blocks,),
            in_specs=[pl.BlockSpec((tmb * SUBLANES, LANES), xmap),
                      pl.BlockSpec((1, d, de), wmap),
                      pl.BlockSpec((1, d, de), wmap),
                      pl.BlockSpec((1, de, d), wmap)],
            out_specs=pl.BlockSpec((tmb * SUBLANES, LANES), lambda i, be, nu: (i, 0)),
            scratch_shapes=[pltpu.VMEM((d, de), BF16), pltpu.VMEM((d, de), BF16),
                            pltpu.VMEM((de, d), BF16)],
        ),
        out_shape=jax.ShapeDtypeStruct((n_rows * SUBLANES, LANES), F32),
        compiler_params=pltpu.CompilerParams(
            dimension_semantics=("arbitrary",), vmem_limit_bytes=VMEM_LIMIT),
        name="experts",
    )(block_e, n_used, xb, wg, wu, wd)


def _combine_kernel(dest_ref, x1_ref, meta_ref, yb_ref, o_ref, buf_ref, sem):
    td = x1_ref.shape[0]

    def row_copy(t, kk, d):
        return pltpu.make_async_copy(yb_ref.at[_tile_rows(d), :], buf_ref.at[kk, _tile_rows(t), :],
                                     sem)

    def issue(t, carry):
        for kk in range(TOP_K):
            row_copy(t, kk, dest_ref[0, 0, TOP_K * t + kk]).start(priority=kk % 2)
        return carry

    lax.fori_loop(0, td, issue, 0, unroll=_DMA_UNROLL)

    def drain(t, carry):
        for kk in range(TOP_K):
            row_copy(0, kk, 0).wait()
        return carry

    lax.fori_loop(0, td, drain, 0, unroll=_DMA_UNROLL)
    meta = meta_ref[...]
    g1, g2 = meta[:, 4:5], meta[:, 5:6]
    y1 = _from_token_tiles(buf_ref.at[0], td)
    y2 = _from_token_tiles(buf_ref.at[1], td)
    for s in range(SUBLANES):
        lanes = slice(s * LANES, (s + 1) * LANES)
        o_ref[:, lanes] = x1_ref[:, lanes] + g1 * y1[s] + g2 * y2[s]


def _combine_call(dest, x1, meta, yb, td):
    n, d = x1.shape
    td = min(td, n)
    dest3 = dest.reshape(n // td, 1, TOP_K * td)
    return pl.pallas_call(
        _combine_kernel,
        grid=(n // td,),
        in_specs=[pl.BlockSpec((1, 1, TOP_K * td), lambda i: (i, 0, 0), memory_space=pltpu.SMEM),
                  pl.BlockSpec((td, d), lambda i: (i, 0)),
                  pl.BlockSpec((td, LANES), lambda i: (i, 0)),
                  pl.BlockSpec(memory_space=pl.ANY)],
        out_specs=pl.BlockSpec((td, d), lambda i: (i, 0)),
        out_shape=jax.ShapeDtypeStruct((n, d), F32),
        scratch_shapes=[pltpu.VMEM((TOP_K, td * SUBLANES, LANES), yb.dtype),
                        pltpu.SemaphoreType.DMA(())],
        compiler_params=pltpu.CompilerParams(
            dimension_semantics=("arbitrary",), vmem_limit_bytes=VMEM_LIMIT),
        name="combine",
    )(dest3, x1, meta, yb)


TM_PROJ = 512
TQ_FOX = 256
TC_RECURRENT = 256
TM_OUT = 512
TD_MOE = 256
TMB_EXPERT = 256


def _place(width, parts):
    cols, at = [], 0
    for pos, blk in parts:
        if pos > at:
            cols.append(jnp.zeros((blk.shape[0], pos - at), blk.dtype))
        cols.append(blk)
        at = pos + blk.shape[1]
    if width > at:
        cols.append(jnp.zeros((parts[0][1].shape[0], width - at), parts[0][1].dtype))
    return jnp.concatenate(cols, axis=1)


def _pad8(v):
    return jnp.zeros((8,), F32).at[:v.shape[0]].set(v.astype(F32)).reshape(8, 1)


def _token_mixer(x, attn_norm, w_in_all, layer, fox_q_norm, fox_k_norm, fox_f_bias,
                 gdn_conv, gdn_a_log, gdn_dt_bias, gla_w_a2, gla_b_a):
    b, s, d = x.shape
    wa2 = jnp.pad(gla_w_a2, ((_SM_A1, LANES - _SM_A1 - GLA_RANK), (0, 0)))
    segm = _seg_matrix(FOX_DIM)
    qg = (jnp.tile(fox_q_norm, H_FOX) * (HEAD_DIM ** -0.5 * _LOG2E)).reshape(1, FOX_DIM)
    kg = jnp.tile(fox_k_norm, H_FOX).reshape(1, FOX_DIM)
    outs = _proj_call(x, attn_norm.reshape(1, d), w_in_all, layer, segm, qg, kg,
                      _pad8(fox_f_bias), _pad8(gdn_a_log), _pad8(gdn_dt_bias),
                      wa2, gla_b_a.reshape(1, GLA_K_DIM), TM_PROJ)
    fq, fka, fv, gqkv, ggate, lq, lk, lv, lr, la, g, beta = outs
    o_fox = _fox_call(fq, fka, fv, TQ_FOX)
    o_gdn, o_gla = _recurrent_call(gqkv, gdn_conv.astype(F32), g, beta, segm,
                                   lq, lk, lv, la, TC_RECURRENT)
    return o_fox, o_gdn, ggate, o_gla, lr


def _layer(x, p, layer, w_in_all, experts):
    b, s, d = x.shape
    n = b * s
    o_fox, o_gdn, ggate, o_gla, lr = _token_mixer(
        x, p['attn_norm'], w_in_all, layer, p['fox_q_norm'], p['fox_k_norm'], p['fox_f_bias'],
        p['gdn_conv'], p['gdn_a_log'], p['gdn_dt_bias'], p['gla_w_a2'], p['gla_b_a'])
    wr = _place(LANES, [(_R_GROUP, p['w_router_group']), (_R_EXPERT, p['w_router_expert'])])
    br = _place(LANES, [(_R_GROUP, p['b_router_group'].reshape(1, -1)),
                        (_R_EXPERT, p['b_router_expert'].reshape(1, -1))])
    flat = lambda a: a.reshape(n, a.shape[-1])
    x1, h2, meta, cnt = _out_call(
        flat(x), flat(o_fox), flat(o_gdn), flat(ggate), flat(o_gla), flat(lr),
        p['w_out'], _seg_matrix(FOX_DIM),
        jnp.tile(p['fox_o_norm'], H_FOX).reshape(1, FOX_DIM),
        jnp.tile(p['gdn_o_norm'], H_GDN).reshape(1, GDN_DIM),
        jnp.tile(p['gla_o_norm'], H_GLA).reshape(1, GLA_V_DIM),
        p['ffn_norm'].reshape(1, d), wr, br, TM_OUT)

    tmb = TMB_EXPERT
    counts = cnt[0, _R_EXPERT:_R_EXPERT + N_EXPERTS].astype(jnp.int32)
    padded = (counts + tmb - 1) // tmb * tmb
    pends = jnp.cumsum(padded)
    pstarts = pends - padded
    eid = meta[:, 0:TOP_K].astype(jnp.int32)
    rank = meta[:, TOP_K:2 * TOP_K].astype(jnp.int32)
    expert_ids = jnp.arange(N_EXPERTS, dtype=jnp.int32)
    start_of = jnp.sum(jnp.where(eid[..., None] == expert_ids, pstarts, 0), axis=-1)
    dest = (start_of + rank).reshape(-1)
    n_blocks = -(-(n * TOP_K) // tmb) + N_EXPERTS
    block_start = jnp.arange(n_blocks, dtype=jnp.int32) * tmb
    block_e = jnp.minimum(jnp.sum(pends[None, :] <= block_start[:, None], axis=1),
                          N_EXPERTS - 1).astype(jnp.int32)
    n_used = (pends[-1:] // tmb).astype(jnp.int32)

    xb = _dispatch_call(dest, h2, n_blocks * tmb, TD_MOE)
    yb = _expert_call(block_e + layer * N_EXPERTS, n_used, xb, *experts, tmb)
    x2 = _combine_call(dest, x1, meta, yb, TD_MOE)
    return x2.reshape(b, s, d)


_PARAM_NAMES = ['attn_norm', 'w_in', 'fox_q_norm', 'fox_k_norm', 'fox_f_bias', 'fox_o_norm',
                'gdn_conv', 'gdn_a_log', 'gdn_dt_bias', 'gdn_o_norm',
                'gla_w_a2', 'gla_b_a', 'gla_o_norm', 'w_out',
                'ffn_norm', 'w_router_group', 'b_router_group', 'w_router_expert',
                'b_router_expert', 'w_expert_gate', 'w_expert_up', 'w_expert_down']


def kernel(x, attn_norm, w_in, fox_q_norm, fox_k_norm, fox_f_bias, fox_o_norm, gdn_conv, gdn_a_log, gdn_dt_bias, gdn_o_norm, gla_w_a2, gla_b_a, gla_o_norm, w_out, ffn_norm, w_router_group, b_router_group, w_router_expert, b_router_expert, w_expert_gate, w_expert_up, w_expert_down):
    params = dict(zip(_PARAM_NAMES, (
        attn_norm, w_in, fox_q_norm, fox_k_norm, fox_f_bias, fox_o_norm, gdn_conv, gdn_a_log,
        gdn_dt_bias, gdn_o_norm, gla_w_a2, gla_b_a, gla_o_norm, w_out, ffn_norm,
        w_router_group, b_router_group, w_router_expert, b_router_expert,
        w_expert_gate, w_expert_up, w_expert_down)))
    experts = tuple(params.pop(name).reshape((-1,) + params_shape[2:])
                    for name, params_shape in (('w_expert_gate', w_expert_gate.shape),
                                               ('w_expert_up', w_expert_up.shape),
                                               ('w_expert_down', w_expert_down.shape)))
    del params['w_in']
    for layer in range(attn_norm.shape[0]):
        x = _layer(x, {name: val[layer] for name, val in params.items()}, layer, w_in, experts)
    return x
```

```python
import functools

import jax
import jax.numpy as jnp
from jax import lax
from jax.experimental import pallas as pl
from jax.experimental.pallas import tpu as pltpu

F32 = jnp.float32
BF16 = jnp.bfloat16

HEAD_DIM = 64
H_FOX = 6
H_GDN = 6
H_GLA = 4
FOX_DIM = H_FOX * HEAD_DIM
GDN_DIM = H_GDN * HEAD_DIM
GLA_DK = 32
GLA_K_DIM = H_GLA * GLA_DK
GLA_V_DIM = H_GLA * HEAD_DIM
GLA_RANK = 16
GLA_TAU = 16.0
CONV_K = 4
CHUNK = 64
N_GROUPS = 4
EXPERTS_PER_GROUP = 8
N_EXPERTS = N_GROUPS * EXPERTS_PER_GROUP
TOP_K = 2
RMS_EPS = 1e-6

_LOG2E = 1.4426950408889634
_C_PIECES = 3

LANES = 128
SUBLANES = 8
VMEM_LIMIT = 56 * 1024 * 1024

_SIZES = [FOX_DIM, FOX_DIM, FOX_DIM, H_FOX, 3 * GDN_DIM, GDN_DIM, H_GDN, H_GDN,
          GLA_K_DIM, GLA_K_DIM, GLA_V_DIM, GLA_V_DIM, GLA_RANK]
_OFFS = [sum(_SIZES[:i]) for i in range(len(_SIZES) + 1)]
_WIDE = [0, 1, 2, 4, 5, 8, 9, 10, 11]
_WIDE_OFF = [0]
for _g in _WIDE:
    _WIDE_OFF.append(_WIDE_OFF[-1] + _SIZES[_g])
_SM_F, _SM_A, _SM_B, _SM_A1 = 0, 8, 16, 32
_R_GROUP, _R_EXPERT = 0, 32


def _dot(a, b):
    return jnp.dot(a, b, preferred_element_type=F32)


def _dot_nt(a, b):
    return lax.dot_general(a, b, (((1,), (1,)), ((), ())), preferred_element_type=F32)


def _dot_tn(a, b):
    return lax.dot_general(a, b, (((0,), (0,)), ((), ())), preferred_element_type=F32)


def _split(a):
    hi = a.astype(BF16)
    return hi, (a - hi.astype(F32)).astype(BF16)


def _dot_split(a, b):
    a_hi, a_lo = _split(a)
    b_hi, b_lo = _split(b)
    return _dot(a_hi, b_hi) + (_dot(a_hi, b_lo) + _dot(a_lo, b_hi))


def _dot_exact_lhs(a01, b):
    a16 = a01.astype(BF16)
    b_hi = b.astype(BF16)
    rem = b - b_hi.astype(F32)
    b_mid = rem.astype(BF16)
    b_lo = (rem - b_mid.astype(F32)).astype(BF16)
    return _dot(a16, b_hi) + (_dot(a16, b_mid) + _dot(a16, b_lo))


def _seg_sum(sq, segm):
    return _dot(sq.astype(BF16), segm)


def _sigmoid(x):
    return 1.0 / (1.0 + jnp.exp(-x))


def _softplus(x):
    return jnp.maximum(x, 0.0) + jnp.log1p(jnp.exp(-jnp.abs(x)))


def _log_sigmoid(x):
    return -_softplus(-x)


def _lane_cumsum(x, seg):
    lane = lax.broadcasted_iota(jnp.int32, x.shape, 1)
    pos = lane & (seg - 1)
    s = 1
    while s < seg:
        x = x + jnp.where(pos >= s, pltpu.roll(x, s, 1), 0.0)
        s *= 2
    return x


def _seg_matrix(n):
    i = jnp.arange(n) // HEAD_DIM
    return (i[:, None] == i[None, :]).astype(BF16)


_W_ROWS = 128


def _regroup_w_in(win_ref, wbf_ref):
    def body(r, carry):
        rows = pl.ds(pl.multiple_of(r * _W_ROWS, _W_ROWS), _W_ROWS)
        for i, g in enumerate(_WIDE):
            wbf_ref[rows, _WIDE_OFF[i]:_WIDE_OFF[i + 1]] = (
                win_ref[0, rows, _OFFS[g]:_OFFS[g + 1]].astype(BF16))
        small = _WIDE_OFF[-1]
        wbf_ref[rows, small:small + LANES] = jnp.zeros((_W_ROWS, LANES), BF16)
        for pos, g in ((_SM_F, 3), (_SM_A, 6), (_SM_B, 7), (_SM_A1, 12)):
            wbf_ref[rows, small + pos:small + pos + _SIZES[g]] = (
                win_ref[0, rows, _OFFS[g]:_OFFS[g + 1]].astype(BF16))
        return carry

    lax.fori_loop(0, win_ref.shape[1] // _W_ROWS, body, 0)


def _proj_kernel(x_ref, gain_ref, win_ref, segm_ref, qg_ref, kg_ref,
                 fb_ref, alog_ref, dtb_ref, wa2_ref, ba_ref,
                 fq_o, ka_o, fv_o, gqkv_o, ggate_o, lq_o, lk_o, lv_o, lr_o, la_o,
                 g_o, beta_o, carry_ref, wbf_ref):
    tm = x_ref.shape[1]

    @pl.when((pl.program_id(0) == 0) & (pl.program_id(1) == 0))
    def _():
        _regroup_w_in(win_ref, wbf_ref)

    x = x_ref[0]
    ms = jnp.mean(x * x, axis=-1, keepdims=True)
    hb = (x * lax.rsqrt(ms + RMS_EPS) * gain_ref[...]).astype(BF16)

    def wide(i):
        return _dot(hb, wbf_ref[:, _WIDE_OFF[i]:_WIDE_OFF[i + 1]])

    segm = segm_ref[...]
    q = wide(0)
    q = q * lax.rsqrt(_seg_sum(q * q, segm) * (1.0 / HEAD_DIM) + RMS_EPS) * qg_ref[...]
    fq_o[0] = q.astype(BF16).T
    k = wide(1)
    k = k * lax.rsqrt(_seg_sum(k * k, segm) * (1.0 / HEAD_DIM) + RMS_EPS) * kg_ref[...]
    fv_o[0] = wide(2).astype(BF16).T
    gqkv_o[0] = wide(3).astype(BF16)
    ggate_o[0] = wide(4).astype(BF16)
    lq_o[0] = wide(5).astype(BF16)
    lk_o[0] = wide(6).astype(BF16)
    lv_o[0] = wide(7).astype(BF16)
    lr_o[0] = wide(8).astype(BF16)

    sm = _dot(hb, wbf_ref[:, _WIDE_OFF[-1]:_WIDE_OFF[-1] + LANES])
    la_logit = _dot_split(sm, wa2_ref[...]) + ba_ref[...]
    la_o[0] = _log_sigmoid(la_logit) * (1.0 / GLA_TAU)

    smt = sm.T
    log_f = _log_sigmoid(smt[_SM_F:_SM_F + 8] + fb_ref[...])

    @pl.when(pl.program_id(1) == 0)
    def _():
        carry_ref[...] = jnp.zeros_like(carry_ref)

    cum = _lane_cumsum(log_f, tm) + carry_ref[:, 0:1]
    carry_ref[...] = jnp.broadcast_to(cum[:, tm - 1:tm], carry_ref.shape)
    pieces = []
    rem = cum * _LOG2E
    for _ in range(_C_PIECES):
        piece = rem.astype(BF16).astype(F32)
        pieces.append(piece)
        rem = rem - piece
    zero_row = jnp.zeros((1, tm), F32)
    rows = []
    for h in range(H_FOX):
        rows += [piece[h:h + 1] for piece in pieces] + [zero_row]
    rows.append(jnp.zeros((LANES - 4 * H_FOX, tm), F32))
    ptw = jnp.concatenate(rows, axis=0).T
    lane = lax.broadcasted_iota(jnp.int32, (tm, LANES), 1)
    for h in range(H_FOX):
        kp = k[:, (h // 2) * LANES:(h // 2 + 1) * LANES]
        if h % 2:
            kp = pltpu.roll(kp, HEAD_DIM, 1)
        cp = pltpu.roll(ptw, HEAD_DIM - 4 * h, 1)
        ka = jnp.where(lane < HEAD_DIM, kp, jnp.where(lane < HEAD_DIM + 4, cp, 0.0))
        ka_o[0, h] = ka.astype(BF16)
    g_o[0] = -jnp.exp(alog_ref[...]) * _softplus(smt[_SM_A:_SM_A + 8] + dtb_ref[...])
    beta_o[0] = _sigmoid(smt[_SM_B:_SM_B + 8])


def _proj_call(x, gain, w_in_all, layer, segm, qg, kg, fb, alog, dtb, wa2, ba, tm):
    b, s, d = x.shape
    tm = min(tm, s)
    const = lambda shape: pl.BlockSpec(shape, lambda i, j: (0,) * len(shape),
                                       pipeline_mode=pl.Buffered(1))
    tok = lambda w: pl.BlockSpec((1, tm, w), lambda i, j: (i, j, 0))
    row = pl.BlockSpec((1, 8, tm), lambda i, j: (i, 0, j))
    widths = [_SIZES[g] for g in _WIDE]
    out_shape = ([jax.ShapeDtypeStruct((b, s, w), BF16) for w in widths]
                 + [jax.ShapeDtypeStruct((b, s, LANES), F32)]
                 + [jax.ShapeDtypeStruct((b, 8, s), F32)] * 2)
    out_specs = [tok(w) for w in widths] + [tok(LANES)] + [row] * 2
    for i in (0, 2):
        out_shape[i] = jax.ShapeDtypeStruct((b, widths[i], s), BF16)
        out_specs[i] = pl.BlockSpec((1, widths[i], tm), lambda i, j: (i, 0, j))
    out_shape[1] = jax.ShapeDtypeStruct((b, H_FOX, s, LANES), BF16)
    out_specs[1] = pl.BlockSpec((1, H_FOX, tm, LANES), lambda i, j: (i, 0, j, 0))
    return pl.pallas_call(
        _proj_kernel,
        grid=(b, s // tm),
        in_specs=[tok(d), const((1, d)),
                  pl.BlockSpec((1,) + w_in_all.shape[1:], lambda i, j: (layer, 0, 0),
                               pipeline_mode=pl.Buffered(1)),
                  const(segm.shape),
                  const(qg.shape), const(kg.shape), const((8, 1)), const((8, 1)), const((8, 1)),
                  const(wa2.shape), const(ba.shape)],
        out_specs=out_specs,
        out_shape=out_shape,
        scratch_shapes=[pltpu.VMEM((8, LANES), F32),
                        pltpu.VMEM((d, _WIDE_OFF[-1] + LANES), BF16)],
        compiler_params=pltpu.CompilerParams(
            dimension_semantics=("arbitrary", "arbitrary"), vmem_limit_bytes=VMEM_LIMIT),
        name="proj",
    )(x, gain, w_in_all, segm, qg, kg, fb, alog, dtb, wa2, ba)


_ACC_ROWS = HEAD_DIM + 16


def _fox_kernel(qt_ref, k_ref, vt_ref, aug_ref, ones_ref, o_ref, m_ref, acc_ref):
    tq = qt_ref.shape[2]
    qi = pl.program_id(1)
    heads = range(H_FOX)
    aug = aug_ref[...]
    qts = [jnp.concatenate([qt_ref[0, h * HEAD_DIM:(h + 1) * HEAD_DIM, :], aug], axis=0)
           for h in heads]
    m_ref[...] = jnp.full_like(m_ref, -jnp.inf)
    acc_ref[...] = jnp.zeros_like(acc_ref)
    krow = lax.broadcasted_iota(jnp.int32, (tq, tq), 0)
    qcol = lax.broadcasted_iota(jnp.int32, (tq, tq), 1)
    ones = ones_ref[...]

    def step(blocks, masked):
        k0s = [pl.multiple_of(j * tq, tq) for j in blocks]
        ss = [[_dot(k_ref[0, h, pl.ds(k0, tq), :], qts[h]) for k0 in k0s]
              for h in heads]
        if masked:
            ss = [[jnp.where(krow <= qcol, s, -jnp.inf) for s in sh] for sh in ss]
        m_prev = [m_ref[h] for h in heads]
        m_new = []
        for h in heads:
            m = m_prev[h]
            for s in ss[h]:
                m = jnp.maximum(m, jnp.max(s, axis=0, keepdims=True))
            m_new.append(m)
        ps = [[jnp.exp2(s - m_new[h]).astype(BF16) for s in ss[h]] for h in heads]
        for h in heads:
            acc = jnp.exp2(m_prev[h] - m_new[h]) * acc_ref[h]
            for k0, p in zip(k0s, ps[h]):
                vta = jnp.concatenate(
                    [vt_ref[0, h * HEAD_DIM:(h + 1) * HEAD_DIM, pl.ds(k0, tq)], ones], axis=0)
                acc = acc + _dot(vta, p)
            acc_ref[h] = acc
            m_ref[h] = m_new[h]

    step([qi], True)

    def body(j2, carry):
        step([2 * j2, 2 * j2 + 1], False)
        return carry

    lax.fori_loop(0, qi // 2, body, 0)

    @pl.when(qi % 2 == 1)
    def _():
        step([qi - 1], False)
    outs = []
    for h in heads:
        acc = acc_ref[h]
        outs.append(acc[0:HEAD_DIM] / acc[HEAD_DIM:HEAD_DIM + 1])
    o_ref[0] = jnp.concatenate(outs, axis=0).T.astype(o_ref.dtype)


def _fox_call(qt, ka, vt, tq):
    b, _, s = qt.shape
    tq = min(tq, s)
    aug = jnp.broadcast_to(jnp.where(jnp.arange(HEAD_DIM)[:, None] < _C_PIECES, -1.0, 0.0),
                           (HEAD_DIM, tq)).astype(BF16)
    ones = jnp.broadcast_to(jnp.where(jnp.arange(16)[:, None] == 0, 1.0, 0.0), (16, tq)).astype(BF16)
    return pl.pallas_call(
        _fox_kernel,
        grid=(b, s // tq),
        in_specs=[pl.BlockSpec((1, FOX_DIM, tq), lambda i, j: (i, 0, j)),
                  pl.BlockSpec((1, H_FOX, s, LANES), lambda i, j: (i, 0, 0, 0)),
                  pl.BlockSpec((1, FOX_DIM, s), lambda i, j: (i, 0, 0)),
                  pl.BlockSpec(aug.shape, lambda i, j: (0, 0)),
                  pl.BlockSpec(ones.shape, lambda i, j: (0, 0))],
        out_specs=pl.BlockSpec((1, tq, FOX_DIM), lambda i, j: (i, j, 0)),
        out_shape=jax.ShapeDtypeStruct((b, s, FOX_DIM), BF16),
        scratch_shapes=[pltpu.VMEM((H_FOX, 1, tq), F32),
                        pltpu.VMEM((H_FOX, _ACC_ROWS, tq), F32)],
        compiler_params=pltpu.CompilerParams(
            dimension_semantics=("arbitrary", "arbitrary"),
            vmem_limit_bytes=VMEM_LIMIT),
        name="fox",
    )(qt, ka, vt, aug, ones)


def _bmm(a, b):
    return jnp.einsum('gmk,gkn->gmn', a.astype(BF16), b.astype(BF16),
                      preferred_element_type=F32)


def _pair_diag(x):
    xb = x.astype(BF16)
    low = lax.broadcasted_iota(jnp.int32, (1,) + xb.shape[1:], 2) < HEAD_DIM
    zero = jnp.zeros_like(xb)
    return jnp.concatenate([jnp.where(low, xb, zero), jnp.where(low, zero, xb)], axis=1)


def _unit_lower_inverse(a_strict, eye, between):
    n = -a_strict
    t = eye + n
    nd = _pair_diag(n)
    size, step = 1, 0
    while 2 * size < CHUNK:
        n = _bmm(n, nd)
        between(step)
        nd = _pair_diag(n)
        t = t + _bmm(t, nd)
        size *= 2
        step += 1
    return t


def _gdn_reset(state_ref, xext_ref):
    state_ref[...] = jnp.zeros_like(state_ref)
    xext_ref[0:SUBLANES, :] = jnp.zeros((SUBLANES, xext_ref.shape[1]), F32)


def _gdn_body(qkv_ref, convw_ref, g_ref, beta_ref, segm_ref, o_ref, state_ref, xext_ref, between):
    tc = qkv_ref.shape[1]
    pad = SUBLANES
    x = qkv_ref[0].astype(F32)
    xext_ref[pad:pad + tc, :] = x
    y = convw_ref[0:1, :] * xext_ref[pad - 3:pad - 3 + tc, :]
    for i in range(1, CONV_K):
        y = y + convw_ref[i:i + 1, :] * xext_ref[pad - 3 + i:pad - 3 + i + tc, :]
    xext_ref[0:pad, :] = x[tc - pad:tc, :]
    y = y * _sigmoid(y)

    segm = segm_ref[...]
    q = y[:, 0:GDN_DIM]
    k = y[:, GDN_DIM:2 * GDN_DIM]
    v = y[:, 2 * GDN_DIM:3 * GDN_DIM]
    q = q * lax.rsqrt(_seg_sum(q * q, segm) + RMS_EPS) * (HEAD_DIM ** -0.5)
    k = k * lax.rsqrt(_seg_sum(k * k, segm) + RMS_EPS)

    kt = k.T

    dec_row = _lane_cumsum(g_ref[0], CHUNK)
    dec_col = dec_row.T
    beta_col = beta_ref[0].T

    head_of_lane = lax.broadcasted_iota(jnp.int32, (tc, GDN_DIM), 1) // HEAD_DIM
    dexp = jnp.zeros((tc, GDN_DIM), F32)
    bexp = jnp.zeros((tc, GDN_DIM), F32)
    for h in range(H_GDN):
        dexp = jnp.where(head_of_lane == h, dec_col[:, h:h + 1], dexp)
        bexp = jnp.where(head_of_lane == h, beta_col[:, h:h + 1], bexp)
    edec = jnp.exp(dexp)
    kb = k * bexp
    vb = v * bexp
    kbe = kb * edec
    qd = q * edec

    nc = tc // CHUNK
    npair = H_GDN // 2
    index = [(c, p) for c in range(nc) for p in range(npair)]

    def split(a):
        return jnp.stack([a[c * CHUNK:(c + 1) * CHUNK, p * LANES:(p + 1) * LANES]
                          for c, p in index])

    q3, kb3, vb3, kbe3, qd3, dcol = (split(a) for a in (q, kb, vb, kbe, qd, dexp))
    drow = jnp.stack([jnp.concatenate(
        [dec_row[2 * p + hh:2 * p + hh + 1, c * CHUNK:(c + 1) * CHUNK] for hh in range(2)], axis=1)
        for c, p in index])
    dlast = dcol[:, CHUNK - 1:CHUNK, :]

    low = lax.broadcasted_iota(jnp.int32, (CHUNK, LANES), 1) < HEAD_DIM
    kdt_list, kd_list = [], []
    for c, p in index:
        tile = kt[p * LANES:(p + 1) * LANES, (c // 2) * LANES:(c // 2 + 1) * LANES]
        swapped = pltpu.roll(tile, HEAD_DIM, 1)
        top = (tile if c % 2 == 0 else swapped)[0:HEAD_DIM]
        bot = (swapped if c % 2 == 0 else tile)[HEAD_DIM:]
        kdt_list.append(jnp.where(low, top, bot))
        kd_list.append(jnp.concatenate([jnp.where(low, top, 0.0), jnp.where(low, 0.0, bot)],
                                       axis=0).astype(BF16))
    kt3 = jnp.stack(kdt_list)
    ktd = jnp.stack(kd_list)

    ri = lax.broadcasted_iota(jnp.int32, (1, CHUNK, LANES), 1)
    cj = lax.broadcasted_iota(jnp.int32, (1, CHUNK, LANES), 2) & (HEAD_DIM - 1)
    causal = cj <= ri
    eye = (cj == ri).astype(F32)
    gamma = jnp.where(causal, jnp.exp(jnp.where(causal, dcol - drow, 0.0)), 0.0)
    a = jnp.where(cj < ri, _bmm(kb3, ktd) * gamma, 0.0)
    t = _unit_lower_inverse(a, eye, between)
    u3 = _bmm(t, _pair_diag(vb3))
    w3 = _bmm(t, _pair_diag(kbe3))
    intra = _bmm(q3, ktd) * gamma
    kdt = kt3 * jnp.exp(dlast - drow)
    elast = jnp.exp(dlast)

    s = state_ref[...]
    for c in range(nc):
        sl = slice(c * npair, (c + 1) * npair)
        sd = _pair_diag(s)
        v_new = u3[sl] - _bmm(w3[sl], sd)
        vd = _pair_diag(v_new)
        o = _bmm(qd3[sl], sd) + _bmm(intra[sl], vd)
        s = s * elast[sl] + _bmm(kdt[sl], vd)
        o_ref[0, c * CHUNK:(c + 1) * CHUNK, :] = jnp.concatenate(
            [o[p] for p in range(npair)], axis=1).astype(o_ref.dtype)
    state_ref[...] = s


def _gla_chunks(chunks, q_ref, k_ref, v_ref, la_ref, tri_ref, mh_ref, mt_ref, o_ref,
                st_ref, kf_all, bq_all, vf_all):
    row = lax.broadcasted_iota(jnp.int32, (CHUNK, GLA_K_DIM), 0)
    mh = mh_ref[...]
    for c in chunks:
        kf_ref, bq_ref, vf_ref = kf_all.at[c], bq_all.at[c], vf_all.at[c]
        r0, r1 = c * CHUNK, (c + 1) * CHUNK
        bq = _dot_exact_lhs(tri_ref[...], la_ref[0, r0:r1, :])
        q = q_ref[0, r0:r1, :].astype(F32) * (GLA_DK ** -0.5)
        k = k_ref[0, r0:r1, :].astype(F32)
        v = v_ref[0, r0:r1, :].astype(F32)
        kf_ref[...] = k
        bq_ref[...] = bq
        vf_ref[...] = v
        st = st_ref[...]
        o_inter = _dot_nt((q * jnp.exp(bq)).astype(BF16), st.astype(BF16))

        group_out = []
        for g0 in range(0, CHUNK, SUBLANES):
            n = CHUNK - g0
            qg, bg, rg = q[g0:, :], bq[g0:, :], row[g0:, :]
            es = []
            for j in range(g0, g0 + SUBLANES):
                kj = kf_ref[j:j + 1, :]
                bj = bq_ref[j:j + 1, :]
                es.append(jnp.where(rg >= j, jnp.exp(bg - bj), 0.0) * (qg * kj))
            p = _dot(jnp.concatenate(es, axis=0).astype(BF16), mh)
            acc = p[0:n] * vf_ref[g0:g0 + 1, :]
            for jj in range(1, SUBLANES):
                acc = acc + p[jj * n:(jj + 1) * n] * vf_ref[g0 + jj:g0 + jj + 1, :]
            group_out.append(acc)
        pieces = []
        for r in range(0, CHUNK, SUBLANES):
            piece = o_inter[r:r + SUBLANES]
            for gi, g0 in enumerate(range(0, r + SUBLANES, SUBLANES)):
                piece = piece + group_out[gi][r - g0:r - g0 + SUBLANES]
            pieces.append(piece)
        o_ref[0, r0:r1, :] = jnp.concatenate(pieces, axis=0).astype(o_ref.dtype)
        blast = bq[CHUNK - 1:CHUNK, :]
        kd = k * jnp.exp(blast - bq)
        upd = _dot_tn(v.astype(BF16), kd.astype(BF16))
        st_ref[...] = (st * jnp.exp(blast) + upd) * mt_ref[...]


_N_GDN_IN, _N_GLA_IN = 5, 7


def _recurrent_kernel(*refs):
    gdn_in = refs[:_N_GDN_IN]
    gla_in = refs[_N_GDN_IN:_N_GDN_IN + _N_GLA_IN]
    o_gdn, o_gla = refs[_N_GDN_IN + _N_GLA_IN:_N_GDN_IN + _N_GLA_IN + 2]
    state_ref, xext_ref, st_ref, kf_ref, bq_ref, vf_ref = refs[_N_GDN_IN + _N_GLA_IN + 2:]

    @pl.when(pl.program_id(1) == 0)
    def _():
        _gdn_reset(state_ref, xext_ref)
        st_ref[...] = jnp.zeros_like(st_ref)

    n_chunks = o_gla.shape[1] // CHUNK
    done = []

    def gla_chunks(step):
        if step < n_chunks:
            _gla_chunks([step], *gla_in, o_gla, st_ref, kf_ref, bq_ref, vf_ref)
            done.append(step)

    _gdn_body(*gdn_in, o_gdn, state_ref, xext_ref, gla_chunks)
    _gla_chunks([c for c in range(n_chunks) if c not in done], *gla_in, o_gla,
                st_ref, kf_ref, bq_ref, vf_ref)


def _recurrent_call(qkv, convw, g, beta, segm, q, k, v, la, tc):
    b, s, w = qkv.shape
    tc = min(tc, s)
    tri = (jnp.arange(CHUNK)[:, None] >= jnp.arange(CHUNK)[None, :]).astype(F32)
    hk = jnp.arange(GLA_K_DIM) // GLA_DK
    hv = jnp.arange(GLA_V_DIM) // HEAD_DIM
    mh = (hk[:, None] == hv[None, :]).astype(BF16)
    mt = (hv[:, None] == hk[None, :]).astype(F32)
    tok = lambda wd: pl.BlockSpec((1, tc, wd), lambda i, j: (i, j, 0))
    row = pl.BlockSpec((1, 8, tc), lambda i, j: (i, 0, j))
    const = lambda a: pl.BlockSpec(a.shape, lambda i, j: (0, 0))
    return pl.pallas_call(
        _recurrent_kernel,
        grid=(b, s // tc),
        in_specs=[tok(w), const(convw), row, row, const(segm),
                  tok(GLA_K_DIM), tok(GLA_K_DIM), tok(GLA_V_DIM), tok(GLA_K_DIM),
                  const(tri), const(mh), const(mt)],
        out_specs=[tok(GDN_DIM), tok(GLA_V_DIM)],
        out_shape=[jax.ShapeDtypeStruct((b, s, GDN_DIM), BF16),
                   jax.ShapeDtypeStruct((b, s, GLA_V_DIM), BF16)],
        scratch_shapes=[pltpu.VMEM((H_GDN // 2, HEAD_DIM, LANES), F32),
                        pltpu.VMEM((tc + SUBLANES, w), F32),
                        pltpu.VMEM((GLA_V_DIM, GLA_K_DIM), F32),
                        pltpu.VMEM((tc // CHUNK, CHUNK, GLA_K_DIM), F32),
                        pltpu.VMEM((tc // CHUNK, CHUNK, GLA_K_DIM), F32),
                        pltpu.VMEM((tc // CHUNK, CHUNK, GLA_V_DIM), F32)],
        compiler_params=pltpu.CompilerParams(
            dimension_semantics=("arbitrary", "arbitrary"), vmem_limit_bytes=VMEM_LIMIT),
        name="gdn_gla",
    )(qkv, convw, g, beta, segm, q, k, v, la, tri, mh, mt)


def _head_norm(o_ref, segm, gain):
    o = o_ref[...].astype(F32)
    return o * lax.rsqrt(_seg_sum(o * o, segm) * (1.0 / HEAD_DIM) + RMS_EPS) * gain


def _silu(x):
    return x * _sigmoid(x)


def _out_kernel(x_ref, ofox_ref, ogdn_ref, ggate_ref, ogla_ref, lr_ref, wout_ref, segm_ref,
                gf_ref, gg_ref, gl_ref, fgain_ref, wr_ref, br_ref, tri_ref,
                x1_o, h2_o, meta_o, cnt_o, carry_ref):
    tm = x_ref.shape[0]
    segm = segm_ref[...]
    a = _head_norm(ofox_ref, segm, gf_ref[...])
    bb = _head_norm(ogdn_ref, segm, gg_ref[...]) * _silu(ggate_ref[...].astype(F32))
    cc = (_head_norm(ogla_ref, segm[:GLA_V_DIM, :GLA_V_DIM], gl_ref[...])
          * _silu(lr_ref[...].astype(F32)))
    y = (x_ref[...]
         + _dot(a.astype(BF16), wout_ref[0:FOX_DIM, :].astype(BF16))
         + _dot(bb.astype(BF16), wout_ref[FOX_DIM:FOX_DIM + GDN_DIM, :].astype(BF16))
         + _dot(cc.astype(BF16), wout_ref[FOX_DIM + GDN_DIM:, :].astype(BF16)))
    x1_o[...] = y
    ms = jnp.mean(y * y, axis=-1, keepdims=True)
    h2 = y * lax.rsqrt(ms + RMS_EPS) * fgain_ref[...]
    _to_token_tiles(h2_o, h2)

    wr_hi, wr_lo = _split(wr_ref[...])
    h2_hi, h2_lo = _split(h2)
    hi_terms = _dot(h2_hi, jnp.concatenate([wr_hi, wr_lo], axis=1))
    logits = (hi_terms[:, :LANES] + (hi_terms[:, LANES:] + _dot(h2_lo, wr_hi))
              + br_ref[...])
    lane = lax.broadcasted_iota(jnp.int32, logits.shape, 1).astype(F32)
    big = float(4 * LANES)
    ninf = -jnp.inf
    gl = jnp.where(lane < _R_GROUP + N_GROUPS, logits, ninf)
    gmax = jnp.max(gl, axis=1, keepdims=True)
    group_p = 1.0 / jnp.sum(jnp.exp(gl - gmax), axis=1, keepdims=True)
    gidx = jnp.min(jnp.where(gl == gmax, lane, big), axis=1, keepdims=True)
    elane = lane - _R_EXPERT
    group_of_lane = jnp.floor(elane * (1.0 / EXPERTS_PER_GROUP))
    in_group = (elane >= 0) & (elane < N_EXPERTS) & (group_of_lane == gidx)
    el = jnp.where(in_group, logits, ninf)
    m1 = jnp.max(el, axis=1, keepdims=True)
    i1 = jnp.min(jnp.where(el == m1, lane, big), axis=1, keepdims=True)
    el2 = jnp.where(lane == i1, ninf, el)
    m2 = jnp.max(el2, axis=1, keepdims=True)
    i2 = jnp.min(jnp.where(el2 == m2, lane, big), axis=1, keepdims=True)
    t = jnp.exp(m2 - m1)
    g1 = group_p / (1.0 + t)
    g2 = group_p * t / (1.0 + t)

    @pl.when(pl.program_id(0) == 0)
    def _():
        carry_ref[...] = jnp.zeros_like(carry_ref)

    sel = jnp.where((lane == i1) | (lane == i2), 1.0, 0.0)
    carry = carry_ref[0:1, :]
    rank = _dot(tri_ref[...], sel.astype(BF16)) + carry
    rank1 = jnp.sum(jnp.where(lane == i1, rank, 0.0), axis=1, keepdims=True)
    rank2 = jnp.sum(jnp.where(lane == i2, rank, 0.0), axis=1, keepdims=True)
    new_carry = carry + jnp.sum(sel, axis=0, keepdims=True)
    carry_ref[...] = jnp.broadcast_to(new_carry, carry_ref.shape)
    cnt_o[...] = jnp.broadcast_to(new_carry, cnt_o.shape)
    cols = [i1 - _R_EXPERT, i2 - _R_EXPERT, rank1, rank2, g1, g2]
    meta = jnp.zeros(logits.shape, F32)
    for idx, col in enumerate(cols):
        meta = jnp.where(lane == idx, col, meta)
    meta_o[...] = meta


def _out_call(x, ofox, ogdn, ggate, ogla, lr, wout, segm, gf, gg, gl, fgain, wr, br, tm):
    n, d = x.shape
    tm = min(tm, n)
    tri = (jnp.arange(tm)[:, None] > jnp.arange(tm)[None, :]).astype(BF16)
    tok = lambda w: pl.BlockSpec((tm, w), lambda i: (i, 0))
    const = lambda a: pl.BlockSpec(a.shape, lambda i: (0,) * a.ndim,
                                   pipeline_mode=pl.Buffered(1))
    return pl.pallas_call(
        _out_kernel,
        grid=(n // tm,),
        in_specs=[tok(d), tok(FOX_DIM), tok(GDN_DIM), tok(GDN_DIM), tok(GLA_V_DIM), tok(GLA_V_DIM),
                  const(wout), const(segm), const(gf), const(gg), const(gl), const(fgain),
                  const(wr), const(br), const(tri)],
        out_specs=[tok(d), pl.BlockSpec((tm * SUBLANES, LANES), lambda i: (i, 0)), tok(LANES),
                   pl.BlockSpec((8, LANES), lambda i: (0, 0))],
        out_shape=[jax.ShapeDtypeStruct((n, d), F32),
                   jax.ShapeDtypeStruct((n * SUBLANES, LANES), F32),
                   jax.ShapeDtypeStruct((n, LANES), F32), jax.ShapeDtypeStruct((8, LANES), F32)],
        scratch_shapes=[pltpu.VMEM((8, LANES), F32)],
        compiler_params=pltpu.CompilerParams(
            dimension_semantics=("arbitrary",), vmem_limit_bytes=VMEM_LIMIT),
        name="out_router",
    )(x, ofox, ogdn, ggate, ogla, lr, wout, segm, gf, gg, gl, fgain, wr, br, tri)


_DMA_UNROLL = 8


def _tile_rows(r):
    return pl.ds(pl.multiple_of(r * SUBLANES, SUBLANES), SUBLANES)


def _to_token_tiles(ref, x):
    t = x.shape[0]
    for s in range(SUBLANES):
        ref[pl.ds(s, t, stride=SUBLANES), :] = x[:, s * LANES:(s + 1) * LANES]


def _from_token_tiles(ref, t):
    return [ref[pl.ds(s, t, stride=SUBLANES), :] for s in range(SUBLANES)]


def _dispatch_kernel(zb_ref, dest_ref, h_ref, xb_ref, zero_ref, sem, zsem):
    td = h_ref.shape[0] // SUBLANES

    @pl.when(pl.program_id(0) == 0)
    def _():
        zero_ref[...] = jnp.zeros_like(zero_ref)
        rows = zero_ref.shape[0]

        def block_copy(b):
            return pltpu.make_async_copy(
                zero_ref, xb_ref.at[pl.ds(pl.multiple_of(b * rows, rows), rows), :], zsem)

        def start(b, carry):
            @pl.when(zb_ref[b] != 0)
            def _():
                block_copy(b).start()
            return carry

        def wait(b, carry):
            @pl.when(zb_ref[b] != 0)
            def _():
                block_copy(b).wait()
            return carry

        lax.fori_loop(0, zb_ref.shape[0], start, 0)
        lax.fori_loop(0, zb_ref.shape[0], wait, 0)

    def row_copy(t, d):
        return pltpu.make_async_copy(h_ref.at[_tile_rows(t), :], xb_ref.at[_tile_rows(d), :], sem)

    def issue(t, carry):
        for kk in range(TOP_K):
            row_copy(t, dest_ref[0, 0, TOP_K * t + kk]).start(priority=kk % 2)
        return carry

    lax.fori_loop(0, td, issue, 0, unroll=_DMA_UNROLL)

    def drain(t, carry):
        for kk in range(TOP_K):
            row_copy(0, 0).wait()
        return carry

    lax.fori_loop(0, td, drain, 0, unroll=_DMA_UNROLL)


def _dispatch_call(zero_block, dest, h2, tmb, td):
    n = h2.shape[0] // SUBLANES
    td = min(td, n)
    n_rows = zero_block.shape[0] * tmb
    dest3 = dest.reshape(n // td, 1, TOP_K * td)
    return pl.pallas_call(
        _dispatch_kernel,
        grid_spec=pltpu.PrefetchScalarGridSpec(
            num_scalar_prefetch=1,
            grid=(n // td,),
            in_specs=[pl.BlockSpec((1, 1, TOP_K * td), lambda i, zb: (i, 0, 0),
                                   memory_space=pltpu.SMEM),
                      pl.BlockSpec((td * SUBLANES, LANES), lambda i, zb: (i, 0))],
            out_specs=pl.BlockSpec(memory_space=pl.ANY),
            scratch_shapes=[pltpu.VMEM((tmb * SUBLANES, LANES), h2.dtype),
                            pltpu.SemaphoreType.DMA(()), pltpu.SemaphoreType.DMA(())],
        ),
        out_shape=jax.ShapeDtypeStruct((n_rows * SUBLANES, LANES), h2.dtype),
        compiler_params=pltpu.CompilerParams(
            dimension_semantics=("arbitrary",), has_side_effects=True),
        name="dispatch",
    )(zero_block, dest3, h2)


def _expert_kernel(be_ref, nu_ref, x_ref, wg_ref, wu_ref, wd_ref, y_ref, wgb_ref, wub_ref, wdb_ref):
    i = pl.program_id(0)
    used = i < nu_ref[0]

    @pl.when(used & ((i == 0) | (be_ref[i] != be_ref[jnp.maximum(i - 1, 0)])))
    def _():
        wgb_ref[...] = wg_ref[0].astype(BF16)
        wub_ref[...] = wu_ref[0].astype(BF16)
        wdb_ref[...] = wd_ref[0].astype(BF16)

    @pl.when(used)
    def _():
        tmb = x_ref.shape[0] // SUBLANES
        x = jnp.concatenate([blk.astype(BF16) for blk in _from_token_tiles(x_ref, tmb)], axis=1)
        a = _dot(x, wgb_ref[...])
        u = _dot(x, wub_ref[...])
        hmid = (_silu(a) * u).astype(BF16)
        _to_token_tiles(y_ref, _dot(hmid, wdb_ref[...]))

    @pl.when(jnp.logical_not(used))
    def _():
        y_ref[...] = jnp.zeros_like(y_ref)


def _expert_call(block_e, n_used, xb, wg, wu, wd, tmb):
    n_rows = xb.shape[0] // SUBLANES
    d = SUBLANES * LANES
    de = wg.shape[-1]
    n_blocks = n_rows // tmb

    def xmap(i, be, nu):
        return (jnp.minimum(i, jnp.maximum(nu[0] - 1, 0)), 0)

    wmap = lambda i, be, nu: (be[i], 0, 0)
    return pl.pallas_call(
        _expert_kernel,
        grid_spec=pltpu.PrefetchScalarGridSpec(
            num_scalar_prefetch=2,
            grid=(n_blocks,),
            in_specs=[pl.BlockSpec((tmb * SUBLANES, LANES), xmap),
                      pl.BlockSpec((1, d, de), wmap),
                      pl.BlockSpec((1, d, de), wmap),
                      pl.BlockSpec((1, de, d), wmap)],
            out_specs=pl.BlockSpec((tmb * SUBLANES, LANES), lambda i, be, nu: (i, 0)),
            scratch_shapes=[pltpu.VMEM((d, de), BF16), pltpu.VMEM((d, de), BF16),
                            pltpu.VMEM((de, d), BF16)],
        ),
        out_shape=jax.ShapeDtypeStruct((n_rows * SUBLANES, LANES), F32),
        compiler_params=pltpu.CompilerParams(
            dimension_semantics=("arbitrary",), vmem_limit_bytes=VMEM_LIMIT),
        name="experts",
    )(block_e, n_used, xb, wg, wu, wd)


def _combine_kernel(dest_ref, x1_ref, meta_ref, yb_ref, o_ref, buf_ref, sem):
    td = x1_ref.shape[0]

    def row_copy(t, kk, d):
        return pltpu.make_async_copy(yb_ref.at[_tile_rows(d), :], buf_ref.at[kk, _tile_rows(t), :],
                                     sem)

    def issue(t, carry):
        for kk in range(TOP_K):
            row_copy(t, kk, dest_ref[0, 0, TOP_K * t + kk]).start(priority=kk % 2)
        return carry

    lax.fori_loop(0, td, issue, 0, unroll=_DMA_UNROLL)

    def drain(t, carry):
        for kk in range(TOP_K):
            row_copy(0, kk, 0).wait()
        return carry

    lax.fori_loop(0, td, drain, 0, unroll=_DMA_UNROLL)
    meta = meta_ref[...]
    g1, g2 = meta[:, 4:5], meta[:, 5:6]
    y1 = _from_token_tiles(buf_ref.at[0], td)
    y2 = _from_token_tiles(buf_ref.at[1], td)
    for s in range(SUBLANES):
        lanes = slice(s * LANES, (s + 1) * LANES)
        o_ref[:, lanes] = x1_ref[:, lanes] + g1 * y1[s] + g2 * y2[s]


def _combine_call(dest, x1, meta, yb, td):
    n, d = x1.shape
    td = min(td, n)
    dest3 = dest.reshape(n // td, 1, TOP_K * td)
    return pl.pallas_call(
        _combine_kernel,
        grid=(n // td,),
        in_specs=[pl.BlockSpec((1, 1, TOP_K * td), lambda i: (i, 0, 0), memory_space=pltpu.SMEM),
                  pl.BlockSpec((td, d), lambda i: (i, 0)),
                  pl.BlockSpec((td, LANES), lambda i: (i, 0)),
                  pl.BlockSpec(memory_space=pl.ANY)],
        out_specs=pl.BlockSpec((td, d), lambda i: (i, 0)),
        out_shape=jax.ShapeDtypeStruct((n, d), F32),
        scratch_shapes=[pltpu.VMEM((TOP_K, td * SUBLANES, LANES), yb.dtype),
                        pltpu.SemaphoreType.DMA(())],
        compiler_params=pltpu.CompilerParams(
            dimension_semantics=("arbitrary",), vmem_limit_bytes=VMEM_LIMIT),
        name="combine",
    )(dest3, x1, meta, yb)


TM_PROJ = 512
TQ_FOX = 256
TC_RECURRENT = 256
TM_OUT = 512
TD_MOE = 256
TMB_EXPERT = 256


def _place(width, parts):
    cols, at = [], 0
    for pos, blk in parts:
        if pos > at:
            cols.append(jnp.zeros((blk.shape[0], pos - at), blk.dtype))
        cols.append(blk)
        at = pos + blk.shape[1]
    if width > at:
        cols.append(jnp.zeros((parts[0][1].shape[0], width - at), parts[0][1].dtype))
    return jnp.concatenate(cols, axis=1)


def _pad8(v):
    return jnp.zeros((8,), F32).at[:v.shape[0]].set(v.astype(F32)).reshape(8, 1)


def _token_mixer(x, attn_norm, w_in_all, layer, fox_q_norm, fox_k_norm, fox_f_bias,
                 gdn_conv, gdn_a_log, gdn_dt_bias, gla_w_a2, gla_b_a):
    b, s, d = x.shape
    wa2 = jnp.pad(gla_w_a2, ((_SM_A1, LANES - _SM_A1 - GLA_RANK), (0, 0)))
    segm = _seg_matrix(FOX_DIM)
    qg = (jnp.tile(fox_q_norm, H_FOX) * (HEAD_DIM ** -0.5 * _LOG2E)).reshape(1, FOX_DIM)
    kg = jnp.tile(fox_k_norm, H_FOX).reshape(1, FOX_DIM)
    outs = _proj_call(x, attn_norm.reshape(1, d), w_in_all, layer, segm, qg, kg,
                      _pad8(fox_f_bias), _pad8(gdn_a_log), _pad8(gdn_dt_bias),
                      wa2, gla_b_a.reshape(1, GLA_K_DIM), TM_PROJ)
    fq, fka, fv, gqkv, ggate, lq, lk, lv, lr, la, g, beta = outs
    o_fox = _fox_call(fq, fka, fv, TQ_FOX)
    o_gdn, o_gla = _recurrent_call(gqkv, gdn_conv.astype(F32), g, beta, segm,
                                   lq, lk, lv, la, TC_RECURRENT)
    return o_fox, o_gdn, ggate, o_gla, lr


def _layer(x, p, layer, w_in_all, experts):
    b, s, d = x.shape
    n = b * s
    o_fox, o_gdn, ggate, o_gla, lr = _token_mixer(
        x, p['attn_norm'], w_in_all, layer, p['fox_q_norm'], p['fox_k_norm'], p['fox_f_bias'],
        p['gdn_conv'], p['gdn_a_log'], p['gdn_dt_bias'], p['gla_w_a2'], p['gla_b_a'])
    wr = _place(LANES, [(_R_GROUP, p['w_router_group']), (_R_EXPERT, p['w_router_expert'])])
    br = _place(LANES, [(_R_GROUP, p['b_router_group'].reshape(1, -1)),
                        (_R_EXPERT, p['b_router_expert'].reshape(1, -1))])
    flat = lambda a: a.reshape(n, a.shape[-1])
    x1, h2, meta, cnt = _out_call(
        flat(x), flat(o_fox), flat(o_gdn), flat(ggate), flat(o_gla), flat(lr),
        p['w_out'], _seg_matrix(FOX_DIM),
        jnp.tile(p['fox_o_norm'], H_FOX).reshape(1, FOX_DIM),
        jnp.tile(p['gdn_o_norm'], H_GDN).reshape(1, GDN_DIM),
        jnp.tile(p['gla_o_norm'], H_GLA).reshape(1, GLA_V_DIM),
        p['ffn_norm'].reshape(1, d), wr, br, TM_OUT)

    tmb = TMB_EXPERT
    counts = cnt[0, _R_EXPERT:_R_EXPERT + N_EXPERTS].astype(jnp.int32)
    padded = (counts + tmb - 1) // tmb * tmb
    pends = jnp.cumsum(padded)
    pstarts = pends - padded
    eid = meta[:, 0:TOP_K].astype(jnp.int32)
    rank = meta[:, TOP_K:2 * TOP_K].astype(jnp.int32)
    expert_ids = jnp.arange(N_EXPERTS, dtype=jnp.int32)
    start_of = jnp.sum(jnp.where(eid[..., None] == expert_ids, pstarts, 0), axis=-1)
    dest = (start_of + rank).reshape(-1)
    n_blocks = -(-(n * TOP_K) // tmb) + N_EXPERTS
    block_start = jnp.arange(n_blocks, dtype=jnp.int32) * tmb
    block_e = jnp.minimum(jnp.sum(pends[None, :] <= block_start[:, None], axis=1),
                          N_EXPERTS - 1).astype(jnp.int32)
    n_used = (pends[-1:] // tmb).astype(jnp.int32)

    is_last = jnp.any((block_start + tmb)[:, None] == pends[None, :], axis=1)
    zero_block = (is_last | (block_start >= pends[-1])).astype(jnp.int32)
    xb = _dispatch_call(zero_block, dest, h2, tmb, TD_MOE)
    yb = _expert_call(block_e + layer * N_EXPERTS, n_used, xb, *experts, tmb)
    x2 = _combine_call(dest, x1, meta, yb, TD_MOE)
    return x2.reshape(b, s, d)


_PARAM_NAMES = ['attn_norm', 'w_in', 'fox_q_norm', 'fox_k_norm', 'fox_f_bias', 'fox_o_norm',
                'gdn_conv', 'gdn_a_log', 'gdn_dt_bias', 'gdn_o_norm',
                'gla_w_a2', 'gla_b_a', 'gla_o_norm', 'w_out',
                'ffn_norm', 'w_router_group', 'b_router_group', 'w_router_expert',
                'b_router_expert', 'w_expert_gate', 'w_expert_up', 'w_expert_down']


def kernel(x, attn_norm, w_in, fox_q_norm, fox_k_norm, fox_f_bias, fox_o_norm, gdn_conv, gdn_a_log, gdn_dt_bias, gdn_o_norm, gla_w_a2, gla_b_a, gla_o_norm, w_out, ffn_norm, w_router_group, b_router_group, w_router_expert, b_router_expert, w_expert_gate, w_expert_up, w_expert_down):
    params = dict(zip(_PARAM_NAMES, (
        attn_norm, w_in, fox_q_norm, fox_k_norm, fox_f_bias, fox_o_norm, gdn_conv, gdn_a_log,
        gdn_dt_bias, gdn_o_norm, gla_w_a2, gla_b_a, gla_o_norm, w_out, ffn_norm,
        w_router_group, b_router_group, w_router_expert, b_router_expert,
        w_expert_gate, w_expert_up, w_expert_down)))
    experts = tuple(params.pop(name).reshape((-1,) + params_shape[2:])
                    for name, params_shape in (('w_expert_gate', w_expert_gate.shape),
                                               ('w_expert_up', w_expert_up.shape),
                                               ('w_expert_down', w_expert_down.shape)))
    del params['w_in']
    for layer in range(attn_norm.shape[0]):
        x = _layer(x, {name: val[layer] for name, val in params.items()}, layer, w_in, experts)
    return x
```

```python
import functools

import jax
import jax.numpy as jnp
from jax import lax
from jax.experimental import pallas as pl
from jax.experimental.pallas import tpu as pltpu

F32 = jnp.float32
BF16 = jnp.bfloat16

HEAD_DIM = 64
H_FOX = 6
H_GDN = 6
H_GLA = 4
FOX_DIM = H_FOX * HEAD_DIM
GDN_DIM = H_GDN * HEAD_DIM
GLA_DK = 32
GLA_K_DIM = H_GLA * GLA_DK
GLA_V_DIM = H_GLA * HEAD_DIM
GLA_RANK = 16
GLA_TAU = 16.0
CONV_K = 4
CHUNK = 64
N_GROUPS = 4
EXPERTS_PER_GROUP = 8
N_EXPERTS = N_GROUPS * EXPERTS_PER_GROUP
TOP_K = 2
RMS_EPS = 1e-6

_LOG2E = 1.4426950408889634
_C_PIECES = 3

LANES = 128
SUBLANES = 8
VMEM_LIMIT = 56 * 1024 * 1024

_SIZES = [FOX_DIM, FOX_DIM, FOX_DIM, H_FOX, 3 * GDN_DIM, GDN_DIM, H_GDN, H_GDN,
          GLA_K_DIM, GLA_K_DIM, GLA_V_DIM, GLA_V_DIM, GLA_RANK]
_OFFS = [sum(_SIZES[:i]) for i in range(len(_SIZES) + 1)]
_WIDE = [0, 1, 2, 4, 5, 8, 9, 10, 11]
_WIDE_OFF = [0]
for _g in _WIDE:
    _WIDE_OFF.append(_WIDE_OFF[-1] + _SIZES[_g])
_SM_F, _SM_A, _SM_B, _SM_A1 = 0, 8, 16, 32
_R_GROUP, _R_EXPERT = 0, 32


def _dot(a, b):
    return jnp.dot(a, b, preferred_element_type=F32)


def _dot_nt(a, b):
    return lax.dot_general(a, b, (((1,), (1,)), ((), ())), preferred_element_type=F32)


def _dot_tn(a, b):
    return lax.dot_general(a, b, (((0,), (0,)), ((), ())), preferred_element_type=F32)


def _split(a):
    hi = a.astype(BF16)
    return hi, (a - hi.astype(F32)).astype(BF16)


def _dot_split(a, b):
    a_hi, a_lo = _split(a)
    b_hi, b_lo = _split(b)
    return _dot(a_hi, b_hi) + (_dot(a_hi, b_lo) + _dot(a_lo, b_hi))


def _dot_exact_lhs(a01, b):
    a16 = a01.astype(BF16)
    b_hi = b.astype(BF16)
    rem = b - b_hi.astype(F32)
    b_mid = rem.astype(BF16)
    b_lo = (rem - b_mid.astype(F32)).astype(BF16)
    return _dot(a16, b_hi) + (_dot(a16, b_mid) + _dot(a16, b_lo))


def _seg_sum(sq, segm):
    return _dot(sq.astype(BF16), segm)


def _sigmoid(x):
    return 1.0 / (1.0 + jnp.exp(-x))


def _softplus(x):
    return jnp.maximum(x, 0.0) + jnp.log1p(jnp.exp(-jnp.abs(x)))


def _log_sigmoid(x):
    return -_softplus(-x)


def _lane_cumsum(x, seg):
    lane = lax.broadcasted_iota(jnp.int32, x.shape, 1)
    pos = lane & (seg - 1)
    s = 1
    while s < seg:
        x = x + jnp.where(pos >= s, pltpu.roll(x, s, 1), 0.0)
        s *= 2
    return x


def _seg_matrix(n):
    i = jnp.arange(n) // HEAD_DIM
    return (i[:, None] == i[None, :]).astype(BF16)


_W_ROWS = 128


def _regroup_w_in(win_ref, wbf_ref):
    def body(r, carry):
        rows = pl.ds(pl.multiple_of(r * _W_ROWS, _W_ROWS), _W_ROWS)
        for i, g in enumerate(_WIDE):
            wbf_ref[rows, _WIDE_OFF[i]:_WIDE_OFF[i + 1]] = (
                win_ref[0, rows, _OFFS[g]:_OFFS[g + 1]].astype(BF16))
        small = _WIDE_OFF[-1]
        wbf_ref[rows, small:small + LANES] = jnp.zeros((_W_ROWS, LANES), BF16)
        for pos, g in ((_SM_F, 3), (_SM_A, 6), (_SM_B, 7), (_SM_A1, 12)):
            wbf_ref[rows, small + pos:small + pos + _SIZES[g]] = (
                win_ref[0, rows, _OFFS[g]:_OFFS[g + 1]].astype(BF16))
        return carry

    lax.fori_loop(0, win_ref.shape[1] // _W_ROWS, body, 0)


def _proj_kernel(x_ref, gain_ref, win_ref, segm_ref, qg_ref, kg_ref,
                 fb_ref, alog_ref, dtb_ref, wa2_ref, ba_ref,
                 fq_o, ka_o, fv_o, gqkv_o, ggate_o, lq_o, lk_o, lv_o, lr_o, la_o,
                 g_o, beta_o, carry_ref, wbf_ref):
    tm = x_ref.shape[1]

    @pl.when((pl.program_id(0) == 0) & (pl.program_id(1) == 0))
    def _():
        _regroup_w_in(win_ref, wbf_ref)

    x = x_ref[0]
    ms = jnp.mean(x * x, axis=-1, keepdims=True)
    hb = (x * lax.rsqrt(ms + RMS_EPS) * gain_ref[...]).astype(BF16)

    def wide(i):
        return _dot(hb, wbf_ref[:, _WIDE_OFF[i]:_WIDE_OFF[i + 1]])

    segm = segm_ref[...]
    q = wide(0)
    q = q * lax.rsqrt(_seg_sum(q * q, segm) * (1.0 / HEAD_DIM) + RMS_EPS) * qg_ref[...]
    fq_o[0] = q.astype(BF16).T
    k = wide(1)
    k = k * lax.rsqrt(_seg_sum(k * k, segm) * (1.0 / HEAD_DIM) + RMS_EPS) * kg_ref[...]
    fv_o[0] = wide(2).astype(BF16).T
    gqkv_o[0] = wide(3).astype(BF16)
    ggate_o[0] = wide(4).astype(BF16)
    lq_o[0] = wide(5).astype(BF16)
    lk_o[0] = wide(6).astype(BF16)
    lv_o[0] = wide(7).astype(BF16)
    lr_o[0] = wide(8).astype(BF16)

    sm = _dot(hb, wbf_ref[:, _WIDE_OFF[-1]:_WIDE_OFF[-1] + LANES])
    la_logit = _dot_split(sm, wa2_ref[...]) + ba_ref[...]
    la_o[0] = _log_sigmoid(la_logit) * (1.0 / GLA_TAU)

    smt = sm.T
    log_f = _log_sigmoid(smt[_SM_F:_SM_F + 8] + fb_ref[...])

    @pl.when(pl.program_id(1) == 0)
    def _():
        carry_ref[...] = jnp.zeros_like(carry_ref)

    cum = _lane_cumsum(log_f, tm) + carry_ref[:, 0:1]
    carry_ref[...] = jnp.broadcast_to(cum[:, tm - 1:tm], carry_ref.shape)
    pieces = []
    rem = cum * _LOG2E
    for _ in range(_C_PIECES):
        piece = rem.astype(BF16).astype(F32)
        pieces.append(piece)
        rem = rem - piece
    zero_row = jnp.zeros((1, tm), F32)
    rows = []
    for h in range(H_FOX):
        rows += [piece[h:h + 1] for piece in pieces] + [zero_row]
    rows.append(jnp.zeros((LANES - 4 * H_FOX, tm), F32))
    ptw = jnp.concatenate(rows, axis=0).T
    lane = lax.broadcasted_iota(jnp.int32, (tm, LANES), 1)
    for h in range(H_FOX):
        kp = k[:, (h // 2) * LANES:(h // 2 + 1) * LANES]
        if h % 2:
            kp = pltpu.roll(kp, HEAD_DIM, 1)
        cp = pltpu.roll(ptw, HEAD_DIM - 4 * h, 1)
        ka = jnp.where(lane < HEAD_DIM, kp, jnp.where(lane < HEAD_DIM + 4, cp, 0.0))
        ka_o[0, h] = ka.astype(BF16)
    g_o[0] = -jnp.exp(alog_ref[...]) * _softplus(smt[_SM_A:_SM_A + 8] + dtb_ref[...])
    beta_o[0] = _sigmoid(smt[_SM_B:_SM_B + 8])


def _proj_call(x, gain, w_in_all, layer, segm, qg, kg, fb, alog, dtb, wa2, ba, tm):
    b, s, d = x.shape
    tm = min(tm, s)
    const = lambda shape: pl.BlockSpec(shape, lambda i, j: (0,) * len(shape),
                                       pipeline_mode=pl.Buffered(1))
    tok = lambda w: pl.BlockSpec((1, tm, w), lambda i, j: (i, j, 0))
    row = pl.BlockSpec((1, 8, tm), lambda i, j: (i, 0, j))
    widths = [_SIZES[g] for g in _WIDE]
    out_shape = ([jax.ShapeDtypeStruct((b, s, w), BF16) for w in widths]
                 + [jax.ShapeDtypeStruct((b, s, LANES), F32)]
                 + [jax.ShapeDtypeStruct((b, 8, s), F32)] * 2)
    out_specs = [tok(w) for w in widths] + [tok(LANES)] + [row] * 2
    for i in (0, 2):
        out_shape[i] = jax.ShapeDtypeStruct((b, widths[i], s), BF16)
        out_specs[i] = pl.BlockSpec((1, widths[i], tm), lambda i, j: (i, 0, j))
    out_shape[1] = jax.ShapeDtypeStruct((b, H_FOX, s, LANES), BF16)
    out_specs[1] = pl.BlockSpec((1, H_FOX, tm, LANES), lambda i, j: (i, 0, j, 0))
    return pl.pallas_call(
        _proj_kernel,
        grid=(b, s // tm),
        in_specs=[tok(d), const((1, d)),
                  pl.BlockSpec((1,) + w_in_all.shape[1:], lambda i, j: (layer, 0, 0),
                               pipeline_mode=pl.Buffered(1)),
                  const(segm.shape),
                  const(qg.shape), const(kg.shape), const((8, 1)), const((8, 1)), const((8, 1)),
                  const(wa2.shape), const(ba.shape)],
        out_specs=out_specs,
        out_shape=out_shape,
        scratch_shapes=[pltpu.VMEM((8, LANES), F32),
                        pltpu.VMEM((d, _WIDE_OFF[-1] + LANES), BF16)],
        compiler_params=pltpu.CompilerParams(
            dimension_semantics=("arbitrary", "arbitrary"), vmem_limit_bytes=VMEM_LIMIT),
        name="proj",
    )(x, gain, w_in_all, segm, qg, kg, fb, alog, dtb, wa2, ba)


_ACC_ROWS = HEAD_DIM + 16


def _fox_kernel(qt_ref, k_ref, vt_ref, aug_ref, ones_ref, o_ref, m_ref, acc_ref):
    tq = qt_ref.shape[2]
    qi = pl.program_id(1)
    heads = range(H_FOX)
    aug = aug_ref[...]
    qts = [jnp.concatenate([qt_ref[0, h * HEAD_DIM:(h + 1) * HEAD_DIM, :], aug], axis=0)
           for h in heads]
    m_ref[...] = jnp.full_like(m_ref, -jnp.inf)
    acc_ref[...] = jnp.zeros_like(acc_ref)
    krow = lax.broadcasted_iota(jnp.int32, (tq, tq), 0)
    qcol = lax.broadcasted_iota(jnp.int32, (tq, tq), 1)
    ones = ones_ref[...]

    def step(blocks):
        k0s = [pl.multiple_of(j * tq, tq) for j, _ in blocks]
        ss = [[_dot(k_ref[0, h, pl.ds(k0, tq), :], qts[h]) for k0 in k0s]
              for h in heads]
        ss = [[jnp.where(krow <= qcol, s, -jnp.inf) if diagonal else s
               for s, (_, diagonal) in zip(sh, blocks)] for sh in ss]
        m_prev = [m_ref[h] for h in heads]
        m_new = []
        for h in heads:
            m = m_prev[h]
            for s in ss[h]:
                m = jnp.maximum(m, jnp.max(s, axis=0, keepdims=True))
            m_new.append(m)
        ps = [[jnp.exp2(s - m_new[h]).astype(BF16) for s in ss[h]] for h in heads]
        for h in heads:
            acc = jnp.exp2(m_prev[h] - m_new[h]) * acc_ref[h]
            for k0, p in zip(k0s, ps[h]):
                vta = jnp.concatenate(
                    [vt_ref[0, h * HEAD_DIM:(h + 1) * HEAD_DIM, pl.ds(k0, tq)], ones], axis=0)
                acc = acc + _dot(vta, p)
            acc_ref[h] = acc
            m_ref[h] = m_new[h]

    @pl.when(qi == 0)
    def _():
        step([(qi, True)])

    @pl.when(qi > 0)
    def _():
        step([(qi, True), (qi - 1, False)])

    rest = jnp.maximum(qi - 1, 0)

    def body(j2, carry):
        step([(2 * j2, False), (2 * j2 + 1, False)])
        return carry

    lax.fori_loop(0, rest // 2, body, 0)

    @pl.when(rest % 2 == 1)
    def _():
        step([(rest - 1, False)])
    outs = []
    for h in heads:
        acc = acc_ref[h]
        outs.append(acc[0:HEAD_DIM] / acc[HEAD_DIM:HEAD_DIM + 1])
    o_ref[0] = jnp.concatenate(outs, axis=0).T.astype(o_ref.dtype)


def _fox_call(qt, ka, vt, tq):
    b, _, s = qt.shape
    tq = min(tq, s)
    aug = jnp.broadcast_to(jnp.where(jnp.arange(HEAD_DIM)[:, None] < _C_PIECES, -1.0, 0.0),
                           (HEAD_DIM, tq)).astype(BF16)
    ones = jnp.broadcast_to(jnp.where(jnp.arange(16)[:, None] == 0, 1.0, 0.0), (16, tq)).astype(BF16)
    return pl.pallas_call(
        _fox_kernel,
        grid=(b, s // tq),
        in_specs=[pl.BlockSpec((1, FOX_DIM, tq), lambda i, j: (i, 0, j)),
                  pl.BlockSpec((1, H_FOX, s, LANES), lambda i, j: (i, 0, 0, 0)),
                  pl.BlockSpec((1, FOX_DIM, s), lambda i, j: (i, 0, 0)),
                  pl.BlockSpec(aug.shape, lambda i, j: (0, 0)),
                  pl.BlockSpec(ones.shape, lambda i, j: (0, 0))],
        out_specs=pl.BlockSpec((1, tq, FOX_DIM), lambda i, j: (i, j, 0)),
        out_shape=jax.ShapeDtypeStruct((b, s, FOX_DIM), BF16),
        scratch_shapes=[pltpu.VMEM((H_FOX, 1, tq), F32),
                        pltpu.VMEM((H_FOX, _ACC_ROWS, tq), F32)],
        compiler_params=pltpu.CompilerParams(
            dimension_semantics=("arbitrary", "arbitrary"),
            vmem_limit_bytes=VMEM_LIMIT),
        name="fox",
    )(qt, ka, vt, aug, ones)


def _bmm(a, b):
    return jnp.einsum('gmk,gkn->gmn', a.astype(BF16), b.astype(BF16),
                      preferred_element_type=F32)


def _pair_diag(x):
    xb = x.astype(BF16)
    low = lax.broadcasted_iota(jnp.int32, (1,) + xb.shape[1:], 2) < HEAD_DIM
    zero = jnp.zeros_like(xb)
    return jnp.concatenate([jnp.where(low, xb, zero), jnp.where(low, zero, xb)], axis=1)


def _unit_lower_inverse(a_strict, eye, between):
    n = -a_strict
    t = eye + n
    nd = _pair_diag(n)
    size, step = 1, 0
    while 2 * size < CHUNK:
        n = _bmm(n, nd)
        between(step)
        nd = _pair_diag(n)
        t = t + _bmm(t, nd)
        size *= 2
        step += 1
    return t


def _gdn_reset(state_ref, xext_ref):
    state_ref[...] = jnp.zeros_like(state_ref)
    xext_ref[0:SUBLANES, :] = jnp.zeros((SUBLANES, xext_ref.shape[1]), F32)


def _gdn_body(qkv_ref, convw_ref, g_ref, beta_ref, segm_ref, o_ref, state_ref, xext_ref, between):
    tc = qkv_ref.shape[1]
    pad = SUBLANES
    x = qkv_ref[0].astype(F32)
    xext_ref[pad:pad + tc, :] = x
    y = convw_ref[0:1, :] * xext_ref[pad - 3:pad - 3 + tc, :]
    for i in range(1, CONV_K):
        y = y + convw_ref[i:i + 1, :] * xext_ref[pad - 3 + i:pad - 3 + i + tc, :]
    xext_ref[0:pad, :] = x[tc - pad:tc, :]
    y = y * _sigmoid(y)

    segm = segm_ref[...]
    q = y[:, 0:GDN_DIM]
    k = y[:, GDN_DIM:2 * GDN_DIM]
    v = y[:, 2 * GDN_DIM:3 * GDN_DIM]
    q = q * lax.rsqrt(_seg_sum(q * q, segm) + RMS_EPS) * (HEAD_DIM ** -0.5)
    k = k * lax.rsqrt(_seg_sum(k * k, segm) + RMS_EPS)

    kt = k.T

    dec_row = _lane_cumsum(g_ref[0], CHUNK)
    dec_col = dec_row.T
    beta_col = beta_ref[0].T

    head_of_lane = lax.broadcasted_iota(jnp.int32, (tc, GDN_DIM), 1) // HEAD_DIM
    dexp = jnp.zeros((tc, GDN_DIM), F32)
    bexp = jnp.zeros((tc, GDN_DIM), F32)
    for h in range(H_GDN):
        dexp = jnp.where(head_of_lane == h, dec_col[:, h:h + 1], dexp)
        bexp = jnp.where(head_of_lane == h, beta_col[:, h:h + 1], bexp)
    edec = jnp.exp(dexp)
    kb = k * bexp
    vb = v * bexp
    kbe = kb * edec
    qd = q * edec

    nc = tc // CHUNK
    npair = H_GDN // 2
    index = [(c, p) for c in range(nc) for p in range(npair)]

    def split(a):
        return jnp.stack([a[c * CHUNK:(c + 1) * CHUNK, p * LANES:(p + 1) * LANES]
                          for c, p in index])

    q3, kb3, vb3, kbe3, qd3, dcol = (split(a) for a in (q, kb, vb, kbe, qd, dexp))
    drow = jnp.stack([jnp.concatenate(
        [dec_row[2 * p + hh:2 * p + hh + 1, c * CHUNK:(c + 1) * CHUNK] for hh in range(2)], axis=1)
        for c, p in index])
    dlast = dcol[:, CHUNK - 1:CHUNK, :]

    low = lax.broadcasted_iota(jnp.int32, (CHUNK, LANES), 1) < HEAD_DIM
    kdt_list, kd_list = [], []
    for c, p in index:
        tile = kt[p * LANES:(p + 1) * LANES, (c // 2) * LANES:(c // 2 + 1) * LANES]
        swapped = pltpu.roll(tile, HEAD_DIM, 1)
        top = (tile if c % 2 == 0 else swapped)[0:HEAD_DIM]
        bot = (swapped if c % 2 == 0 else tile)[HEAD_DIM:]
        kdt_list.append(jnp.where(low, top, bot))
        kd_list.append(jnp.concatenate([jnp.where(low, top, 0.0), jnp.where(low, 0.0, bot)],
                                       axis=0).astype(BF16))
    kt3 = jnp.stack(kdt_list)
    ktd = jnp.stack(kd_list)

    ri = lax.broadcasted_iota(jnp.int32, (1, CHUNK, LANES), 1)
    cj = lax.broadcasted_iota(jnp.int32, (1, CHUNK, LANES), 2) & (HEAD_DIM - 1)
    causal = cj <= ri
    eye = (cj == ri).astype(F32)
    gamma = jnp.where(causal, jnp.exp(jnp.where(causal, dcol - drow, 0.0)), 0.0)
    a = jnp.where(cj < ri, _bmm(kb3, ktd) * gamma, 0.0)
    t = _unit_lower_inverse(a, eye, between)
    u3 = _bmm(t, _pair_diag(vb3))
    w3 = _bmm(t, _pair_diag(kbe3))
    intra = _bmm(q3, ktd) * gamma
    kdt = kt3 * jnp.exp(dlast - drow)
    elast = jnp.exp(dlast)

    s = state_ref[...]
    for c in range(nc):
        sl = slice(c * npair, (c + 1) * npair)
        sd = _pair_diag(s)
        v_new = u3[sl] - _bmm(w3[sl], sd)
        vd = _pair_diag(v_new)
        o = _bmm(qd3[sl], sd) + _bmm(intra[sl], vd)
        s = s * elast[sl] + _bmm(kdt[sl], vd)
        o_ref[0, c * CHUNK:(c + 1) * CHUNK, :] = jnp.concatenate(
            [o[p] for p in range(npair)], axis=1).astype(o_ref.dtype)
    state_ref[...] = s


def _gla_chunks(chunks, q_ref, k_ref, v_ref, la_ref, tri_ref, mh_ref, mt_ref, o_ref,
                st_ref, kf_all, bq_all, vf_all):
    row = lax.broadcasted_iota(jnp.int32, (CHUNK, GLA_K_DIM), 0)
    mh = mh_ref[...]
    for c in chunks:
        kf_ref, bq_ref, vf_ref = kf_all.at[c], bq_all.at[c], vf_all.at[c]
        r0, r1 = c * CHUNK, (c + 1) * CHUNK
        bq = _dot_exact_lhs(tri_ref[...], la_ref[0, r0:r1, :])
        q = q_ref[0, r0:r1, :].astype(F32) * (GLA_DK ** -0.5)
        k = k_ref[0, r0:r1, :].astype(F32)
        v = v_ref[0, r0:r1, :].astype(F32)
        kf_ref[...] = k
        bq_ref[...] = bq
        vf_ref[...] = v
        st = st_ref[...]
        o_inter = _dot_nt((q * jnp.exp(bq)).astype(BF16), st.astype(BF16))

        group_out = []
        for g0 in range(0, CHUNK, SUBLANES):
            n = CHUNK - g0
            qg, bg, rg = q[g0:, :], bq[g0:, :], row[g0:, :]
            es = []
            for j in range(g0, g0 + SUBLANES):
                kj = kf_ref[j:j + 1, :]
                bj = bq_ref[j:j + 1, :]
                decay = jnp.exp(bg - bj)
                decay = jnp.concatenate(
                    [jnp.where(rg[:SUBLANES] >= j, decay[:SUBLANES], 0.0), decay[SUBLANES:]],
                    axis=0) if n > SUBLANES else jnp.where(rg >= j, decay, 0.0)
                es.append(decay * (qg * kj))
            p = _dot(jnp.concatenate(es, axis=0).astype(BF16), mh)
            acc = p[0:n] * vf_ref[g0:g0 + 1, :]
            for jj in range(1, SUBLANES):
                acc = acc + p[jj * n:(jj + 1) * n] * vf_ref[g0 + jj:g0 + jj + 1, :]
            group_out.append(acc)
        pieces = []
        for r in range(0, CHUNK, SUBLANES):
            piece = o_inter[r:r + SUBLANES]
            for gi, g0 in enumerate(range(0, r + SUBLANES, SUBLANES)):
                piece = piece + group_out[gi][r - g0:r - g0 + SUBLANES]
            pieces.append(piece)
        o_ref[0, r0:r1, :] = jnp.concatenate(pieces, axis=0).astype(o_ref.dtype)
        blast = bq[CHUNK - 1:CHUNK, :]
        kd = k * jnp.exp(blast - bq)
        upd = _dot_tn(v.astype(BF16), kd.astype(BF16))
        st_ref[...] = (st * jnp.exp(blast) + upd) * mt_ref[...]


_N_GDN_IN, _N_GLA_IN = 5, 7


def _recurrent_kernel(*refs):
    gdn_in = refs[:_N_GDN_IN]
    gla_in = refs[_N_GDN_IN:_N_GDN_IN + _N_GLA_IN]
    o_gdn, o_gla = refs[_N_GDN_IN + _N_GLA_IN:_N_GDN_IN + _N_GLA_IN + 2]
    state_ref, xext_ref, st_ref, kf_ref, bq_ref, vf_ref = refs[_N_GDN_IN + _N_GLA_IN + 2:]

    @pl.when(pl.program_id(1) == 0)
    def _():
        _gdn_reset(state_ref, xext_ref)
        st_ref[...] = jnp.zeros_like(st_ref)

    n_chunks = o_gla.shape[1] // CHUNK
    done = []

    def gla_chunks(step):
        if step < n_chunks:
            _gla_chunks([step], *gla_in, o_gla, st_ref, kf_ref, bq_ref, vf_ref)
            done.append(step)

    _gdn_body(*gdn_in, o_gdn, state_ref, xext_ref, gla_chunks)
    _gla_chunks([c for c in range(n_chunks) if c not in done], *gla_in, o_gla,
                st_ref, kf_ref, bq_ref, vf_ref)


def _recurrent_call(qkv, convw, g, beta, segm, q, k, v, la, tc):
    b, s, w = qkv.shape
    tc = min(tc, s)
    tri = (jnp.arange(CHUNK)[:, None] >= jnp.arange(CHUNK)[None, :]).astype(F32)
    hk = jnp.arange(GLA_K_DIM) // GLA_DK
    hv = jnp.arange(GLA_V_DIM) // HEAD_DIM
    mh = (hk[:, None] == hv[None, :]).astype(BF16)
    mt = (hv[:, None] == hk[None, :]).astype(F32)
    tok = lambda wd: pl.BlockSpec((1, tc, wd), lambda i, j: (i, j, 0))
    row = pl.BlockSpec((1, 8, tc), lambda i, j: (i, 0, j))
    const = lambda a: pl.BlockSpec(a.shape, lambda i, j: (0, 0))
    return pl.pallas_call(
        _recurrent_kernel,
        grid=(b, s // tc),
        in_specs=[tok(w), const(convw), row, row, const(segm),
                  tok(GLA_K_DIM), tok(GLA_K_DIM), tok(GLA_V_DIM), tok(GLA_K_DIM),
                  const(tri), const(mh), const(mt)],
        out_specs=[tok(GDN_DIM), tok(GLA_V_DIM)],
        out_shape=[jax.ShapeDtypeStruct((b, s, GDN_DIM), BF16),
                   jax.ShapeDtypeStruct((b, s, GLA_V_DIM), BF16)],
        scratch_shapes=[pltpu.VMEM((H_GDN // 2, HEAD_DIM, LANES), F32),
                        pltpu.VMEM((tc + SUBLANES, w), F32),
                        pltpu.VMEM((GLA_V_DIM, GLA_K_DIM), F32),
                        pltpu.VMEM((tc // CHUNK, CHUNK, GLA_K_DIM), F32),
                        pltpu.VMEM((tc // CHUNK, CHUNK, GLA_K_DIM), F32),
                        pltpu.VMEM((tc // CHUNK, CHUNK, GLA_V_DIM), F32)],
        compiler_params=pltpu.CompilerParams(
            dimension_semantics=("arbitrary", "arbitrary"), vmem_limit_bytes=VMEM_LIMIT),
        name="gdn_gla",
    )(qkv, convw, g, beta, segm, q, k, v, la, tri, mh, mt)


def _head_norm(o_ref, segm, gain):
    o = o_ref[...].astype(F32)
    return o * lax.rsqrt(_seg_sum(o * o, segm) * (1.0 / HEAD_DIM) + RMS_EPS) * gain


def _silu(x):
    return x * _sigmoid(x)


def _out_kernel(x_ref, ofox_ref, ogdn_ref, ggate_ref, ogla_ref, lr_ref, wout_ref, segm_ref,
                gf_ref, gg_ref, gl_ref, fgain_ref, wr_ref, br_ref, tri_ref,
                x1_o, h2_o, meta_o, cnt_o, carry_ref):
    tm = x_ref.shape[0]
    segm = segm_ref[...]
    a = _head_norm(ofox_ref, segm, gf_ref[...])
    bb = _head_norm(ogdn_ref, segm, gg_ref[...]) * _silu(ggate_ref[...].astype(F32))
    cc = (_head_norm(ogla_ref, segm[:GLA_V_DIM, :GLA_V_DIM], gl_ref[...])
          * _silu(lr_ref[...].astype(F32)))
    y = (x_ref[...]
         + _dot(a.astype(BF16), wout_ref[0:FOX_DIM, :].astype(BF16))
         + _dot(bb.astype(BF16), wout_ref[FOX_DIM:FOX_DIM + GDN_DIM, :].astype(BF16))
         + _dot(cc.astype(BF16), wout_ref[FOX_DIM + GDN_DIM:, :].astype(BF16)))
    x1_o[...] = y
    ms = jnp.mean(y * y, axis=-1, keepdims=True)
    h2 = y * lax.rsqrt(ms + RMS_EPS) * fgain_ref[...]
    _to_token_tiles(h2_o, h2)

    wr_hi, wr_lo = _split(wr_ref[...])
    h2_hi, h2_lo = _split(h2)
    hi_terms = _dot(h2_hi, jnp.concatenate([wr_hi, wr_lo], axis=1))
    logits = (hi_terms[:, :LANES] + (hi_terms[:, LANES:] + _dot(h2_lo, wr_hi))
              + br_ref[...])
    lane = lax.broadcasted_iota(jnp.int32, logits.shape, 1).astype(F32)
    big = float(4 * LANES)
    ninf = -jnp.inf
    gl = jnp.where(lane < _R_GROUP + N_GROUPS, logits, ninf)
    gmax = jnp.max(gl, axis=1, keepdims=True)
    group_p = 1.0 / jnp.sum(jnp.exp(gl - gmax), axis=1, keepdims=True)
    gidx = jnp.min(jnp.where(gl == gmax, lane, big), axis=1, keepdims=True)
    elane = lane - _R_EXPERT
    group_of_lane = jnp.floor(elane * (1.0 / EXPERTS_PER_GROUP))
    in_group = (elane >= 0) & (elane < N_EXPERTS) & (group_of_lane == gidx)
    el = jnp.where(in_group, logits, ninf)
    m1 = jnp.max(el, axis=1, keepdims=True)
    i1 = jnp.min(jnp.where(el == m1, lane, big), axis=1, keepdims=True)
    el2 = jnp.where(lane == i1, ninf, el)
    m2 = jnp.max(el2, axis=1, keepdims=True)
    i2 = jnp.min(jnp.where(el2 == m2, lane, big), axis=1, keepdims=True)
    t = jnp.exp(m2 - m1)
    g1 = group_p / (1.0 + t)
    g2 = group_p * t / (1.0 + t)

    @pl.when(pl.program_id(0) == 0)
    def _():
        carry_ref[...] = jnp.zeros_like(carry_ref)

    sel = jnp.where((lane == i1) | (lane == i2), 1.0, 0.0)
    carry = carry_ref[0:1, :]
    rank = _dot(tri_ref[...], sel.astype(BF16)) + carry
    rank1 = jnp.sum(jnp.where(lane == i1, rank, 0.0), axis=1, keepdims=True)
    rank2 = jnp.sum(jnp.where(lane == i2, rank, 0.0), axis=1, keepdims=True)
    new_carry = carry + jnp.sum(sel, axis=0, keepdims=True)
    carry_ref[...] = jnp.broadcast_to(new_carry, carry_ref.shape)
    cnt_o[...] = jnp.broadcast_to(new_carry, cnt_o.shape)
    cols = [i1 - _R_EXPERT, i2 - _R_EXPERT, rank1, rank2, g1, g2]
    meta = jnp.zeros(logits.shape, F32)
    for idx, col in enumerate(cols):
        meta = jnp.where(lane == idx, col, meta)
    meta_o[...] = meta


def _out_call(x, ofox, ogdn, ggate, ogla, lr, wout, segm, gf, gg, gl, fgain, wr, br, tm):
    n, d = x.shape
    tm = min(tm, n)
    tri = (jnp.arange(tm)[:, None] > jnp.arange(tm)[None, :]).astype(BF16)
    tok = lambda w: pl.BlockSpec((tm, w), lambda i: (i, 0))
    const = lambda a: pl.BlockSpec(a.shape, lambda i: (0,) * a.ndim,
                                   pipeline_mode=pl.Buffered(1))
    return pl.pallas_call(
        _out_kernel,
        grid=(n // tm,),
        in_specs=[tok(d), tok(FOX_DIM), tok(GDN_DIM), tok(GDN_DIM), tok(GLA_V_DIM), tok(GLA_V_DIM),
                  const(wout), const(segm), const(gf), const(gg), const(gl), const(fgain),
                  const(wr), const(br), const(tri)],
        out_specs=[tok(d), pl.BlockSpec((tm * SUBLANES, LANES), lambda i: (i, 0)), tok(LANES),
                   pl.BlockSpec((8, LANES), lambda i: (0, 0))],
        out_shape=[jax.ShapeDtypeStruct((n, d), F32),
                   jax.ShapeDtypeStruct((n * SUBLANES, LANES), F32),
                   jax.ShapeDtypeStruct((n, LANES), F32), jax.ShapeDtypeStruct((8, LANES), F32)],
        scratch_shapes=[pltpu.VMEM((8, LANES), F32)],
        compiler_params=pltpu.CompilerParams(
            dimension_semantics=("arbitrary",), vmem_limit_bytes=VMEM_LIMIT),
        name="out_router",
    )(x, ofox, ogdn, ggate, ogla, lr, wout, segm, gf, gg, gl, fgain, wr, br, tri)


_DMA_UNROLL = 8


def _tile_rows(r):
    return pl.ds(pl.multiple_of(r * SUBLANES, SUBLANES), SUBLANES)


def _to_token_tiles(ref, x):
    t = x.shape[0]
    for s in range(SUBLANES):
        ref[pl.ds(s, t, stride=SUBLANES), :] = x[:, s * LANES:(s + 1) * LANES]


def _from_token_tiles(ref, t):
    return [ref[pl.ds(s, t, stride=SUBLANES), :] for s in range(SUBLANES)]


def _dispatch_kernel(zb_ref, dest_ref, h_ref, xb_ref, zero_ref, sem, zsem):
    td = h_ref.shape[0] // SUBLANES

    @pl.when(pl.program_id(0) == 0)
    def _():
        zero_ref[...] = jnp.zeros_like(zero_ref)
        rows = zero_ref.shape[0]

        def block_copy(b):
            return pltpu.make_async_copy(
                zero_ref, xb_ref.at[pl.ds(pl.multiple_of(b * rows, rows), rows), :], zsem)

        def start(b, carry):
            @pl.when(zb_ref[b] != 0)
            def _():
                block_copy(b).start()
            return carry

        def wait(b, carry):
            @pl.when(zb_ref[b] != 0)
            def _():
                block_copy(b).wait()
            return carry

        lax.fori_loop(0, zb_ref.shape[0], start, 0)
        lax.fori_loop(0, zb_ref.shape[0], wait, 0)

    def row_copy(t, d):
        return pltpu.make_async_copy(h_ref.at[_tile_rows(t), :], xb_ref.at[_tile_rows(d), :], sem)

    def issue(t, carry):
        for kk in range(TOP_K):
            row_copy(t, dest_ref[0, 0, TOP_K * t + kk]).start(priority=kk % 2)
        return carry

    lax.fori_loop(0, td, issue, 0, unroll=_DMA_UNROLL)

    def drain(t, carry):
        for kk in range(TOP_K):
            row_copy(0, 0).wait()
        return carry

    lax.fori_loop(0, td, drain, 0, unroll=_DMA_UNROLL)


def _dispatch_call(zero_block, dest, h2, tmb, td):
    n = h2.shape[0] // SUBLANES
    td = min(td, n)
    n_rows = zero_block.shape[0] * tmb
    dest3 = dest.reshape(n // td, 1, TOP_K * td)
    return pl.pallas_call(
        _dispatch_kernel,
        grid_spec=pltpu.PrefetchScalarGridSpec(
            num_scalar_prefetch=1,
            grid=(n // td,),
            in_specs=[pl.BlockSpec((1, 1, TOP_K * td), lambda i, zb: (i, 0, 0),
                                   memory_space=pltpu.SMEM),
                      pl.BlockSpec((td * SUBLANES, LANES), lambda i, zb: (i, 0))],
            out_specs=pl.BlockSpec(memory_space=pl.ANY),
            scratch_shapes=[pltpu.VMEM((tmb * SUBLANES, LANES), h2.dtype),
                            pltpu.SemaphoreType.DMA(()), pltpu.SemaphoreType.DMA(())],
        ),
        out_shape=jax.ShapeDtypeStruct((n_rows * SUBLANES, LANES), h2.dtype),
        compiler_params=pltpu.CompilerParams(
            dimension_semantics=("arbitrary",), has_side_effects=True),
        name="dispatch",
    )(zero_block, dest3, h2)


def _expert_kernel(be_ref, nu_ref, x_ref, wg_ref, wu_ref, wd_ref, y_ref, wgb_ref, wub_ref, wdb_ref):
    i = pl.program_id(0)
    used = i < nu_ref[0]

    @pl.when(used & ((i == 0) | (be_ref[i] != be_ref[jnp.maximum(i - 1, 0)])))
    def _():
        wgb_ref[...] = wg_ref[0].astype(BF16)
        wub_ref[...] = wu_ref[0].astype(BF16)
        wdb_ref[...] = wd_ref[0].astype(BF16)

    @pl.when(used)
    def _():
        tmb = x_ref.shape[0] // SUBLANES
        x = jnp.concatenate([blk.astype(BF16) for blk in _from_token_tiles(x_ref, tmb)], axis=1)
        a = _dot(x, wgb_ref[...])
        u = _dot(x, wub_ref[...])
        hmid = (_silu(a) * u).astype(BF16)
        _to_token_tiles(y_ref, _dot(hmid, wdb_ref[...]))

    @pl.when(jnp.logical_not(used))
    def _():
        y_ref[...] = jnp.zeros_like(y_ref)


def _expert_call(block_e, n_used, xb, wg, wu, wd, tmb):
    n_rows = xb.shape[0] // SUBLANES
    d = SUBLANES * LANES
    de = wg.shape[-1]
    n_blocks = n_rows // tmb

    def xmap(i, be, nu):
        return (jnp.minimum(i, jnp.maximum(nu[0] - 1, 0)), 0)

    wmap = lambda i, be, nu: (be[i], 0, 0)
    return pl.pallas_call(
        _expert_kernel,
        grid_spec=pltpu.PrefetchScalarGridSpec(
            num_scalar_prefetch=2,
            grid=(n_blocks,),
            in_specs=[pl.BlockSpec((tmb * SUBLANES, LANES), xmap),
                      pl.BlockSpec((1, d, de), wmap),
                      pl.BlockSpec((1, d, de), wmap),
                      pl.BlockSpec((1, de, d), wmap)],
            out_specs=pl.BlockSpec((tmb * SUBLANES, LANES), lambda i, be, nu: (i, 0)),
            scratch_shapes=[pltpu.VMEM((d, de), BF16), pltpu.VMEM((d, de), BF16),
                            pltpu.VMEM((de, d), BF16)],
        ),
        out_shape=jax.ShapeDtypeStruct((n_rows * SUBLANES, LANES), F32),
        compiler_params=pltpu.CompilerParams(
            dimension_semantics=("arbitrary",), vmem_limit_bytes=VMEM_LIMIT),
        name="experts",
    )(block_e, n_used, xb, wg, wu, wd)


def _combine_kernel(dest_ref, x1_ref, meta_ref, yb_ref, o_ref, buf_ref, sem):
    td = x1_ref.shape[0]

    def row_copy(t, kk, d):
        return pltpu.make_async_copy(yb_ref.at[_tile_rows(d), :], buf_ref.at[kk, _tile_rows(t), :],
                                     sem)

    def issue(t, carry):
        for kk in range(TOP_K):
            row_copy(t, kk, dest_ref[0, 0, TOP_K * t + kk]).start(priority=kk % 2)
        return carry

    lax.fori_loop(0, td, issue, 0, unroll=_DMA_UNROLL)

    def drain(t, carry):
        for kk in range(TOP_K):
            row_copy(0, kk, 0).wait()
        return carry

    lax.fori_loop(0, td, drain, 0, unroll=_DMA_UNROLL)
    meta = meta_ref[...]
    g1, g2 = meta[:, 4:5], meta[:, 5:6]
    y1 = _from_token_tiles(buf_ref.at[0], td)
    y2 = _from_token_tiles(buf_ref.at[1], td)
    for s in range(SUBLANES):
        lanes = slice(s * LANES, (s + 1) * LANES)
        o_ref[:, lanes] = x1_ref[:, lanes] + g1 * y1[s] + g2 * y2[s]


def _combine_call(dest, x1, meta, yb, td):
    n, d = x1.shape
    td = min(td, n)
    dest3 = dest.reshape(n // td, 1, TOP_K * td)
    return pl.pallas_call(
        _combine_kernel,
        grid=(n // td,),
        in_specs=[pl.BlockSpec((1, 1, TOP_K * td), lambda i: (i, 0, 0), memory_space=pltpu.SMEM),
                  pl.BlockSpec((td, d), lambda i: (i, 0)),
                  pl.BlockSpec((td, LANES), lambda i: (i, 0)),
                  pl.BlockSpec(memory_space=pl.ANY)],
        out_specs=pl.BlockSpec((td, d), lambda i: (i, 0)),
        out_shape=jax.ShapeDtypeStruct((n, d), F32),
        scratch_shapes=[pltpu.VMEM((TOP_K, td * SUBLANES, LANES), yb.dtype),
                        pltpu.SemaphoreType.DMA(())],
        compiler_params=pltpu.CompilerParams(
            dimension_semantics=("arbitrary",), vmem_limit_bytes=VMEM_LIMIT),
        name="combine",
    )(dest3, x1, meta, yb)


TM_PROJ = 512
TQ_FOX = 256
TC_RECURRENT = 256
TM_OUT = 512
TD_MOE = 256
TMB_EXPERT = 256


def _place(width, parts):
    cols, at = [], 0
    for pos, blk in parts:
        if pos > at:
            cols.append(jnp.zeros((blk.shape[0], pos - at), blk.dtype))
        cols.append(blk)
        at = pos + blk.shape[1]
    if width > at:
        cols.append(jnp.zeros((parts[0][1].shape[0], width - at), parts[0][1].dtype))
    return jnp.concatenate(cols, axis=1)


def _pad8(v):
    return jnp.zeros((8,), F32).at[:v.shape[0]].set(v.astype(F32)).reshape(8, 1)


def _token_mixer(x, attn_norm, w_in_all, layer, fox_q_norm, fox_k_norm, fox_f_bias,
                 gdn_conv, gdn_a_log, gdn_dt_bias, gla_w_a2, gla_b_a):
    b, s, d = x.shape
    wa2 = jnp.pad(gla_w_a2, ((_SM_A1, LANES - _SM_A1 - GLA_RANK), (0, 0)))
    segm = _seg_matrix(FOX_DIM)
    qg = (jnp.tile(fox_q_norm, H_FOX) * (HEAD_DIM ** -0.5 * _LOG2E)).reshape(1, FOX_DIM)
    kg = jnp.tile(fox_k_norm, H_FOX).reshape(1, FOX_DIM)
    outs = _proj_call(x, attn_norm.reshape(1, d), w_in_all, layer, segm, qg, kg,
                      _pad8(fox_f_bias), _pad8(gdn_a_log), _pad8(gdn_dt_bias),
                      wa2, gla_b_a.reshape(1, GLA_K_DIM), TM_PROJ)
    fq, fka, fv, gqkv, ggate, lq, lk, lv, lr, la, g, beta = outs
    o_fox = _fox_call(fq, fka, fv, TQ_FOX)
    o_gdn, o_gla = _recurrent_call(gqkv, gdn_conv.astype(F32), g, beta, segm,
                                   lq, lk, lv, la, TC_RECURRENT)
    return o_fox, o_gdn, ggate, o_gla, lr


def _layer(x, p, layer, w_in_all, experts):
    b, s, d = x.shape
    n = b * s
    o_fox, o_gdn, ggate, o_gla, lr = _token_mixer(
        x, p['attn_norm'], w_in_all, layer, p['fox_q_norm'], p['fox_k_norm'], p['fox_f_bias'],
        p['gdn_conv'], p['gdn_a_log'], p['gdn_dt_bias'], p['gla_w_a2'], p['gla_b_a'])
    wr = _place(LANES, [(_R_GROUP, p['w_router_group']), (_R_EXPERT, p['w_router_expert'])])
    br = _place(LANES, [(_R_GROUP, p['b_router_group'].reshape(1, -1)),
                        (_R_EXPERT, p['b_router_expert'].reshape(1, -1))])
    flat = lambda a: a.reshape(n, a.shape[-1])
    x1, h2, meta, cnt = _out_call(
        flat(x), flat(o_fox), flat(o_gdn), flat(ggate), flat(o_gla), flat(lr),
        p['w_out'], _seg_matrix(FOX_DIM),
        jnp.tile(p['fox_o_norm'], H_FOX).reshape(1, FOX_DIM),
        jnp.tile(p['gdn_o_norm'], H_GDN).reshape(1, GDN_DIM),
        jnp.tile(p['gla_o_norm'], H_GLA).reshape(1, GLA_V_DIM),
        p['ffn_norm'].reshape(1, d), wr, br, TM_OUT)

    tmb = TMB_EXPERT
    counts = cnt[0, _R_EXPERT:_R_EXPERT + N_EXPERTS].astype(jnp.int32)
    padded = (counts + tmb - 1) // tmb * tmb
    pends = jnp.cumsum(padded)
    pstarts = pends - padded
    eid = meta[:, 0:TOP_K].astype(jnp.int32)
    rank = meta[:, TOP_K:2 * TOP_K].astype(jnp.int32)
    expert_ids = jnp.arange(N_EXPERTS, dtype=jnp.int32)
    start_of = jnp.sum(jnp.where(eid[..., None] == expert_ids, pstarts, 0), axis=-1)
    dest = (start_of + rank).reshape(-1)
    n_blocks = -(-(n * TOP_K) // tmb) + N_EXPERTS
    block_start = jnp.arange(n_blocks, dtype=jnp.int32) * tmb
    block_e = jnp.minimum(jnp.sum(pends[None, :] <= block_start[:, None], axis=1),
                          N_EXPERTS - 1).astype(jnp.int32)
    n_used = (pends[-1:] // tmb).astype(jnp.int32)

    is_last = jnp.any((block_start + tmb)[:, None] == pends[None, :], axis=1)
    zero_block = (is_last | (block_start >= pends[-1])).astype(jnp.int32)
    xb = _dispatch_call(zero_block, dest, h2, tmb, TD_MOE)
    yb = _expert_call(block_e + layer * N_EXPERTS, n_used, xb, *experts, tmb)
    x2 = _combine_call(dest, x1, meta, yb, TD_MOE)
    return x2.reshape(b, s, d)


_PARAM_NAMES = ['attn_norm', 'w_in', 'fox_q_norm', 'fox_k_norm', 'fox_f_bias', 'fox_o_norm',
                'gdn_conv', 'gdn_a_log', 'gdn_dt_bias', 'gdn_o_norm',
                'gla_w_a2', 'gla_b_a', 'gla_o_norm', 'w_out',
                'ffn_norm', 'w_router_group', 'b_router_group', 'w_router_expert',
                'b_router_expert', 'w_expert_gate', 'w_expert_up', 'w_expert_down']


def kernel(x, attn_norm, w_in, fox_q_norm, fox_k_norm, fox_f_bias, fox_o_norm, gdn_conv, gdn_a_log, gdn_dt_bias, gdn_o_norm, gla_w_a2, gla_b_a, gla_o_norm, w_out, ffn_norm, w_router_group, b_router_group, w_router_expert, b_router_expert, w_expert_gate, w_expert_up, w_expert_down):
    params = dict(zip(_PARAM_NAMES, (
        attn_norm, w_in, fox_q_norm, fox_k_norm, fox_f_bias, fox_o_norm, gdn_conv, gdn_a_log,
        gdn_dt_bias, gdn_o_norm, gla_w_a2, gla_b_a, gla_o_norm, w_out, ffn_norm,
        w_router_group, b_router_group, w_router_expert, b_router_expert,
        w_expert_gate, w_expert_up, w_expert_down)))
    experts = tuple(params.pop(name).reshape((-1,) + params_shape[2:])
                    for name, params_shape in (('w_expert_gate', w_expert_gate.shape),
                                               ('w_expert_up', w_expert_up.shape),
                                               ('w_expert_down', w_expert_down.shape)))
    del params['w_in']
    for layer in range(attn_norm.shape[0]):
        x = _layer(x, {name: val[layer] for name, val in params.items()}, layer, w_in, experts)
    return x
```

```python
import functools

import jax
import jax.numpy as jnp
from jax import lax
from jax.experimental import pallas as pl
from jax.experimental.pallas import tpu as pltpu

F32 = jnp.float32
BF16 = jnp.bfloat16

HEAD_DIM = 64
H_FOX = 6
H_GDN = 6
H_GLA = 4
FOX_DIM = H_FOX * HEAD_DIM
GDN_DIM = H_GDN * HEAD_DIM
GLA_DK = 32
GLA_K_DIM = H_GLA * GLA_DK
GLA_V_DIM = H_GLA * HEAD_DIM
GLA_RANK = 16
GLA_TAU = 16.0
CONV_K = 4
CHUNK = 64
N_GROUPS = 4
EXPERTS_PER_GROUP = 8
N_EXPERTS = N_GROUPS * EXPERTS_PER_GROUP
TOP_K = 2
RMS_EPS = 1e-6

_LOG2E = 1.4426950408889634
_C_PIECES = 3

LANES = 128
SUBLANES = 8
VMEM_LIMIT = 56 * 1024 * 1024

_SIZES = [FOX_DIM, FOX_DIM, FOX_DIM, H_FOX, 3 * GDN_DIM, GDN_DIM, H_GDN, H_GDN,
          GLA_K_DIM, GLA_K_DIM, GLA_V_DIM, GLA_V_DIM, GLA_RANK]
_OFFS = [sum(_SIZES[:i]) for i in range(len(_SIZES) + 1)]
_WIDE = [0, 1, 2, 4, 5, 8, 9, 10, 11]
_WIDE_OFF = [0]
for _g in _WIDE:
    _WIDE_OFF.append(_WIDE_OFF[-1] + _SIZES[_g])
_SM_F, _SM_A, _SM_B, _SM_A1 = 0, 8, 16, 32
_R_GROUP, _R_EXPERT = 0, 32


def _dot(a, b):
    return jnp.dot(a, b, preferred_element_type=F32)


def _dot_nt(a, b):
    return lax.dot_general(a, b, (((1,), (1,)), ((), ())), preferred_element_type=F32)


def _dot_tn(a, b):
    return lax.dot_general(a, b, (((0,), (0,)), ((), ())), preferred_element_type=F32)


def _split(a):
    hi = a.astype(BF16)
    return hi, (a - hi.astype(F32)).astype(BF16)


def _dot_split(a, b):
    a_hi, a_lo = _split(a)
    b_hi, b_lo = _split(b)
    return _dot(a_hi, b_hi) + (_dot(a_hi, b_lo) + _dot(a_lo, b_hi))


def _dot_exact_lhs(a01, b):
    a16 = a01.astype(BF16)
    b_hi = b.astype(BF16)
    rem = b - b_hi.astype(F32)
    b_mid = rem.astype(BF16)
    b_lo = (rem - b_mid.astype(F32)).astype(BF16)
    return _dot(a16, b_hi) + (_dot(a16, b_mid) + _dot(a16, b_lo))


def _seg_sum(sq, segm):
    return _dot(sq.astype(BF16), segm)


def _sigmoid(x):
    return 1.0 / (1.0 + jnp.exp(-x))


def _softplus(x):
    return jnp.maximum(x, 0.0) + jnp.log1p(jnp.exp(-jnp.abs(x)))


def _log_sigmoid(x):
    return -_softplus(-x)


def _lane_cumsum(x, seg):
    lane = lax.broadcasted_iota(jnp.int32, x.shape, 1)
    pos = lane & (seg - 1)
    s = 1
    while s < seg:
        x = x + jnp.where(pos >= s, pltpu.roll(x, s, 1), 0.0)
        s *= 2
    return x


def _seg_matrix(n):
    i = jnp.arange(n) // HEAD_DIM
    return (i[:, None] == i[None, :]).astype(BF16)


_W_ROWS = 128


def _regroup_w_in(win_ref, wbf_ref):
    def body(r, carry):
        rows = pl.ds(pl.multiple_of(r * _W_ROWS, _W_ROWS), _W_ROWS)
        for i, g in enumerate(_WIDE):
            wbf_ref[rows, _WIDE_OFF[i]:_WIDE_OFF[i + 1]] = (
                win_ref[0, rows, _OFFS[g]:_OFFS[g + 1]].astype(BF16))
        small = _WIDE_OFF[-1]
        wbf_ref[rows, small:small + LANES] = jnp.zeros((_W_ROWS, LANES), BF16)
        for pos, g in ((_SM_F, 3), (_SM_A, 6), (_SM_B, 7), (_SM_A1, 12)):
            wbf_ref[rows, small + pos:small + pos + _SIZES[g]] = (
                win_ref[0, rows, _OFFS[g]:_OFFS[g + 1]].astype(BF16))
        return carry

    lax.fori_loop(0, win_ref.shape[1] // _W_ROWS, body, 0)


def _proj_kernel(x_ref, gain_ref, win_ref, segm_ref, qg_ref, kg_ref,
                 fb_ref, alog_ref, dtb_ref, wa2_ref, ba_ref,
                 fq_o, ka_o, fv_o, gqkv_o, ggate_o, lq_o, lk_o, lv_o, lr_o, la_o,
                 g_o, beta_o, carry_ref, wbf_ref):
    tm = x_ref.shape[1]

    @pl.when((pl.program_id(0) == 0) & (pl.program_id(1) == 0))
    def _():
        _regroup_w_in(win_ref, wbf_ref)

    x = x_ref[0]
    ms = jnp.mean(x * x, axis=-1, keepdims=True)
    hb = (x * lax.rsqrt(ms + RMS_EPS) * gain_ref[...]).astype(BF16)

    def wide(i):
        return _dot(hb, wbf_ref[:, _WIDE_OFF[i]:_WIDE_OFF[i + 1]])

    segm = segm_ref[...]
    q = wide(0)
    q = q * lax.rsqrt(_seg_sum(q * q, segm) * (1.0 / HEAD_DIM) + RMS_EPS) * qg_ref[...]
    fq_o[0] = q.astype(BF16).T
    k = wide(1)
    k = k * lax.rsqrt(_seg_sum(k * k, segm) * (1.0 / HEAD_DIM) + RMS_EPS) * kg_ref[...]
    fv_o[0] = wide(2).astype(BF16).T
    gqkv_o[0] = wide(3).astype(BF16)
    ggate_o[0] = wide(4).astype(BF16)
    lq_o[0] = wide(5).astype(BF16)
    lk_o[0] = wide(6).astype(BF16)
    lv_o[0] = wide(7).astype(BF16)
    lr_o[0] = wide(8).astype(BF16)

    sm = _dot(hb, wbf_ref[:, _WIDE_OFF[-1]:_WIDE_OFF[-1] + LANES])
    la_logit = _dot_split(sm, wa2_ref[...]) + ba_ref[...]
    la_o[0] = _log_sigmoid(la_logit) * (1.0 / GLA_TAU)

    smt = sm.T
    log_f = _log_sigmoid(smt[_SM_F:_SM_F + 8] + fb_ref[...])

    @pl.when(pl.program_id(1) == 0)
    def _():
        carry_ref[...] = jnp.zeros_like(carry_ref)

    cum = _lane_cumsum(log_f, tm) + carry_ref[:, 0:1]
    carry_ref[...] = jnp.broadcast_to(cum[:, tm - 1:tm], carry_ref.shape)
    pieces = []
    rem = cum * _LOG2E
    for _ in range(_C_PIECES):
        piece = rem.astype(BF16).astype(F32)
        pieces.append(piece)
        rem = rem - piece
    zero_row = jnp.zeros((1, tm), F32)
    rows = []
    for h in range(H_FOX):
        rows += [piece[h:h + 1] for piece in pieces] + [zero_row]
    rows.append(jnp.zeros((LANES - 4 * H_FOX, tm), F32))
    ptw = jnp.concatenate(rows, axis=0).T
    lane = lax.broadcasted_iota(jnp.int32, (tm, LANES), 1)
    for h in range(H_FOX):
        kp = k[:, (h // 2) * LANES:(h // 2 + 1) * LANES]
        if h % 2:
            kp = pltpu.roll(kp, HEAD_DIM, 1)
        cp = pltpu.roll(ptw, HEAD_DIM - 4 * h, 1)
        ka = jnp.where(lane < HEAD_DIM, kp, jnp.where(lane < HEAD_DIM + 4, cp, 0.0))
        ka_o[0, h] = ka.astype(BF16)
    g_o[0] = -jnp.exp(alog_ref[...]) * _softplus(smt[_SM_A:_SM_A + 8] + dtb_ref[...])
    beta_o[0] = _sigmoid(smt[_SM_B:_SM_B + 8])


def _proj_call(x, gain, w_in_all, layer, segm, qg, kg, fb, alog, dtb, wa2, ba, tm):
    b, s, d = x.shape
    tm = min(tm, s)
    const = lambda shape: pl.BlockSpec(shape, lambda i, j: (0,) * len(shape),
                                       pipeline_mode=pl.Buffered(1))
    tok = lambda w: pl.BlockSpec((1, tm, w), lambda i, j: (i, j, 0))
    row = pl.BlockSpec((1, 8, tm), lambda i, j: (i, 0, j))
    widths = [_SIZES[g] for g in _WIDE]
    out_shape = ([jax.ShapeDtypeStruct((b, s, w), BF16) for w in widths]
                 + [jax.ShapeDtypeStruct((b, s, LANES), F32)]
                 + [jax.ShapeDtypeStruct((b, 8, s), F32)] * 2)
    out_specs = [tok(w) for w in widths] + [tok(LANES)] + [row] * 2
    for i in (0, 2):
        out_shape[i] = jax.ShapeDtypeStruct((b, widths[i], s), BF16)
        out_specs[i] = pl.BlockSpec((1, widths[i], tm), lambda i, j: (i, 0, j))
    out_shape[1] = jax.ShapeDtypeStruct((b, H_FOX, s, LANES), BF16)
    out_specs[1] = pl.BlockSpec((1, H_FOX, tm, LANES), lambda i, j: (i, 0, j, 0))
    return pl.pallas_call(
        _proj_kernel,
        grid=(b, s // tm),
        in_specs=[tok(d), const((1, d)),
                  pl.BlockSpec((1,) + w_in_all.shape[1:], lambda i, j: (layer, 0, 0),
                               pipeline_mode=pl.Buffered(1)),
                  const(segm.shape),
                  const(qg.shape), const(kg.shape), const((8, 1)), const((8, 1)), const((8, 1)),
                  const(wa2.shape), const(ba.shape)],
        out_specs=out_specs,
        out_shape=out_shape,
        scratch_shapes=[pltpu.VMEM((8, LANES), F32),
                        pltpu.VMEM((d, _WIDE_OFF[-1] + LANES), BF16)],
        compiler_params=pltpu.CompilerParams(
            dimension_semantics=("arbitrary", "arbitrary"), vmem_limit_bytes=VMEM_LIMIT),
        name="proj",
    )(x, gain, w_in_all, segm, qg, kg, fb, alog, dtb, wa2, ba)


_ACC_ROWS = HEAD_DIM + 16


def _fox_kernel(qt_ref, k_ref, vt_ref, aug_ref, ones_ref, o_ref, m_ref, acc_ref):
    tq = qt_ref.shape[2]
    qi = pl.program_id(1)
    heads = range(H_FOX)
    aug = aug_ref[...]
    qts = [jnp.concatenate([qt_ref[0, h * HEAD_DIM:(h + 1) * HEAD_DIM, :], aug], axis=0)
           for h in heads]
    m_ref[...] = jnp.full_like(m_ref, -jnp.inf)
    acc_ref[...] = jnp.zeros_like(acc_ref)
    krow = lax.broadcasted_iota(jnp.int32, (tq, tq), 0)
    qcol = lax.broadcasted_iota(jnp.int32, (tq, tq), 1)
    ones = ones_ref[...]

    def step(blocks, masked):
        k0s = [pl.multiple_of(j * tq, tq) for j in blocks]
        ss = [[_dot(k_ref[0, h, pl.ds(k0, tq), :], qts[h]) for k0 in k0s]
              for h in heads]
        if masked:
            ss = [[jnp.where(krow <= qcol, s, -jnp.inf) for s in sh] for sh in ss]
        m_prev = [m_ref[h] for h in heads]
        m_new = []
        for h in heads:
            m = m_prev[h]
            for s in ss[h]:
                m = jnp.maximum(m, jnp.max(s, axis=0, keepdims=True))
            m_new.append(m)
        ps = [[jnp.exp2(s - m_new[h]).astype(BF16) for s in ss[h]] for h in heads]
        for h in heads:
            acc = jnp.exp2(m_prev[h] - m_new[h]) * acc_ref[h]
            for k0, p in zip(k0s, ps[h]):
                vta = jnp.concatenate(
                    [vt_ref[0, h * HEAD_DIM:(h + 1) * HEAD_DIM, pl.ds(k0, tq)], ones], axis=0)
                acc = acc + _dot(vta, p)
            acc_ref[h] = acc
            m_ref[h] = m_new[h]

    step([qi], True)

    def body(j2, carry):
        step([2 * j2, 2 * j2 + 1], False)
        return carry

    lax.fori_loop(0, qi // 2, body, 0)

    @pl.when(qi % 2 == 1)
    def _():
        step([qi - 1], False)
    outs = []
    for h in heads:
        acc = acc_ref[h]
        outs.append(acc[0:HEAD_DIM] / acc[HEAD_DIM:HEAD_DIM + 1])
    o_ref[0] = jnp.concatenate(outs, axis=0).T.astype(o_ref.dtype)


def _fox_call(qt, ka, vt, tq):
    b, _, s = qt.shape
    tq = min(tq, s)
    aug = jnp.broadcast_to(jnp.where(jnp.arange(HEAD_DIM)[:, None] < _C_PIECES, -1.0, 0.0),
                           (HEAD_DIM, tq)).astype(BF16)
    ones = jnp.broadcast_to(jnp.where(jnp.arange(16)[:, None] == 0, 1.0, 0.0), (16, tq)).astype(BF16)
    return pl.pallas_call(
        _fox_kernel,
        grid=(b, s // tq),
        in_specs=[pl.BlockSpec((1, FOX_DIM, tq), lambda i, j: (i, 0, j)),
                  pl.BlockSpec((1, H_FOX, s, LANES), lambda i, j: (i, 0, 0, 0)),
                  pl.BlockSpec((1, FOX_DIM, s), lambda i, j: (i, 0, 0)),
                  pl.BlockSpec(aug.shape, lambda i, j: (0, 0)),
                  pl.BlockSpec(ones.shape, lambda i, j: (0, 0))],
        out_specs=pl.BlockSpec((1, tq, FOX_DIM), lambda i, j: (i, j, 0)),
        out_shape=jax.ShapeDtypeStruct((b, s, FOX_DIM), BF16),
        scratch_shapes=[pltpu.VMEM((H_FOX, 1, tq), F32),
                        pltpu.VMEM((H_FOX, _ACC_ROWS, tq), F32)],
        compiler_params=pltpu.CompilerParams(
            dimension_semantics=("arbitrary", "arbitrary"),
            vmem_limit_bytes=VMEM_LIMIT),
        name="fox",
    )(qt, ka, vt, aug, ones)


def _bmm(a, b):
    return jnp.einsum('gmk,gkn->gmn', a.astype(BF16), b.astype(BF16),
                      preferred_element_type=F32)


def _pair_diag(x):
    xb = x.astype(BF16)
    low = lax.broadcasted_iota(jnp.int32, (1,) + xb.shape[1:], 2) < HEAD_DIM
    zero = jnp.zeros_like(xb)
    return jnp.concatenate([jnp.where(low, xb, zero), jnp.where(low, zero, xb)], axis=1)


def _unit_lower_inverse(a_strict, eye, between):
    n = -a_strict
    t = eye + n
    nd = _pair_diag(n)
    size, step = 1, 0
    while 2 * size < CHUNK:
        n = _bmm(n, nd)
        between(step)
        nd = _pair_diag(n)
        t = t + _bmm(t, nd)
        size *= 2
        step += 1
    return t


def _gdn_reset(state_ref, xext_ref):
    state_ref[...] = jnp.zeros_like(state_ref)
    xext_ref[0:SUBLANES, :] = jnp.zeros((SUBLANES, xext_ref.shape[1]), F32)


def _gdn_body(qkv_ref, convw_ref, g_ref, beta_ref, segm_ref, o_ref, state_ref, xext_ref, between):
    tc = qkv_ref.shape[1]
    pad = SUBLANES
    x = qkv_ref[0].astype(F32)
    xext_ref[pad:pad + tc, :] = x
    y = convw_ref[0:1, :] * xext_ref[pad - 3:pad - 3 + tc, :]
    for i in range(1, CONV_K):
        y = y + convw_ref[i:i + 1, :] * xext_ref[pad - 3 + i:pad - 3 + i + tc, :]
    xext_ref[0:pad, :] = x[tc - pad:tc, :]
    y = y * _sigmoid(y)

    segm = segm_ref[...]
    q = y[:, 0:GDN_DIM]
    k = y[:, GDN_DIM:2 * GDN_DIM]
    v = y[:, 2 * GDN_DIM:3 * GDN_DIM]
    q = q * lax.rsqrt(_seg_sum(q * q, segm) + RMS_EPS) * (HEAD_DIM ** -0.5)
    k = k * lax.rsqrt(_seg_sum(k * k, segm) + RMS_EPS)

    kt = k.T

    dec_row = _lane_cumsum(g_ref[0], CHUNK)
    dec_col = dec_row.T
    beta_col = beta_ref[0].T

    head_of_lane = lax.broadcasted_iota(jnp.int32, (tc, GDN_DIM), 1) // HEAD_DIM
    dexp = jnp.zeros((tc, GDN_DIM), F32)
    bexp = jnp.zeros((tc, GDN_DIM), F32)
    for h in range(H_GDN):
        dexp = jnp.where(head_of_lane == h, dec_col[:, h:h + 1], dexp)
        bexp = jnp.where(head_of_lane == h, beta_col[:, h:h + 1], bexp)
    edec = jnp.exp(dexp)
    kb = k * bexp
    vb = v * bexp
    kbe = kb * edec
    qd = q * edec

    nc = tc // CHUNK
    npair = H_GDN // 2
    index = [(c, p) for c in range(nc) for p in range(npair)]

    def split(a):
        return jnp.stack([a[c * CHUNK:(c + 1) * CHUNK, p * LANES:(p + 1) * LANES]
                          for c, p in index])

    q3, kb3, vb3, kbe3, qd3, dcol = (split(a) for a in (q, kb, vb, kbe, qd, dexp))
    drow = jnp.stack([jnp.concatenate(
        [dec_row[2 * p + hh:2 * p + hh + 1, c * CHUNK:(c + 1) * CHUNK] for hh in range(2)], axis=1)
        for c, p in index])
    dlast = dcol[:, CHUNK - 1:CHUNK, :]

    low = lax.broadcasted_iota(jnp.int32, (CHUNK, LANES), 1) < HEAD_DIM
    kdt_list, kd_list = [], []
    for c, p in index:
        tile = kt[p * LANES:(p + 1) * LANES, (c // 2) * LANES:(c // 2 + 1) * LANES]
        swapped = pltpu.roll(tile, HEAD_DIM, 1)
        top = (tile if c % 2 == 0 else swapped)[0:HEAD_DIM]
        bot = (swapped if c % 2 == 0 else tile)[HEAD_DIM:]
        kdt_list.append(jnp.where(low, top, bot))
        kd_list.append(jnp.concatenate([jnp.where(low, top, 0.0), jnp.where(low, 0.0, bot)],
                                       axis=0).astype(BF16))
    kt3 = jnp.stack(kdt_list)
    ktd = jnp.stack(kd_list)

    ri = lax.broadcasted_iota(jnp.int32, (1, CHUNK, LANES), 1)
    cj = lax.broadcasted_iota(jnp.int32, (1, CHUNK, LANES), 2) & (HEAD_DIM - 1)
    causal = cj <= ri
    eye = (cj == ri).astype(F32)
    gamma = jnp.where(causal, jnp.exp(jnp.where(causal, dcol - drow, 0.0)), 0.0)
    a = jnp.where(cj < ri, _bmm(kb3, ktd) * gamma, 0.0)
    t = _unit_lower_inverse(a, eye, between)
    u3 = _bmm(t, _pair_diag(vb3))
    w3 = _bmm(t, _pair_diag(kbe3))
    intra = _bmm(q3, ktd) * gamma
    kdt = kt3 * jnp.exp(dlast - drow)
    elast = jnp.exp(dlast)

    s = state_ref[...]
    for c in range(nc):
        sl = slice(c * npair, (c + 1) * npair)
        sd = _pair_diag(s)
        v_new = u3[sl] - _bmm(w3[sl], sd)
        vd = _pair_diag(v_new)
        o = _bmm(qd3[sl], sd) + _bmm(intra[sl], vd)
        s = s * elast[sl] + _bmm(kdt[sl], vd)
        o_ref[0, c * CHUNK:(c + 1) * CHUNK, :] = jnp.concatenate(
            [o[p] for p in range(npair)], axis=1).astype(o_ref.dtype)
    state_ref[...] = s


def _gla_chunks(chunks, q_ref, k_ref, v_ref, la_ref, tri_ref, mh_ref, mt_ref, o_ref,
                st_ref, kf_all, bq_all, vf_all):
    row = lax.broadcasted_iota(jnp.int32, (CHUNK, GLA_K_DIM), 0)
    mh = mh_ref[...]
    for c in chunks:
        kf_ref, bq_ref, vf_ref = kf_all.at[c], bq_all.at[c], vf_all.at[c]
        r0, r1 = c * CHUNK, (c + 1) * CHUNK
        bq = _dot_exact_lhs(tri_ref[...], la_ref[0, r0:r1, :])
        q = q_ref[0, r0:r1, :].astype(F32) * (GLA_DK ** -0.5)
        k = k_ref[0, r0:r1, :].astype(F32)
        v = v_ref[0, r0:r1, :].astype(F32)
        kf_ref[...] = k
        bq_ref[...] = bq
        vf_ref[...] = v
        st = st_ref[...]
        o_inter = _dot_nt((q * jnp.exp(bq)).astype(BF16), st.astype(BF16))

        group_out = []
        for g0 in range(0, CHUNK, SUBLANES):
            n = CHUNK - g0
            qg, bg, rg = q[g0:, :], bq[g0:, :], row[g0:, :]
            es = []
            for j in range(g0, g0 + SUBLANES):
                kj = kf_ref[j:j + 1, :]
                bj = bq_ref[j:j + 1, :]
                es.append(jnp.where(rg >= j, jnp.exp(bg - bj), 0.0) * (qg * kj))
            p = _dot(jnp.concatenate(es, axis=0).astype(BF16), mh)
            acc = p[0:n] * vf_ref[g0:g0 + 1, :]
            for jj in range(1, SUBLANES):
                acc = acc + p[jj * n:(jj + 1) * n] * vf_ref[g0 + jj:g0 + jj + 1, :]
            group_out.append(acc)
        pieces = []
        for r in range(0, CHUNK, SUBLANES):
            piece = o_inter[r:r + SUBLANES]
            for gi, g0 in enumerate(range(0, r + SUBLANES, SUBLANES)):
                piece = piece + group_out[gi][r - g0:r - g0 + SUBLANES]
            pieces.append(piece)
        o_ref[0, r0:r1, :] = jnp.concatenate(pieces, axis=0).astype(o_ref.dtype)
        blast = bq[CHUNK - 1:CHUNK, :]
        kd = k * jnp.exp(blast - bq)
        upd = _dot_tn(v.astype(BF16), kd.astype(BF16))
        st_ref[...] = (st * jnp.exp(blast) + upd) * mt_ref[...]


_N_GDN_IN, _N_GLA_IN = 5, 7


def _recurrent_kernel(*refs):
    gdn_in = refs[:_N_GDN_IN]
    gla_in = refs[_N_GDN_IN:_N_GDN_IN + _N_GLA_IN]
    o_gdn, o_gla = refs[_N_GDN_IN + _N_GLA_IN:_N_GDN_IN + _N_GLA_IN + 2]
    state_ref, xext_ref, st_ref, kf_ref, bq_ref, vf_ref = refs[_N_GDN_IN + _N_GLA_IN + 2:]

    @pl.when(pl.program_id(1) == 0)
    def _():
        _gdn_reset(state_ref, xext_ref)
        st_ref[...] = jnp.zeros_like(st_ref)

    n_chunks = o_gla.shape[1] // CHUNK
    done = []

    def gla_chunks(step):
        if step < n_chunks:
            _gla_chunks([step], *gla_in, o_gla, st_ref, kf_ref, bq_ref, vf_ref)
            done.append(step)

    _gdn_body(*gdn_in, o_gdn, state_ref, xext_ref, gla_chunks)
    _gla_chunks([c for c in range(n_chunks) if c not in done], *gla_in, o_gla,
                st_ref, kf_ref, bq_ref, vf_ref)


def _recurrent_call(qkv, convw, g, beta, segm, q, k, v, la, tc):
    b, s, w = qkv.shape
    tc = min(tc, s)
    tri = (jnp.arange(CHUNK)[:, None] >= jnp.arange(CHUNK)[None, :]).astype(F32)
    hk = jnp.arange(GLA_K_DIM) // GLA_DK
    hv = jnp.arange(GLA_V_DIM) // HEAD_DIM
    mh = (hk[:, None] == hv[None, :]).astype(BF16)
    mt = (hv[:, None] == hk[None, :]).astype(F32)
    tok = lambda wd: pl.BlockSpec((1, tc, wd), lambda i, j: (i, j, 0))
    row = pl.BlockSpec((1, 8, tc), lambda i, j: (i, 0, j))
    const = lambda a: pl.BlockSpec(a.shape, lambda i, j: (0, 0))
    return pl.pallas_call(
        _recurrent_kernel,
        grid=(b, s // tc),
        in_specs=[tok(w), const(convw), row, row, const(segm),
                  tok(GLA_K_DIM), tok(GLA_K_DIM), tok(GLA_V_DIM), tok(GLA_K_DIM),
                  const(tri), const(mh), const(mt)],
        out_specs=[tok(GDN_DIM), tok(GLA_V_DIM)],
        out_shape=[jax.ShapeDtypeStruct((b, s, GDN_DIM), BF16),
                   jax.ShapeDtypeStruct((b, s, GLA_V_DIM), BF16)],
        scratch_shapes=[pltpu.VMEM((H_GDN // 2, HEAD_DIM, LANES), F32),
                        pltpu.VMEM((tc + SUBLANES, w), F32),
                        pltpu.VMEM((GLA_V_DIM, GLA_K_DIM), F32),
                        pltpu.VMEM((tc // CHUNK, CHUNK, GLA_K_DIM), F32),
                        pltpu.VMEM((tc // CHUNK, CHUNK, GLA_K_DIM), F32),
                        pltpu.VMEM((tc // CHUNK, CHUNK, GLA_V_DIM), F32)],
        compiler_params=pltpu.CompilerParams(
            dimension_semantics=("arbitrary", "arbitrary"), vmem_limit_bytes=VMEM_LIMIT),
        name="gdn_gla",
    )(qkv, convw, g, beta, segm, q, k, v, la, tri, mh, mt)


def _head_norm(o_ref, segm, gain):
    o = o_ref[...].astype(F32)
    return o * lax.rsqrt(_seg_sum(o * o, segm) * (1.0 / HEAD_DIM) + RMS_EPS) * gain


def _silu(x):
    return x * _sigmoid(x)


def _out_kernel(x_ref, ofox_ref, ogdn_ref, ggate_ref, ogla_ref, lr_ref, wout_ref, segm_ref,
                gf_ref, gg_ref, gl_ref, fgain_ref, wr_ref, br_ref, tri_ref,
                x1_o, h2_o, meta_o, cnt_o, carry_ref):
    tm = x_ref.shape[0]
    segm = segm_ref[...]
    a = _head_norm(ofox_ref, segm, gf_ref[...])
    bb = _head_norm(ogdn_ref, segm, gg_ref[...]) * _silu(ggate_ref[...].astype(F32))
    cc = (_head_norm(ogla_ref, segm[:GLA_V_DIM, :GLA_V_DIM], gl_ref[...])
          * _silu(lr_ref[...].astype(F32)))
    y = (x_ref[...]
         + _dot(a.astype(BF16), wout_ref[0:FOX_DIM, :].astype(BF16))
         + _dot(bb.astype(BF16), wout_ref[FOX_DIM:FOX_DIM + GDN_DIM, :].astype(BF16))
         + _dot(cc.astype(BF16), wout_ref[FOX_DIM + GDN_DIM:, :].astype(BF16)))
    x1_o[...] = y
    ms = jnp.mean(y * y, axis=-1, keepdims=True)
    h2 = y * lax.rsqrt(ms + RMS_EPS) * fgain_ref[...]
    _to_token_tiles(h2_o, h2)

    wr_hi, wr_lo = _split(wr_ref[...])
    h2_hi, h2_lo = _split(h2)
    hi_terms = _dot(h2_hi, jnp.concatenate([wr_hi, wr_lo], axis=1))
    logits = (hi_terms[:, :LANES] + (hi_terms[:, LANES:] + _dot(h2_lo, wr_hi))
              + br_ref[...])
    lane = lax.broadcasted_iota(jnp.int32, logits.shape, 1).astype(F32)
    big = float(4 * LANES)
    ninf = -jnp.inf
    gl = jnp.where(lane < _R_GROUP + N_GROUPS, logits, ninf)
    gmax = jnp.max(gl, axis=1, keepdims=True)
    group_p = 1.0 / jnp.sum(jnp.exp(gl - gmax), axis=1, keepdims=True)
    gidx = jnp.min(jnp.where(gl == gmax, lane, big), axis=1, keepdims=True)
    elane = lane - _R_EXPERT
    group_of_lane = jnp.floor(elane * (1.0 / EXPERTS_PER_GROUP))
    in_group = (elane >= 0) & (elane < N_EXPERTS) & (group_of_lane == gidx)
    el = jnp.where(in_group, logits, ninf)
    m1 = jnp.max(el, axis=1, keepdims=True)
    i1 = jnp.min(jnp.where(el == m1, lane, big), axis=1, keepdims=True)
    el2 = jnp.where(lane == i1, ninf, el)
    m2 = jnp.max(el2, axis=1, keepdims=True)
    i2 = jnp.min(jnp.where(el2 == m2, lane, big), axis=1, keepdims=True)
    t = jnp.exp(m2 - m1)
    g1 = group_p / (1.0 + t)
    g2 = group_p * t / (1.0 + t)

    @pl.when(pl.program_id(0) == 0)
    def _():
        carry_ref[...] = jnp.zeros_like(carry_ref)

    sel = jnp.where((lane == i1) | (lane == i2), 1.0, 0.0)
    carry = carry_ref[0:1, :]
    rank = _dot(tri_ref[...], sel.astype(BF16)) + carry
    rank1 = jnp.sum(jnp.where(lane == i1, rank, 0.0), axis=1, keepdims=True)
    rank2 = jnp.sum(jnp.where(lane == i2, rank, 0.0), axis=1, keepdims=True)
    new_carry = carry + jnp.sum(sel, axis=0, keepdims=True)
    carry_ref[...] = jnp.broadcast_to(new_carry, carry_ref.shape)
    cnt_o[...] = jnp.broadcast_to(new_carry, cnt_o.shape)
    cols = [i1 - _R_EXPERT, i2 - _R_EXPERT, rank1, rank2, g1, g2]
    meta = jnp.zeros(logits.shape, F32)
    for idx, col in enumerate(cols):
        meta = jnp.where(lane == idx, col, meta)
    meta_o[...] = meta


def _out_call(x, ofox, ogdn, ggate, ogla, lr, wout, segm, gf, gg, gl, fgain, wr, br, tm):
    n, d = x.shape
    tm = min(tm, n)
    tri = (jnp.arange(tm)[:, None] > jnp.arange(tm)[None, :]).astype(BF16)
    tok = lambda w: pl.BlockSpec((tm, w), lambda i: (i, 0))
    const = lambda a: pl.BlockSpec(a.shape, lambda i: (0,) * a.ndim,
                                   pipeline_mode=pl.Buffered(1))
    return pl.pallas_call(
        _out_kernel,
        grid=(n // tm,),
        in_specs=[tok(d), tok(FOX_DIM), tok(GDN_DIM), tok(GDN_DIM), tok(GLA_V_DIM), tok(GLA_V_DIM),
                  const(wout), const(segm), const(gf), const(gg), const(gl), const(fgain),
                  const(wr), const(br), const(tri)],
        out_specs=[tok(d), pl.BlockSpec((tm * SUBLANES, LANES), lambda i: (i, 0)), tok(LANES),
                   pl.BlockSpec((8, LANES), lambda i: (0, 0))],
        out_shape=[jax.ShapeDtypeStruct((n, d), F32),
                   jax.ShapeDtypeStruct((n * SUBLANES, LANES), F32),
                   jax.ShapeDtypeStruct((n, LANES), F32), jax.ShapeDtypeStruct((8, LANES), F32)],
        scratch_shapes=[pltpu.VMEM((8, LANES), F32)],
        compiler_params=pltpu.CompilerParams(
            dimension_semantics=("arbitrary",), vmem_limit_bytes=VMEM_LIMIT),
        name="out_router",
    )(x, ofox, ogdn, ggate, ogla, lr, wout, segm, gf, gg, gl, fgain, wr, br, tri)


_DMA_UNROLL = 8


def _tile_rows(r):
    return pl.ds(pl.multiple_of(r * SUBLANES, SUBLANES), SUBLANES)


def _to_token_tiles(ref, x):
    t = x.shape[0]
    for s in range(SUBLANES):
        ref[pl.ds(s, t, stride=SUBLANES), :] = x[:, s * LANES:(s + 1) * LANES]


def _from_token_tiles(ref, t):
    return [ref[pl.ds(s, t, stride=SUBLANES), :] for s in range(SUBLANES)]


def _dispatch_kernel(zb_ref, dest_ref, h_ref, xb_ref, zero_ref, sem, zsem):
    td = h_ref.shape[0] // SUBLANES

    @pl.when(pl.program_id(0) == 0)
    def _():
        zero_ref[...] = jnp.zeros_like(zero_ref)
        rows = zero_ref.shape[0]

        def block_copy(b):
            return pltpu.make_async_copy(
                zero_ref, xb_ref.at[pl.ds(pl.multiple_of(b * rows, rows), rows), :], zsem)

        def start(b, carry):
            @pl.when(zb_ref[b] != 0)
            def _():
                block_copy(b).start()
            return carry

        def wait(b, carry):
            @pl.when(zb_ref[b] != 0)
            def _():
                block_copy(b).wait()
            return carry

        lax.fori_loop(0, zb_ref.shape[0], start, 0)
        lax.fori_loop(0, zb_ref.shape[0], wait, 0)

    def row_copy(t, d):
        return pltpu.make_async_copy(h_ref.at[_tile_rows(t), :], xb_ref.at[_tile_rows(d), :], sem)

    def issue(t, carry):
        for kk in range(TOP_K):
            row_copy(t, dest_ref[0, 0, TOP_K * t + kk]).start(priority=kk % 2)
        return carry

    lax.fori_loop(0, td, issue, 0, unroll=_DMA_UNROLL)

    def drain(t, carry):
        for kk in range(TOP_K):
            row_copy(0, 0).wait()
        return carry

    lax.fori_loop(0, td, drain, 0, unroll=_DMA_UNROLL)


def _dispatch_call(zero_block, dest, h2, tmb, td):
    n = h2.shape[0] // SUBLANES
    td = min(td, n)
    n_rows = zero_block.shape[0] * tmb
    dest3 = dest.reshape(n // td, 1, TOP_K * td)
    return pl.pallas_call(
        _dispatch_kernel,
        grid_spec=pltpu.PrefetchScalarGridSpec(
            num_scalar_prefetch=1,
            grid=(n // td,),
            in_specs=[pl.BlockSpec((1, 1, TOP_K * td), lambda i, zb: (i, 0, 0),
                                   memory_space=pltpu.SMEM),
                      pl.BlockSpec((td * SUBLANES, LANES), lambda i, zb: (i, 0))],
            out_specs=pl.BlockSpec(memory_space=pl.ANY),
            scratch_shapes=[pltpu.VMEM((tmb * SUBLANES, LANES), h2.dtype),
                            pltpu.SemaphoreType.DMA(()), pltpu.SemaphoreType.DMA(())],
        ),
        out_shape=jax.ShapeDtypeStruct((n_rows * SUBLANES, LANES), h2.dtype),
        compiler_params=pltpu.CompilerParams(
            dimension_semantics=("arbitrary",), has_side_effects=True),
        name="dispatch",
    )(zero_block, dest3, h2)


def _expert_kernel(be_ref, nu_ref, x_ref, wg_ref, wu_ref, wd_ref, y_ref, wgb_ref, wub_ref, wdb_ref):
    i = pl.program_id(0)
    used = i < nu_ref[0]

    @pl.when(used & ((i == 0) | (be_ref[i] != be_ref[jnp.maximum(i - 1, 0)])))
    def _():
        wgb_ref[...] = wg_ref[0].astype(BF16)
        wub_ref[...] = wu_ref[0].astype(BF16)
        wdb_ref[...] = wd_ref[0].astype(BF16)

    @pl.when(used)
    def _():
        tmb = x_ref.shape[0] // SUBLANES
        x = jnp.concatenate([blk.astype(BF16) for blk in _from_token_tiles(x_ref, tmb)], axis=1)
        a = _dot(x, wgb_ref[...])
        u = _dot(x, wub_ref[...])
        hmid = (_silu(a) * u).astype(BF16)
        _to_token_tiles(y_ref, _dot(hmid, wdb_ref[...]))

    @pl.when(jnp.logical_not(used))
    def _():
        y_ref[...] = jnp.zeros_like(y_ref)


def _expert_call(block_e, n_used, xb, wg, wu, wd, tmb):
    n_rows = xb.shape[0] // SUBLANES
    d = SUBLANES * LANES
    de = wg.shape[-1]
    n_blocks = n_rows // tmb

    def xmap(i, be, nu):
        return (jnp.minimum(i, jnp.maximum(nu[0] - 1, 0)), 0)

    wmap = lambda i, be, nu: (be[i], 0, 0)
    return pl.pallas_call(
        _expert_kernel,
        grid_spec=pltpu.PrefetchScalarGridSpec(
            num_scalar_prefetch=2,
            grid=(n_blocks,),
            in_specs=[pl.BlockSpec((tmb * SUBLANES, LANES), xmap),
                      pl.BlockSpec((1, d, de), wmap),
                      pl.BlockSpec((1, d, de), wmap),
                      pl.BlockSpec((1, de, d), wmap)],
            out_specs=pl.BlockSpec((tmb * SUBLANES, LANES), lambda i, be, nu: (i, 0)),
            scratch_shapes=[pltpu.VMEM((d, de), BF16), pltpu.VMEM((d, de), BF16),
                            pltpu.VMEM((de, d), BF16)],
        ),
        out_shape=jax.ShapeDtypeStruct((n_rows * SUBLANES, LANES), F32),
        compiler_params=pltpu.CompilerParams(
            dimension_semantics=("arbitrary",), vmem_limit_bytes=VMEM_LIMIT),
        name="experts",
    )(block_e, n_used, xb, wg, wu, wd)


def _combine_kernel(dest_ref, x1_ref, meta_ref, yb_ref, o_ref, buf_ref, sem):
    td = x1_ref.shape[0]

    def row_copy(t, kk, d):
        return pltpu.make_async_copy(yb_ref.at[_tile_rows(d), :], buf_ref.at[kk, _tile_rows(t), :],
                                     sem)

    def issue(t, carry):
        for kk in range(TOP_K):
            row_copy(t, kk, dest_ref[0, 0, TOP_K * t + kk]).start(priority=kk % 2)
        return carry

    lax.fori_loop(0, td, issue, 0, unroll=_DMA_UNROLL)

    def drain(t, carry):
        for kk in range(TOP_K):
            row_copy(0, kk, 0).wait()
        return carry

    lax.fori_loop(0, td, drain, 0, unroll=_DMA_UNROLL)
    meta = meta_ref[...]
    g1, g2 = meta[:, 4:5], meta[:, 5:6]
    y1 = _from_token_tiles(buf_ref.at[0], td)
    y2 = _from_token_tiles(buf_ref.at[1], td)
    for s in range(SUBLANES):
        lanes = slice(s * LANES, (s + 1) * LANES)
        o_ref[:, lanes] = x1_ref[:, lanes] + g1 * y1[s] + g2 * y2[s]


def _combine_call(dest, x1, meta, yb, td):
    n, d = x1.shape
    td = min(td, n)
    dest3 = dest.reshape(n // td, 1, TOP_K * td)
    return pl.pallas_call(
        _combine_kernel,
        grid=(n // td,),
        in_specs=[pl.BlockSpec((1, 1, TOP_K * td), lambda i: (i, 0, 0), memory_space=pltpu.SMEM),
                  pl.BlockSpec((td, d), lambda i: (i, 0)),
                  pl.BlockSpec((td, LANES), lambda i: (i, 0)),
                  pl.BlockSpec(memory_space=pl.ANY)],
        out_specs=pl.BlockSpec((td, d), lambda i: (i, 0)),
        out_shape=jax.ShapeDtypeStruct((n, d), F32),
        scratch_shapes=[pltpu.VMEM((TOP_K, td * SUBLANES, LANES), yb.dtype),
                        pltpu.SemaphoreType.DMA(())],
        compiler_params=pltpu.CompilerParams(
            dimension_semantics=("arbitrary",), vmem_limit_bytes=VMEM_LIMIT),
        name="combine",
    )(dest3, x1, meta, yb)


TM_PROJ = 512
TQ_FOX = 256
TC_RECURRENT = 256
TM_OUT = 512
TD_MOE = 256
TMB_EXPERT = 512


def _place(width, parts):
    cols, at = [], 0
    for pos, blk in parts:
        if pos > at:
            cols.append(jnp.zeros((blk.shape[0], pos - at), blk.dtype))
        cols.append(blk)
        at = pos + blk.shape[1]
    if width > at:
        cols.append(jnp.zeros((parts[0][1].shape[0], width - at), parts[0][1].dtype))
    return jnp.concatenate(cols, axis=1)


def _pad8(v):
    return jnp.zeros((8,), F32).at[:v.shape[0]].set(v.astype(F32)).reshape(8, 1)


def _token_mixer(x, attn_norm, w_in_all, layer, fox_q_norm, fox_k_norm, fox_f_bias,
                 gdn_conv, gdn_a_log, gdn_dt_bias, gla_w_a2, gla_b_a):
    b, s, d = x.shape
    wa2 = jnp.pad(gla_w_a2, ((_SM_A1, LANES - _SM_A1 - GLA_RANK), (0, 0)))
    segm = _seg_matrix(FOX_DIM)
    qg = (jnp.tile(fox_q_norm, H_FOX) * (HEAD_DIM ** -0.5 * _LOG2E)).reshape(1, FOX_DIM)
    kg = jnp.tile(fox_k_norm, H_FOX).reshape(1, FOX_DIM)
    outs = _proj_call(x, attn_norm.reshape(1, d), w_in_all, layer, segm, qg, kg,
                      _pad8(fox_f_bias), _pad8(gdn_a_log), _pad8(gdn_dt_bias),
                      wa2, gla_b_a.reshape(1, GLA_K_DIM), TM_PROJ)
    fq, fka, fv, gqkv, ggate, lq, lk, lv, lr, la, g, beta = outs
    o_fox = _fox_call(fq, fka, fv, TQ_FOX)
    o_gdn, o_gla = _recurrent_call(gqkv, gdn_conv.astype(F32), g, beta, segm,
                                   lq, lk, lv, la, TC_RECURRENT)
    return o_fox, o_gdn, ggate, o_gla, lr


def _layer(x, p, layer, w_in_all, experts):
    b, s, d = x.shape
    n = b * s
    o_fox, o_gdn, ggate, o_gla, lr = _token_mixer(
        x, p['attn_norm'], w_in_all, layer, p['fox_q_norm'], p['fox_k_norm'], p['fox_f_bias'],
        p['gdn_conv'], p['gdn_a_log'], p['gdn_dt_bias'], p['gla_w_a2'], p['gla_b_a'])
    wr = _place(LANES, [(_R_GROUP, p['w_router_group']), (_R_EXPERT, p['w_router_expert'])])
    br = _place(LANES, [(_R_GROUP, p['b_router_group'].reshape(1, -1)),
                        (_R_EXPERT, p['b_router_expert'].reshape(1, -1))])
    flat = lambda a: a.reshape(n, a.shape[-1])
    x1, h2, meta, cnt = _out_call(
        flat(x), flat(o_fox), flat(o_gdn), flat(ggate), flat(o_gla), flat(lr),
        p['w_out'], _seg_matrix(FOX_DIM),
        jnp.tile(p['fox_o_norm'], H_FOX).reshape(1, FOX_DIM),
        jnp.tile(p['gdn_o_norm'], H_GDN).reshape(1, GDN_DIM),
        jnp.tile(p['gla_o_norm'], H_GLA).reshape(1, GLA_V_DIM),
        p['ffn_norm'].reshape(1, d), wr, br, TM_OUT)

    tmb = TMB_EXPERT
    counts = cnt[0, _R_EXPERT:_R_EXPERT + N_EXPERTS].astype(jnp.int32)
    padded = (counts + tmb - 1) // tmb * tmb
    pends = jnp.cumsum(padded)
    pstarts = pends - padded
    eid = meta[:, 0:TOP_K].astype(jnp.int32)
    rank = meta[:, TOP_K:2 * TOP_K].astype(jnp.int32)
    expert_ids = jnp.arange(N_EXPERTS, dtype=jnp.int32)
    start_of = jnp.sum(jnp.where(eid[..., None] == expert_ids, pstarts, 0), axis=-1)
    dest = (start_of + rank).reshape(-1)
    n_blocks = -(-(n * TOP_K) // tmb) + N_EXPERTS
    block_start = jnp.arange(n_blocks, dtype=jnp.int32) * tmb
    block_e = jnp.minimum(jnp.sum(pends[None, :] <= block_start[:, None], axis=1),
                          N_EXPERTS - 1).astype(jnp.int32)
    n_used = (pends[-1:] // tmb).astype(jnp.int32)

    is_last = jnp.any((block_start + tmb)[:, None] == pends[None, :], axis=1)
    zero_block = (is_last | (block_start >= pends[-1])).astype(jnp.int32)
    xb = _dispatch_call(zero_block, dest, h2, tmb, TD_MOE)
    yb = _expert_call(block_e + layer * N_EXPERTS, n_used, xb, *experts, tmb)
    x2 = _combine_call(dest, x1, meta, yb, TD_MOE)
    return x2.reshape(b, s, d)


_PARAM_NAMES = ['attn_norm', 'w_in', 'fox_q_norm', 'fox_k_norm', 'fox_f_bias', 'fox_o_norm',
                'gdn_conv', 'gdn_a_log', 'gdn_dt_bias', 'gdn_o_norm',
                'gla_w_a2', 'gla_b_a', 'gla_o_norm', 'w_out',
                'ffn_norm', 'w_router_group', 'b_router_group', 'w_router_expert',
                'b_router_expert', 'w_expert_gate', 'w_expert_up', 'w_expert_down']


def kernel(x, attn_norm, w_in, fox_q_norm, fox_k_norm, fox_f_bias, fox_o_norm, gdn_conv, gdn_a_log, gdn_dt_bias, gdn_o_norm, gla_w_a2, gla_b_a, gla_o_norm, w_out, ffn_norm, w_router_group, b_router_group, w_router_expert, b_router_expert, w_expert_gate, w_expert_up, w_expert_down):
    params = dict(zip(_PARAM_NAMES, (
        attn_norm, w_in, fox_q_norm, fox_k_norm, fox_f_bias, fox_o_norm, gdn_conv, gdn_a_log,
        gdn_dt_bias, gdn_o_norm, gla_w_a2, gla_b_a, gla_o_norm, w_out, ffn_norm,
        w_router_group, b_router_group, w_router_expert, b_router_expert,
        w_expert_gate, w_expert_up, w_expert_down)))
    experts = tuple(params.pop(name).reshape((-1,) + params_shape[2:])
                    for name, params_shape in (('w_expert_gate', w_expert_gate.shape),
                                               ('w_expert_up', w_expert_up.shape),
                                               ('w_expert_down', w_expert_down.shape)))
    del params['w_in']
    for layer in range(attn_norm.shape[0]):
        x = _layer(x, {name: val[layer] for name, val in params.items()}, layer, w_in, experts)
    return x
```

```python
import functools

import jax
import jax.numpy as jnp
from jax import lax
from jax.experimental import pallas as pl
from jax.experimental.pallas import tpu as pltpu

F32 = jnp.float32
BF16 = jnp.bfloat16

HEAD_DIM = 64
H_FOX = 6
H_GDN = 6
H_GLA = 4
FOX_DIM = H_FOX * HEAD_DIM
GDN_DIM = H_GDN * HEAD_DIM
GLA_DK = 32
GLA_K_DIM = H_GLA * GLA_DK
GLA_V_DIM = H_GLA * HEAD_DIM
GLA_RANK = 16
GLA_TAU = 16.0
CONV_K = 4
CHUNK = 64
N_GROUPS = 4
EXPERTS_PER_GROUP = 8
N_EXPERTS = N_GROUPS * EXPERTS_PER_GROUP
TOP_K = 2
RMS_EPS = 1e-6

_LOG2E = 1.4426950408889634
_C_PIECES = 3

LANES = 128
SUBLANES = 8
VMEM_LIMIT = 56 * 1024 * 1024

_SIZES = [FOX_DIM, FOX_DIM, FOX_DIM, H_FOX, 3 * GDN_DIM, GDN_DIM, H_GDN, H_GDN,
          GLA_K_DIM, GLA_K_DIM, GLA_V_DIM, GLA_V_DIM, GLA_RANK]
_OFFS = [sum(_SIZES[:i]) for i in range(len(_SIZES) + 1)]
_WIDE = [0, 1, 2, 4, 5, 8, 9, 10, 11]
_WIDE_OFF = [0]
for _g in _WIDE:
    _WIDE_OFF.append(_WIDE_OFF[-1] + _SIZES[_g])
_SM_F, _SM_A, _SM_B, _SM_A1 = 0, 8, 16, 32
_R_GROUP, _R_EXPERT = 0, 32


def _dot(a, b):
    return jnp.dot(a, b, preferred_element_type=F32)


def _dot_nt(a, b):
    return lax.dot_general(a, b, (((1,), (1,)), ((), ())), preferred_element_type=F32)


def _dot_tn(a, b):
    return lax.dot_general(a, b, (((0,), (0,)), ((), ())), preferred_element_type=F32)


def _split(a):
    hi = a.astype(BF16)
    return hi, (a - hi.astype(F32)).astype(BF16)


def _dot_split(a, b):
    a_hi, a_lo = _split(a)
    b_hi, b_lo = _split(b)
    return _dot(a_hi, b_hi) + (_dot(a_hi, b_lo) + _dot(a_lo, b_hi))


def _dot_exact_lhs(a01, b):
    a16 = a01.astype(BF16)
    b_hi = b.astype(BF16)
    rem = b - b_hi.astype(F32)
    b_mid = rem.astype(BF16)
    b_lo = (rem - b_mid.astype(F32)).astype(BF16)
    return _dot(a16, b_hi) + (_dot(a16, b_mid) + _dot(a16, b_lo))


def _seg_sum(sq, segm):
    return _dot(sq.astype(BF16), segm)


def _sigmoid(x):
    return 1.0 / (1.0 + jnp.exp(-x))


def _softplus(x):
    return jnp.maximum(x, 0.0) + jnp.log1p(jnp.exp(-jnp.abs(x)))


def _log_sigmoid(x):
    return -_softplus(-x)


def _lane_cumsum(x, seg):
    lane = lax.broadcasted_iota(jnp.int32, x.shape, 1)
    pos = lane & (seg - 1)
    s = 1
    while s < seg:
        x = x + jnp.where(pos >= s, pltpu.roll(x, s, 1), 0.0)
        s *= 2
    return x


def _seg_matrix(n):
    i = jnp.arange(n) // HEAD_DIM
    return (i[:, None] == i[None, :]).astype(BF16)


_W_ROWS = 128


def _regroup_w_in(win_ref, wbf_ref):
    def body(r, carry):
        rows = pl.ds(pl.multiple_of(r * _W_ROWS, _W_ROWS), _W_ROWS)
        for i, g in enumerate(_WIDE):
            wbf_ref[rows, _WIDE_OFF[i]:_WIDE_OFF[i + 1]] = (
                win_ref[0, rows, _OFFS[g]:_OFFS[g + 1]].astype(BF16))
        small = _WIDE_OFF[-1]
        wbf_ref[rows, small:small + LANES] = jnp.zeros((_W_ROWS, LANES), BF16)
        for pos, g in ((_SM_F, 3), (_SM_A, 6), (_SM_B, 7), (_SM_A1, 12)):
            wbf_ref[rows, small + pos:small + pos + _SIZES[g]] = (
                win_ref[0, rows, _OFFS[g]:_OFFS[g + 1]].astype(BF16))
        return carry

    lax.fori_loop(0, win_ref.shape[1] // _W_ROWS, body, 0)


def _proj_kernel(x_ref, gain_ref, win_ref, segm_ref, qg_ref, kg_ref,
                 fb_ref, alog_ref, dtb_ref, wa2_ref, ba_ref,
                 fq_o, ka_o, fv_o, gqkv_o, ggate_o, lq_o, lk_o, lv_o, lr_o, la_o,
                 g_o, beta_o, carry_ref, wbf_ref):
    tm = x_ref.shape[1]

    @pl.when((pl.program_id(0) == 0) & (pl.program_id(1) == 0))
    def _():
        _regroup_w_in(win_ref, wbf_ref)

    x = x_ref[0]
    ms = jnp.mean(x * x, axis=-1, keepdims=True)
    hb = (x * lax.rsqrt(ms + RMS_EPS) * gain_ref[...]).astype(BF16)

    def wide(i):
        return _dot(hb, wbf_ref[:, _WIDE_OFF[i]:_WIDE_OFF[i + 1]])

    segm = segm_ref[...]
    q = wide(0)
    q = q * lax.rsqrt(_seg_sum(q * q, segm) * (1.0 / HEAD_DIM) + RMS_EPS) * qg_ref[...]
    fq_o[0] = q.astype(BF16).T
    k = wide(1)
    k = k * lax.rsqrt(_seg_sum(k * k, segm) * (1.0 / HEAD_DIM) + RMS_EPS) * kg_ref[...]
    fv_o[0] = wide(2).astype(BF16).T
    gqkv_o[0] = wide(3).astype(BF16)
    ggate_o[0] = wide(4).astype(BF16)
    lq_o[0] = wide(5).astype(BF16)
    lk_o[0] = wide(6).astype(BF16)
    lv_o[0] = wide(7).astype(BF16)
    lr_o[0] = wide(8).astype(BF16)

    sm = _dot(hb, wbf_ref[:, _WIDE_OFF[-1]:_WIDE_OFF[-1] + LANES])
    la_logit = _dot_split(sm, wa2_ref[...]) + ba_ref[...]
    la_o[0] = _log_sigmoid(la_logit) * (1.0 / GLA_TAU)

    smt = sm.T
    log_f = _log_sigmoid(smt[_SM_F:_SM_F + 8] + fb_ref[...])

    @pl.when(pl.program_id(1) == 0)
    def _():
        carry_ref[...] = jnp.zeros_like(carry_ref)

    cum = _lane_cumsum(log_f, tm) + carry_ref[:, 0:1]
    carry_ref[...] = jnp.broadcast_to(cum[:, tm - 1:tm], carry_ref.shape)
    pieces = []
    rem = cum * _LOG2E
    for _ in range(_C_PIECES):
        piece = rem.astype(BF16).astype(F32)
        pieces.append(piece)
        rem = rem - piece
    zero_row = jnp.zeros((1, tm), F32)
    rows = []
    for h in range(H_FOX):
        rows += [piece[h:h + 1] for piece in pieces] + [zero_row]
    rows.append(jnp.zeros((LANES - 4 * H_FOX, tm), F32))
    ptw = jnp.concatenate(rows, axis=0).T
    lane = lax.broadcasted_iota(jnp.int32, (tm, LANES), 1)
    for h in range(H_FOX):
        kp = k[:, (h // 2) * LANES:(h // 2 + 1) * LANES]
        if h % 2:
            kp = pltpu.roll(kp, HEAD_DIM, 1)
        cp = pltpu.roll(ptw, HEAD_DIM - 4 * h, 1)
        ka = jnp.where(lane < HEAD_DIM, kp, jnp.where(lane < HEAD_DIM + 4, cp, 0.0))
        ka_o[0, h] = ka.astype(BF16)
    g_o[0] = -jnp.exp(alog_ref[...]) * _softplus(smt[_SM_A:_SM_A + 8] + dtb_ref[...])
    beta_o[0] = _sigmoid(smt[_SM_B:_SM_B + 8])


def _proj_call(x, gain, w_in_all, layer, segm, qg, kg, fb, alog, dtb, wa2, ba, tm):
    b, s, d = x.shape
    tm = min(tm, s)
    const = lambda shape: pl.BlockSpec(shape, lambda i, j: (0,) * len(shape),
                                       pipeline_mode=pl.Buffered(1))
    tok = lambda w: pl.BlockSpec((1, tm, w), lambda i, j: (i, j, 0))
    row = pl.BlockSpec((1, 8, tm), lambda i, j: (i, 0, j))
    widths = [_SIZES[g] for g in _WIDE]
    out_shape = ([jax.ShapeDtypeStruct((b, s, w), BF16) for w in widths]
                 + [jax.ShapeDtypeStruct((b, s, LANES), F32)]
                 + [jax.ShapeDtypeStruct((b, 8, s), F32)] * 2)
    out_specs = [tok(w) for w in widths] + [tok(LANES)] + [row] * 2
    for i in (0, 2):
        out_shape[i] = jax.ShapeDtypeStruct((b, widths[i], s), BF16)
        out_specs[i] = pl.BlockSpec((1, widths[i], tm), lambda i, j: (i, 0, j))
    out_shape[1] = jax.ShapeDtypeStruct((b, H_FOX, s, LANES), BF16)
    out_specs[1] = pl.BlockSpec((1, H_FOX, tm, LANES), lambda i, j: (i, 0, j, 0))
    return pl.pallas_call(
        _proj_kernel,
        grid=(b, s // tm),
        in_specs=[tok(d), const((1, d)),
                  pl.BlockSpec((1,) + w_in_all.shape[1:], lambda i, j: (layer, 0, 0),
                               pipeline_mode=pl.Buffered(1)),
                  const(segm.shape),
                  const(qg.shape), const(kg.shape), const((8, 1)), const((8, 1)), const((8, 1)),
                  const(wa2.shape), const(ba.shape)],
        out_specs=out_specs,
        out_shape=out_shape,
        scratch_shapes=[pltpu.VMEM((8, LANES), F32),
                        pltpu.VMEM((d, _WIDE_OFF[-1] + LANES), BF16)],
        compiler_params=pltpu.CompilerParams(
            dimension_semantics=("arbitrary", "arbitrary"), vmem_limit_bytes=VMEM_LIMIT),
        name="proj",
    )(x, gain, w_in_all, segm, qg, kg, fb, alog, dtb, wa2, ba)


_ACC_ROWS = HEAD_DIM + 16


def _fox_kernel(qt_ref, k_ref, vt_ref, aug_ref, ones_ref, o_ref, m_ref, acc_ref):
    tq = qt_ref.shape[2]
    qi = pl.program_id(1)
    heads = range(H_FOX)
    aug = aug_ref[...]
    qts = [jnp.concatenate([qt_ref[0, h * HEAD_DIM:(h + 1) * HEAD_DIM, :], aug], axis=0)
           for h in heads]
    m_ref[...] = jnp.full_like(m_ref, -jnp.inf)
    acc_ref[...] = jnp.zeros_like(acc_ref)
    krow = lax.broadcasted_iota(jnp.int32, (tq, tq), 0)
    qcol = lax.broadcasted_iota(jnp.int32, (tq, tq), 1)
    ones = ones_ref[...]

    def step(blocks, masked):
        k0s = [pl.multiple_of(j * tq, tq) for j in blocks]
        ss = [[_dot(k_ref[0, h, pl.ds(k0, tq), :], qts[h]) for k0 in k0s]
              for h in heads]
        if masked:
            ss = [[jnp.where(krow <= qcol, s, -jnp.inf) for s in sh] for sh in ss]
        m_prev = [m_ref[h] for h in heads]
        m_new = []
        for h in heads:
            m = m_prev[h]
            for s in ss[h]:
                m = jnp.maximum(m, jnp.max(s, axis=0, keepdims=True))
            m_new.append(m)
        ps = [[jnp.exp2(s - m_new[h]).astype(BF16) for s in ss[h]] for h in heads]
        for h in heads:
            acc = jnp.exp2(m_prev[h] - m_new[h]) * acc_ref[h]
            for k0, p in zip(k0s, ps[h]):
                vta = jnp.concatenate(
                    [vt_ref[0, h * HEAD_DIM:(h + 1) * HEAD_DIM, pl.ds(k0, tq)], ones], axis=0)
                acc = acc + _dot(vta, p)
            acc_ref[h] = acc
            m_ref[h] = m_new[h]

    step([qi], True)

    def body(j2, carry):
        step([2 * j2, 2 * j2 + 1], False)
        return carry

    lax.fori_loop(0, qi // 2, body, 0)

    @pl.when(qi % 2 == 1)
    def _():
        step([qi - 1], False)
    outs = []
    for h in heads:
        acc = acc_ref[h]
        outs.append(acc[0:HEAD_DIM] / acc[HEAD_DIM:HEAD_DIM + 1])
    o_ref[0] = jnp.concatenate(outs, axis=0).T.astype(o_ref.dtype)


def _fox_call(qt, ka, vt, tq):
    b, _, s = qt.shape
    tq = min(tq, s)
    aug = jnp.broadcast_to(jnp.where(jnp.arange(HEAD_DIM)[:, None] < _C_PIECES, -1.0, 0.0),
                           (HEAD_DIM, tq)).astype(BF16)
    ones = jnp.broadcast_to(jnp.where(jnp.arange(16)[:, None] == 0, 1.0, 0.0), (16, tq)).astype(BF16)
    return pl.pallas_call(
        _fox_kernel,
        grid=(b, s // tq),
        in_specs=[pl.BlockSpec((1, FOX_DIM, tq), lambda i, j: (i, 0, j)),
                  pl.BlockSpec((1, H_FOX, s, LANES), lambda i, j: (i, 0, 0, 0)),
                  pl.BlockSpec((1, FOX_DIM, s), lambda i, j: (i, 0, 0)),
                  pl.BlockSpec(aug.shape, lambda i, j: (0, 0)),
                  pl.BlockSpec(ones.shape, lambda i, j: (0, 0))],
        out_specs=pl.BlockSpec((1, tq, FOX_DIM), lambda i, j: (i, j, 0)),
        out_shape=jax.ShapeDtypeStruct((b, s, FOX_DIM), BF16),
        scratch_shapes=[pltpu.VMEM((H_FOX, 1, tq), F32),
                        pltpu.VMEM((H_FOX, _ACC_ROWS, tq), F32)],
        compiler_params=pltpu.CompilerParams(
            dimension_semantics=("arbitrary", "arbitrary"),
            vmem_limit_bytes=VMEM_LIMIT),
        name="fox",
    )(qt, ka, vt, aug, ones)


def _bmm(a, b):
    return jnp.einsum('gmk,gkn->gmn', a.astype(BF16), b.astype(BF16),
                      preferred_element_type=F32)


def _pair_diag(x):
    xb = x.astype(BF16)
    low = lax.broadcasted_iota(jnp.int32, (1,) + xb.shape[1:], 2) < HEAD_DIM
    zero = jnp.zeros_like(xb)
    return jnp.concatenate([jnp.where(low, xb, zero), jnp.where(low, zero, xb)], axis=1)


def _unit_lower_inverse(a_strict, eye, between):
    n = -a_strict
    t = eye + n
    nd = _pair_diag(n)
    size, step = 1, 0
    while 2 * size < CHUNK:
        n = _bmm(n, nd)
        between(step)
        nd = _pair_diag(n)
        t = t + _bmm(t, nd)
        size *= 2
        step += 1
    return t


def _gdn_reset(state_ref, xext_ref):
    state_ref[...] = jnp.zeros_like(state_ref)
    xext_ref[0:SUBLANES, :] = jnp.zeros((SUBLANES, xext_ref.shape[1]), F32)


def _gdn_body(qkv_ref, convw_ref, g_ref, beta_ref, segm_ref, o_ref, state_ref, xext_ref, between):
    tc = qkv_ref.shape[1]
    pad = SUBLANES
    x = qkv_ref[0].astype(F32)
    xext_ref[pad:pad + tc, :] = x
    y = convw_ref[0:1, :] * xext_ref[pad - 3:pad - 3 + tc, :]
    for i in range(1, CONV_K):
        y = y + convw_ref[i:i + 1, :] * xext_ref[pad - 3 + i:pad - 3 + i + tc, :]
    xext_ref[0:pad, :] = x[tc - pad:tc, :]
    y = y * _sigmoid(y)

    segm = segm_ref[...]
    q = y[:, 0:GDN_DIM]
    k = y[:, GDN_DIM:2 * GDN_DIM]
    v = y[:, 2 * GDN_DIM:3 * GDN_DIM]
    q = q * lax.rsqrt(_seg_sum(q * q, segm) + RMS_EPS) * (HEAD_DIM ** -0.5)
    k = k * lax.rsqrt(_seg_sum(k * k, segm) + RMS_EPS)

    kt = k.T

    dec_row = _lane_cumsum(g_ref[0], CHUNK)
    dec_col = dec_row.T
    beta_col = beta_ref[0].T

    head_of_lane = lax.broadcasted_iota(jnp.int32, (tc, GDN_DIM), 1) // HEAD_DIM
    dexp = jnp.zeros((tc, GDN_DIM), F32)
    bexp = jnp.zeros((tc, GDN_DIM), F32)
    for h in range(H_GDN):
        dexp = jnp.where(head_of_lane == h, dec_col[:, h:h + 1], dexp)
        bexp = jnp.where(head_of_lane == h, beta_col[:, h:h + 1], bexp)
    edec = jnp.exp(dexp)
    kb = k * bexp
    vb = v * bexp
    kbe = kb * edec
    qd = q * edec

    nc = tc // CHUNK
    npair = H_GDN // 2
    index = [(c, p) for c in range(nc) for p in range(npair)]

    def split(a):
        return jnp.stack([a[c * CHUNK:(c + 1) * CHUNK, p * LANES:(p + 1) * LANES]
                          for c, p in index])

    q3, kb3, vb3, kbe3, qd3, dcol = (split(a) for a in (q, kb, vb, kbe, qd, dexp))
    drow = jnp.stack([jnp.concatenate(
        [dec_row[2 * p + hh:2 * p + hh + 1, c * CHUNK:(c + 1) * CHUNK] for hh in range(2)], axis=1)
        for c, p in index])
    dlast = dcol[:, CHUNK - 1:CHUNK, :]

    low = lax.broadcasted_iota(jnp.int32, (CHUNK, LANES), 1) < HEAD_DIM
    kdt_list, kd_list = [], []
    for c, p in index:
        tile = kt[p * LANES:(p + 1) * LANES, (c // 2) * LANES:(c // 2 + 1) * LANES]
        swapped = pltpu.roll(tile, HEAD_DIM, 1)
        top = (tile if c % 2 == 0 else swapped)[0:HEAD_DIM]
        bot = (swapped if c % 2 == 0 else tile)[HEAD_DIM:]
        kdt_list.append(jnp.where(low, top, bot))
        kd_list.append(jnp.concatenate([jnp.where(low, top, 0.0), jnp.where(low, 0.0, bot)],
                                       axis=0).astype(BF16))
    kt3 = jnp.stack(kdt_list)
    ktd = jnp.stack(kd_list)

    ri = lax.broadcasted_iota(jnp.int32, (1, CHUNK, LANES), 1)
    cj = lax.broadcasted_iota(jnp.int32, (1, CHUNK, LANES), 2) & (HEAD_DIM - 1)
    causal = cj <= ri
    eye = (cj == ri).astype(F32)
    gamma = jnp.where(causal, jnp.exp(jnp.where(causal, dcol - drow, 0.0)), 0.0)
    a = jnp.where(cj < ri, _bmm(kb3, ktd) * gamma, 0.0)
    t = _unit_lower_inverse(a, eye, between)
    u3 = _bmm(t, _pair_diag(vb3))
    w3 = _bmm(t, _pair_diag(kbe3))
    intra = _bmm(q3, ktd) * gamma
    kdt = kt3 * jnp.exp(dlast - drow)
    elast = jnp.exp(dlast)

    s = state_ref[...]
    for c in range(nc):
        sl = slice(c * npair, (c + 1) * npair)
        sd = _pair_diag(s)
        v_new = u3[sl] - _bmm(w3[sl], sd)
        vd = _pair_diag(v_new)
        o = _bmm(qd3[sl], sd) + _bmm(intra[sl], vd)
        s = s * elast[sl] + _bmm(kdt[sl], vd)
        o_ref[0, c * CHUNK:(c + 1) * CHUNK, :] = jnp.concatenate(
            [o[p] for p in range(npair)], axis=1).astype(o_ref.dtype)
    state_ref[...] = s


def _gla_chunks(chunks, q_ref, k_ref, v_ref, la_ref, tri_ref, mh_ref, mt_ref, o_ref,
                st_ref, kf_all, bq_all, vf_all):
    row = lax.broadcasted_iota(jnp.int32, (CHUNK, GLA_K_DIM), 0)
    mh = mh_ref[...]
    for c in chunks:
        kf_ref, bq_ref, vf_ref = kf_all.at[c], bq_all.at[c], vf_all.at[c]
        r0, r1 = c * CHUNK, (c + 1) * CHUNK
        bq = _dot_exact_lhs(tri_ref[...], la_ref[0, r0:r1, :])
        q = q_ref[0, r0:r1, :].astype(F32) * (GLA_DK ** -0.5)
        k = k_ref[0, r0:r1, :].astype(F32)
        v = v_ref[0, r0:r1, :].astype(F32)
        kf_ref[...] = k
        bq_ref[...] = bq
        vf_ref[...] = v
        st = st_ref[...]
        o_inter = _dot_nt((q * jnp.exp(bq)).astype(BF16), st.astype(BF16))

        group_out = []
        for g0 in range(0, CHUNK, SUBLANES):
            n = CHUNK - g0
            qg, bg, rg = q[g0:, :], bq[g0:, :], row[g0:, :]
            es = []
            for j in range(g0, g0 + SUBLANES):
                kj = kf_ref[j:j + 1, :]
                bj = bq_ref[j:j + 1, :]
                es.append(jnp.where(rg >= j, jnp.exp(bg - bj), 0.0) * (qg * kj))
            p = _dot(jnp.concatenate(es, axis=0).astype(BF16), mh)
            acc = p[0:n] * vf_ref[g0:g0 + 1, :]
            for jj in range(1, SUBLANES):
                acc = acc + p[jj * n:(jj + 1) * n] * vf_ref[g0 + jj:g0 + jj + 1, :]
            group_out.append(acc)
        pieces = []
        for r in range(0, CHUNK, SUBLANES):
            piece = o_inter[r:r + SUBLANES]
            for gi, g0 in enumerate(range(0, r + SUBLANES, SUBLANES)):
                piece = piece + group_out[gi][r - g0:r - g0 + SUBLANES]
            pieces.append(piece)
        o_ref[0, r0:r1, :] = jnp.concatenate(pieces, axis=0).astype(o_ref.dtype)
        blast = bq[CHUNK - 1:CHUNK, :]
        kd = k * jnp.exp(blast - bq)
        upd = _dot_tn(v.astype(BF16), kd.astype(BF16))
        st_ref[...] = (st * jnp.exp(blast) + upd) * mt_ref[...]


_N_GDN_IN, _N_GLA_IN = 5, 7


def _recurrent_kernel(*refs):
    gdn_in = refs[:_N_GDN_IN]
    gla_in = refs[_N_GDN_IN:_N_GDN_IN + _N_GLA_IN]
    o_gdn, o_gla = refs[_N_GDN_IN + _N_GLA_IN:_N_GDN_IN + _N_GLA_IN + 2]
    state_ref, xext_ref, st_ref, kf_ref, bq_ref, vf_ref = refs[_N_GDN_IN + _N_GLA_IN + 2:]

    @pl.when(pl.program_id(1) == 0)
    def _():
        _gdn_reset(state_ref, xext_ref)
        st_ref[...] = jnp.zeros_like(st_ref)

    n_chunks = o_gla.shape[1] // CHUNK
    done = []

    def gla_chunks(step):
        if step < n_chunks:
            _gla_chunks([step], *gla_in, o_gla, st_ref, kf_ref, bq_ref, vf_ref)
            done.append(step)

    _gdn_body(*gdn_in, o_gdn, state_ref, xext_ref, gla_chunks)
    _gla_chunks([c for c in range(n_chunks) if c not in done], *gla_in, o_gla,
                st_ref, kf_ref, bq_ref, vf_ref)


def _recurrent_call(qkv, convw, g, beta, segm, q, k, v, la, tc):
    b, s, w = qkv.shape
    tc = min(tc, s)
    tri = (jnp.arange(CHUNK)[:, None] >= jnp.arange(CHUNK)[None, :]).astype(F32)
    hk = jnp.arange(GLA_K_DIM) // GLA_DK
    hv = jnp.arange(GLA_V_DIM) // HEAD_DIM
    mh = (hk[:, None] == hv[None, :]).astype(BF16)
    mt = (hv[:, None] == hk[None, :]).astype(F32)
    tok = lambda wd: pl.BlockSpec((1, tc, wd), lambda i, j: (i, j, 0))
    row = pl.BlockSpec((1, 8, tc), lambda i, j: (i, 0, j))
    const = lambda a: pl.BlockSpec(a.shape, lambda i, j: (0, 0))
    return pl.pallas_call(
        _recurrent_kernel,
        grid=(b, s // tc),
        in_specs=[tok(w), const(convw), row, row, const(segm),
                  tok(GLA_K_DIM), tok(GLA_K_DIM), tok(GLA_V_DIM), tok(GLA_K_DIM),
                  const(tri), const(mh), const(mt)],
        out_specs=[tok(GDN_DIM), tok(GLA_V_DIM)],
        out_shape=[jax.ShapeDtypeStruct((b, s, GDN_DIM), BF16),
                   jax.ShapeDtypeStruct((b, s, GLA_V_DIM), BF16)],
        scratch_shapes=[pltpu.VMEM((H_GDN // 2, HEAD_DIM, LANES), F32),
                        pltpu.VMEM((tc + SUBLANES, w), F32),
                        pltpu.VMEM((GLA_V_DIM, GLA_K_DIM), F32),
                        pltpu.VMEM((tc // CHUNK, CHUNK, GLA_K_DIM), F32),
                        pltpu.VMEM((tc // CHUNK, CHUNK, GLA_K_DIM), F32),
                        pltpu.VMEM((tc // CHUNK, CHUNK, GLA_V_DIM), F32)],
        compiler_params=pltpu.CompilerParams(
            dimension_semantics=("arbitrary", "arbitrary"), vmem_limit_bytes=VMEM_LIMIT),
        name="gdn_gla",
    )(qkv, convw, g, beta, segm, q, k, v, la, tri, mh, mt)


def _head_norm(o_ref, segm, gain):
    o = o_ref[...].astype(F32)
    return o * lax.rsqrt(_seg_sum(o * o, segm) * (1.0 / HEAD_DIM) + RMS_EPS) * gain


def _silu(x):
    return x * _sigmoid(x)


def _out_kernel(x_ref, ofox_ref, ogdn_ref, ggate_ref, ogla_ref, lr_ref, wout_ref, segm_ref,
                gf_ref, gg_ref, gl_ref, fgain_ref, wr_ref, br_ref, tri_ref,
                x1_o, h2_o, meta_o, cnt_o, carry_ref):
    tm = x_ref.shape[0]
    segm = segm_ref[...]
    a = _head_norm(ofox_ref, segm, gf_ref[...])
    bb = _head_norm(ogdn_ref, segm, gg_ref[...]) * _silu(ggate_ref[...].astype(F32))
    cc = (_head_norm(ogla_ref, segm[:GLA_V_DIM, :GLA_V_DIM], gl_ref[...])
          * _silu(lr_ref[...].astype(F32)))
    y = (x_ref[...]
         + _dot(a.astype(BF16), wout_ref[0:FOX_DIM, :].astype(BF16))
         + _dot(bb.astype(BF16), wout_ref[FOX_DIM:FOX_DIM + GDN_DIM, :].astype(BF16))
         + _dot(cc.astype(BF16), wout_ref[FOX_DIM + GDN_DIM:, :].astype(BF16)))
    x1_o[...] = y
    ms = jnp.mean(y * y, axis=-1, keepdims=True)
    h2 = y * lax.rsqrt(ms + RMS_EPS) * fgain_ref[...]
    _to_token_tiles(h2_o, h2)

    wr_hi, wr_lo = _split(wr_ref[...])
    h2_hi, h2_lo = _split(h2)
    hi_terms = _dot(h2_hi, jnp.concatenate([wr_hi, wr_lo], axis=1))
    logits = (hi_terms[:, :LANES] + (hi_terms[:, LANES:] + _dot(h2_lo, wr_hi))
              + br_ref[...])
    lane = lax.broadcasted_iota(jnp.int32, logits.shape, 1).astype(F32)
    big = float(4 * LANES)
    ninf = -jnp.inf
    gl = jnp.where(lane < _R_GROUP + N_GROUPS, logits, ninf)
    gmax = jnp.max(gl, axis=1, keepdims=True)
    group_p = 1.0 / jnp.sum(jnp.exp(gl - gmax), axis=1, keepdims=True)
    gidx = jnp.min(jnp.where(gl == gmax, lane, big), axis=1, keepdims=True)
    elane = lane - _R_EXPERT
    group_of_lane = jnp.floor(elane * (1.0 / EXPERTS_PER_GROUP))
    in_group = (elane >= 0) & (elane < N_EXPERTS) & (group_of_lane == gidx)
    el = jnp.where(in_group, logits, ninf)
    m1 = jnp.max(el, axis=1, keepdims=True)
    i1 = jnp.min(jnp.where(el == m1, lane, big), axis=1, keepdims=True)
    el2 = jnp.where(lane == i1, ninf, el)
    m2 = jnp.max(el2, axis=1, keepdims=True)
    i2 = jnp.min(jnp.where(el2 == m2, lane, big), axis=1, keepdims=True)
    t = jnp.exp(m2 - m1)
    g1 = group_p / (1.0 + t)
    g2 = group_p * t / (1.0 + t)

    @pl.when(pl.program_id(0) == 0)
    def _():
        carry_ref[...] = jnp.zeros_like(carry_ref)

    sel = jnp.where((lane == i1) | (lane == i2), 1.0, 0.0)
    carry = carry_ref[0:1, :]
    rank = _dot(tri_ref[...], sel.astype(BF16)) + carry
    rank1 = jnp.sum(jnp.where(lane == i1, rank, 0.0), axis=1, keepdims=True)
    rank2 = jnp.sum(jnp.where(lane == i2, rank, 0.0), axis=1, keepdims=True)
    new_carry = carry + jnp.sum(sel, axis=0, keepdims=True)
    carry_ref[...] = jnp.broadcast_to(new_carry, carry_ref.shape)
    cnt_o[...] = jnp.broadcast_to(new_carry, cnt_o.shape)
    cols = [i1 - _R_EXPERT, i2 - _R_EXPERT, rank1, rank2, g1, g2]
    meta = jnp.zeros(logits.shape, F32)
    for idx, col in enumerate(cols):
        meta = jnp.where(lane == idx, col, meta)
    meta_o[...] = meta


def _out_call(x, ofox, ogdn, ggate, ogla, lr, wout, segm, gf, gg, gl, fgain, wr, br, tm):
    n, d = x.shape
    tm = min(tm, n)
    tri = (jnp.arange(tm)[:, None] > jnp.arange(tm)[None, :]).astype(BF16)
    tok = lambda w: pl.BlockSpec((tm, w), lambda i: (i, 0))
    const = lambda a: pl.BlockSpec(a.shape, lambda i: (0,) * a.ndim,
                                   pipeline_mode=pl.Buffered(1))
    return pl.pallas_call(
        _out_kernel,
        grid=(n // tm,),
        in_specs=[tok(d), tok(FOX_DIM), tok(GDN_DIM), tok(GDN_DIM), tok(GLA_V_DIM), tok(GLA_V_DIM),
                  const(wout), const(segm), const(gf), const(gg), const(gl), const(fgain),
                  const(wr), const(br), const(tri)],
        out_specs=[tok(d), pl.BlockSpec((tm * SUBLANES, LANES), lambda i: (i, 0)), tok(LANES),
                   pl.BlockSpec((8, LANES), lambda i: (0, 0))],
        out_shape=[jax.ShapeDtypeStruct((n, d), F32),
                   jax.ShapeDtypeStruct((n * SUBLANES, LANES), F32),
                   jax.ShapeDtypeStruct((n, LANES), F32), jax.ShapeDtypeStruct((8, LANES), F32)],
        scratch_shapes=[pltpu.VMEM((8, LANES), F32)],
        compiler_params=pltpu.CompilerParams(
            dimension_semantics=("arbitrary",), vmem_limit_bytes=VMEM_LIMIT),
        name="out_router",
    )(x, ofox, ogdn, ggate, ogla, lr, wout, segm, gf, gg, gl, fgain, wr, br, tri)


_DMA_UNROLL = 8


def _tile_rows(r):
    return pl.ds(pl.multiple_of(r * SUBLANES, SUBLANES), SUBLANES)


def _to_token_tiles(ref, x):
    t = x.shape[0]
    for s in range(SUBLANES):
        ref[pl.ds(s, t, stride=SUBLANES), :] = x[:, s * LANES:(s + 1) * LANES]


def _from_token_tiles(ref, t):
    return [ref[pl.ds(s, t, stride=SUBLANES), :] for s in range(SUBLANES)]


def _dispatch_kernel(zb_ref, dest_ref, h_ref, xb_ref, zero_ref, sem, zsem):
    td = h_ref.shape[0] // SUBLANES

    @pl.when(pl.program_id(0) == 0)
    def _():
        zero_ref[...] = jnp.zeros_like(zero_ref)
        rows = zero_ref.shape[0]

        def block_copy(b):
            return pltpu.make_async_copy(
                zero_ref, xb_ref.at[pl.ds(pl.multiple_of(b * rows, rows), rows), :], zsem)

        def start(b, carry):
            @pl.when(zb_ref[b] != 0)
            def _():
                block_copy(b).start()
            return carry

        def wait(b, carry):
            @pl.when(zb_ref[b] != 0)
            def _():
                block_copy(b).wait()
            return carry

        lax.fori_loop(0, zb_ref.shape[0], start, 0)
        lax.fori_loop(0, zb_ref.shape[0], wait, 0)

    def row_copy(t, d):
        return pltpu.make_async_copy(h_ref.at[_tile_rows(t), :], xb_ref.at[_tile_rows(d), :], sem)

    def issue(t, carry):
        for kk in range(TOP_K):
            row_copy(t, dest_ref[0, 0, TOP_K * t + kk]).start(priority=kk % 2)
        return carry

    lax.fori_loop(0, td, issue, 0, unroll=_DMA_UNROLL)

    def drain(t, carry):
        for kk in range(TOP_K):
            row_copy(0, 0).wait()
        return carry

    lax.fori_loop(0, td, drain, 0, unroll=_DMA_UNROLL)


def _dispatch_call(zero_block, dest, h2, tmb, td):
    n = h2.shape[0] // SUBLANES
    td = min(td, n)
    n_rows = zero_block.shape[0] * tmb
    dest3 = dest.reshape(n // td, 1, TOP_K * td)
    return pl.pallas_call(
        _dispatch_kernel,
        grid_spec=pltpu.PrefetchScalarGridSpec(
            num_scalar_prefetch=1,
            grid=(n // td,),
            in_specs=[pl.BlockSpec((1, 1, TOP_K * td), lambda i, zb: (i, 0, 0),
                                   memory_space=pltpu.SMEM),
                      pl.BlockSpec((td * SUBLANES, LANES), lambda i, zb: (i, 0))],
            out_specs=pl.BlockSpec(memory_space=pl.ANY),
            scratch_shapes=[pltpu.VMEM((tmb * SUBLANES, LANES), h2.dtype),
                            pltpu.SemaphoreType.DMA(()), pltpu.SemaphoreType.DMA(())],
        ),
        out_shape=jax.ShapeDtypeStruct((n_rows * SUBLANES, LANES), h2.dtype),
        compiler_params=pltpu.CompilerParams(
            dimension_semantics=("arbitrary",), has_side_effects=True),
        name="dispatch",
    )(zero_block, dest3, h2)


def _expert_kernel(be_ref, nu_ref, x_ref, wg_ref, wu_ref, wd_ref, y_ref, wgb_ref, wub_ref, wdb_ref):
    i = pl.program_id(0)
    used = i < nu_ref[0]

    @pl.when(used & ((i == 0) | (be_ref[i] != be_ref[jnp.maximum(i - 1, 0)])))
    def _():
        wgb_ref[...] = wg_ref[0].astype(BF16)
        wub_ref[...] = wu_ref[0].astype(BF16)
        wdb_ref[...] = wd_ref[0].astype(BF16)

    @pl.when(used)
    def _():
        tmb = x_ref.shape[0] // SUBLANES
        x = jnp.concatenate([blk.astype(BF16) for blk in _from_token_tiles(x_ref, tmb)], axis=1)
        a = _dot(x, wgb_ref[...])
        u = _dot(x, wub_ref[...])
        hmid = (_silu(a) * u).astype(BF16)
        _to_token_tiles(y_ref, _dot(hmid, wdb_ref[...]))

    @pl.when(jnp.logical_not(used))
    def _():
        y_ref[...] = jnp.zeros_like(y_ref)


def _expert_call(block_e, n_used, xb, wg, wu, wd, tmb):
    n_rows = xb.shape[0] // SUBLANES
    d = SUBLANES * LANES
    de = wg.shape[-1]
    n_blocks = n_rows // tmb

    def xmap(i, be, nu):
        return (jnp.minimum(i, jnp.maximum(nu[0] - 1, 0)), 0)

    wmap = lambda i, be, nu: (be[i], 0, 0)
    return pl.pallas_call(
        _expert_kernel,
        grid_spec=pltpu.PrefetchScalarGridSpec(
            num_scalar_prefetch=2,
            grid=(n_blocks,),
            in_specs=[pl.BlockSpec((tmb * SUBLANES, LANES), xmap),
                      pl.BlockSpec((1, d, de), wmap),
                      pl.BlockSpec((1, d, de), wmap),
                      pl.BlockSpec((1, de, d), wmap)],
            out_specs=pl.BlockSpec((tmb * SUBLANES, LANES), lambda i, be, nu: (i, 0)),
            scratch_shapes=[pltpu.VMEM((d, de), BF16), pltpu.VMEM((d, de), BF16),
                            pltpu.VMEM((de, d), BF16)],
        ),
        out_shape=jax.ShapeDtypeStruct((n_rows * SUBLANES, LANES), F32),
        compiler_params=pltpu.CompilerParams(
            dimension_semantics=("arbitrary",), vmem_limit_bytes=VMEM_LIMIT),
        name="experts",
    )(block_e, n_used, xb, wg, wu, wd)


def _combine_kernel(dest_ref, x1_ref, meta_ref, yb_ref, o_ref, buf_ref, sem):
    td = x1_ref.shape[0]

    def row_copy(t, kk, d):
        return pltpu.make_async_copy(yb_ref.at[_tile_rows(d), :], buf_ref.at[kk, _tile_rows(t), :],
                                     sem)

    def issue(t, carry):
        for kk in range(TOP_K):
            row_copy(t, kk, dest_ref[0, 0, TOP_K * t + kk]).start(priority=kk % 2)
        return carry

    lax.fori_loop(0, td, issue, 0, unroll=_DMA_UNROLL)

    def drain(t, carry):
        for kk in range(TOP_K):
            row_copy(0, kk, 0).wait()
        return carry

    lax.fori_loop(0, td, drain, 0, unroll=_DMA_UNROLL)
    meta = meta_ref[...]
    g1, g2 = meta[:, 4:5], meta[:, 5:6]
    y1 = _from_token_tiles(buf_ref.at[0], td)
    y2 = _from_token_tiles(buf_ref.at[1], td)
    for s in range(SUBLANES):
        lanes = slice(s * LANES, (s + 1) * LANES)
        o_ref[:, lanes] = x1_ref[:, lanes] + g1 * y1[s] + g2 * y2[s]


def _combine_call(dest, x1, meta, yb, td):
    n, d = x1.shape
    td = min(td, n)
    dest3 = dest.reshape(n // td, 1, TOP_K * td)
    return pl.pallas_call(
        _combine_kernel,
        grid=(n // td,),
        in_specs=[pl.BlockSpec((1, 1, TOP_K * td), lambda i: (i, 0, 0), memory_space=pltpu.SMEM),
                  pl.BlockSpec((td, d), lambda i: (i, 0)),
                  pl.BlockSpec((td, LANES), lambda i: (i, 0)),
                  pl.BlockSpec(memory_space=pl.ANY)],
        out_specs=pl.BlockSpec((td, d), lambda i: (i, 0)),
        out_shape=jax.ShapeDtypeStruct((n, d), F32),
        scratch_shapes=[pltpu.VMEM((TOP_K, td * SUBLANES, LANES), yb.dtype),
                        pltpu.SemaphoreType.DMA(())],
        compiler_params=pltpu.CompilerParams(
            dimension_semantics=("arbitrary",), vmem_limit_bytes=VMEM_LIMIT),
        name="combine",
    )(dest3, x1, meta, yb)


TM_PROJ = 512
TQ_FOX = 256
TC_RECURRENT = 256
TM_OUT = 512
TD_MOE = 512
TMB_EXPERT = 512


def _place(width, parts):
    cols, at = [], 0
    for pos, blk in parts:
        if pos > at:
            cols.append(jnp.zeros((blk.shape[0], pos - at), blk.dtype))
        cols.append(blk)
        at = pos + blk.shape[1]
    if width > at:
        cols.append(jnp.zeros((parts[0][1].shape[0], width - at), parts[0][1].dtype))
    return jnp.concatenate(cols, axis=1)


def _pad8(v):
    return jnp.zeros((8,), F32).at[:v.shape[0]].set(v.astype(F32)).reshape(8, 1)


def _token_mixer(x, attn_norm, w_in_all, layer, fox_q_norm, fox_k_norm, fox_f_bias,
                 gdn_conv, gdn_a_log, gdn_dt_bias, gla_w_a2, gla_b_a):
    b, s, d = x.shape
    wa2 = jnp.pad(gla_w_a2, ((_SM_A1, LANES - _SM_A1 - GLA_RANK), (0, 0)))
    segm = _seg_matrix(FOX_DIM)
    qg = (jnp.tile(fox_q_norm, H_FOX) * (HEAD_DIM ** -0.5 * _LOG2E)).reshape(1, FOX_DIM)
    kg = jnp.tile(fox_k_norm, H_FOX).reshape(1, FOX_DIM)
    outs = _proj_call(x, attn_norm.reshape(1, d), w_in_all, layer, segm, qg, kg,
                      _pad8(fox_f_bias), _pad8(gdn_a_log), _pad8(gdn_dt_bias),
                      wa2, gla_b_a.reshape(1, GLA_K_DIM), TM_PROJ)
    fq, fka, fv, gqkv, ggate, lq, lk, lv, lr, la, g, beta = outs
    o_fox = _fox_call(fq, fka, fv, TQ_FOX)
    o_gdn, o_gla = _recurrent_call(gqkv, gdn_conv.astype(F32), g, beta, segm,
                                   lq, lk, lv, la, TC_RECURRENT)
    return o_fox, o_gdn, ggate, o_gla, lr


def _layer(x, p, layer, w_in_all, experts):
    b, s, d = x.shape
    n = b * s
    o_fox, o_gdn, ggate, o_gla, lr = _token_mixer(
        x, p['attn_norm'], w_in_all, layer, p['fox_q_norm'], p['fox_k_norm'], p['fox_f_bias'],
        p['gdn_conv'], p['gdn_a_log'], p['gdn_dt_bias'], p['gla_w_a2'], p['gla_b_a'])
    wr = _place(LANES, [(_R_GROUP, p['w_router_group']), (_R_EXPERT, p['w_router_expert'])])
    br = _place(LANES, [(_R_GROUP, p['b_router_group'].reshape(1, -1)),
                        (_R_EXPERT, p['b_router_expert'].reshape(1, -1))])
    flat = lambda a: a.reshape(n, a.shape[-1])
    x1, h2, meta, cnt = _out_call(
        flat(x), flat(o_fox), flat(o_gdn), flat(ggate), flat(o_gla), flat(lr),
        p['w_out'], _seg_matrix(FOX_DIM),
        jnp.tile(p['fox_o_norm'], H_FOX).reshape(1, FOX_DIM),
        jnp.tile(p['gdn_o_norm'], H_GDN).reshape(1, GDN_DIM),
        jnp.tile(p['gla_o_norm'], H_GLA).reshape(1, GLA_V_DIM),
        p['ffn_norm'].reshape(1, d), wr, br, TM_OUT)

    tmb = TMB_EXPERT
    counts = cnt[0, _R_EXPERT:_R_EXPERT + N_EXPERTS].astype(jnp.int32)
    padded = (counts + tmb - 1) // tmb * tmb
    pends = jnp.cumsum(padded)
    pstarts = pends - padded
    eid = meta[:, 0:TOP_K].astype(jnp.int32)
    rank = meta[:, TOP_K:2 * TOP_K].astype(jnp.int32)
    expert_ids = jnp.arange(N_EXPERTS, dtype=jnp.int32)
    start_of = jnp.sum(jnp.where(eid[..., None] == expert_ids, pstarts, 0), axis=-1)
    dest = (start_of + rank).reshape(-1)
    n_blocks = -(-(n * TOP_K) // tmb) + N_EXPERTS
    block_start = jnp.arange(n_blocks, dtype=jnp.int32) * tmb
    block_e = jnp.minimum(jnp.sum(pends[None, :] <= block_start[:, None], axis=1),
                          N_EXPERTS - 1).astype(jnp.int32)
    n_used = (pends[-1:] // tmb).astype(jnp.int32)

    is_last = jnp.any((block_start + tmb)[:, None] == pends[None, :], axis=1)
    zero_block = (is_last | (block_start >= pends[-1])).astype(jnp.int32)
    xb = _dispatch_call(zero_block, dest, h2, tmb, TD_MOE)
    yb = _expert_call(block_e + layer * N_EXPERTS, n_used, xb, *experts, tmb)
    x2 = _combine_call(dest, x1, meta, yb, TD_MOE)
    return x2.reshape(b, s, d)


_PARAM_NAMES = ['attn_norm', 'w_in', 'fox_q_norm', 'fox_k_norm', 'fox_f_bias', 'fox_o_norm',
                'gdn_conv', 'gdn_a_log', 'gdn_dt_bias', 'gdn_o_norm',
                'gla_w_a2', 'gla_b_a', 'gla_o_norm', 'w_out',
                'ffn_norm', 'w_router_group', 'b_router_group', 'w_router_expert',
                'b_router_expert', 'w_expert_gate', 'w_expert_up', 'w_expert_down']


def kernel(x, attn_norm, w_in, fox_q_norm, fox_k_norm, fox_f_bias, fox_o_norm, gdn_conv, gdn_a_log, gdn_dt_bias, gdn_o_norm, gla_w_a2, gla_b_a, gla_o_norm, w_out, ffn_norm, w_router_group, b_router_group, w_router_expert, b_router_expert, w_expert_gate, w_expert_up, w_expert_down):
    params = dict(zip(_PARAM_NAMES, (
        attn_norm, w_in, fox_q_norm, fox_k_norm, fox_f_bias, fox_o_norm, gdn_conv, gdn_a_log,
        gdn_dt_bias, gdn_o_norm, gla_w_a2, gla_b_a, gla_o_norm, w_out, ffn_norm,
        w_router_group, b_router_group, w_router_expert, b_router_expert,
        w_expert_gate, w_expert_up, w_expert_down)))
    experts = tuple(params.pop(name).reshape((-1,) + params_shape[2:])
                    for name, params_shape in (('w_expert_gate', w_expert_gate.shape),
                                               ('w_expert_up', w_expert_up.shape),
                                               ('w_expert_down', w_expert_down.shape)))
    del params['w_in']
    for layer in range(attn_norm.shape[0]):
        x = _layer(x, {name: val[layer] for name, val in params.items()}, layer, w_in, experts)
    return x
```
